```python
import math
import jax, jax.numpy as jnp
from jax import lax
import numpy as np

D_MODEL = 1024
BATCH = 8
SEQ = 2048
DEPTH = 2

CTX_LEN = 256
GRID_W = 64
N_EVEN = (DEPTH + 1) // 2
N_ODD = DEPTH // 2
N_MOD = 9
D_FF = 2816
NORM_EPS = 1e-6
NEG_INF = -1e30
ROPE_BASE = 10000.0
NA_HEADS = 8
NA_HD = 64
NA_WIN_H = 8
NA_WIN_W = 16
NA_QB_W = 16
NA_KB_W = NA_QB_W + NA_WIN_W
NA_PROJ = 3 * NA_HEADS * NA_HD
DN_HEADS = 4
DN_DK = 128
DN_DV = 128
DN_CONV = 5
DN_CHUNK = 64
DN_QKV = DN_HEADS * (2 * DN_DK + DN_DV)
DN_PROJ = DN_QKV + DN_HEADS * DN_DV + 4 * DN_HEADS
EVEN_PROJ = NA_PROJ + DN_PROJ
EVEN_MIX_W = NA_HEADS * NA_HD + DN_HEADS * DN_DV
SWA_HEADS = 16
SWA_KV_HEADS = 4
SWA_HD = 64
SWA_WINDOW = 128
SWA_BLOCK = 128
ODD_PROJ = (SWA_HEADS + 2 * SWA_KV_HEADS) * SWA_HD
ODD_MIX_W = SWA_HEADS * SWA_HD

kernel_name = "hybrid_na_deltanet_swa_prefix_dit"


def rmsnorm(x, g):
    xf = x.astype(jnp.float32)
    y = xf * lax.rsqrt(jnp.mean(xf * xf, -1, keepdims=True) + NORM_EPS)
    return (y * g.astype(jnp.float32)).astype(x.dtype)


def l2norm(x):
    xf = x.astype(jnp.float32)
    return (xf * lax.rsqrt(jnp.sum(xf * xf, -1, keepdims=True) + NORM_EPS)).astype(x.dtype)


def modulate(h, g, shift, scale):
    return rmsnorm(h, g) * (1 + scale) + shift


def adaln_terms(s, w, b):
    return jnp.split(jax.nn.silu(s) @ w + b, N_MOD, -1)


def swiglu(x, w_up, w_down):
    gate, up = jnp.split(x @ w_up, 2, -1)
    return (jax.nn.silu(gate) * up) @ w_down


def axial_rope_tables(T):
    t = jnp.arange(T)
    half = SWA_HD // 2
    inv = jnp.power(ROPE_BASE, -jnp.arange(0, half, 2, dtype=jnp.float32) / half)
    ang_r = (t // GRID_W).astype(jnp.float32)[:, None] * inv
    ang_c = (t % GRID_W).astype(jnp.float32)[:, None] * inv
    return (jnp.cos(ang_r)[:, None, :], jnp.sin(ang_r)[:, None, :],
            jnp.cos(ang_c)[:, None, :], jnp.sin(ang_c)[:, None, :])


def rope_half(x, cos, sin):
    x1, x2 = jnp.split(x.astype(jnp.float32), 2, -1)
    return jnp.concatenate([x1 * cos - x2 * sin, x2 * cos + x1 * sin], -1)


def axial_rope(x, tabs):
    cr, sr, cc, sc = tabs
    xr, xc = jnp.split(x, 2, -1)
    return jnp.concatenate([rope_half(xr, cr, sr), rope_half(xc, cc, sc)], -1).astype(x.dtype)


def context_attention(qc, kc, vc, sink=None):
    B, Tc = qc.shape[:2]
    s = jnp.einsum('bqgrd,bkgd->bgrqk', qc, kc).astype(jnp.float32) * qc.shape[-1] ** -0.5
    if sink is not None:
        s = jnp.concatenate([s, jnp.broadcast_to(sink.astype(jnp.float32)[None, :, :, None, None], s.shape[:-1] + (1,))], -1)
    p = jax.nn.softmax(s, -1)[..., :kc.shape[1]].astype(vc.dtype)
    return jnp.einsum('bgrqk,bkgd->bqgrd', p, vc).reshape(B, Tc, -1)


def neighbourhood_attention(q, k, v, kc, vc, rpb):
    B, T, H, d = q.shape
    rows = T // GRID_W
    kh = min(NA_WIN_H, rows)
    ncb = GRID_W // NA_QB_W
    nwin = kh * NA_KB_W
    jb = jnp.arange(ncb)
    cidx = jnp.clip(jb * NA_QB_W - NA_WIN_W // 2, 0, GRID_W - NA_KB_W)[:, None] + jnp.arange(NA_KB_W)[None, :]
    qcol = jb[:, None] * NA_QB_W + jnp.arange(NA_QB_W)[None, :]
    cstart = jnp.clip(qcol - NA_WIN_W // 2, 0, GRID_W - NA_WIN_W)
    col_ok = (cidx[:, None, :] >= cstart[:, :, None]) & (cidx[:, None, :] < cstart[:, :, None] + NA_WIN_W)
    dc = jnp.clip(cidx[:, None, :] - qcol[:, :, None] + NA_WIN_W - 1, 0, 2 * NA_WIN_W - 2)
    kgrid = k.reshape(B, rows, GRID_W, H, d)
    vgrid = v.reshape(B, rows, GRID_W, H, d)
    q_rows = jnp.moveaxis(q.reshape(B, rows, ncb, NA_QB_W, H, d), 1, 0)
    rpb_f = rpb.astype(jnp.float32)
    scale = d ** -0.5

    def row_block(args):
        r, q_r = args
        r0 = jnp.clip(r - kh // 2, 0, rows - kh)
        k_r = lax.dynamic_slice_in_dim(kgrid, r0, kh, axis=1)[:, :, cidx]
        v_r = lax.dynamic_slice_in_dim(vgrid, r0, kh, axis=1)[:, :, cidx]
        dr = r0 + jnp.arange(kh) - r + NA_WIN_H - 1
        bias = rpb_f[:, dr[None, None, :, None], dc[:, :, None, :]]
        s_win = jnp.einsum('bjqhd,bajkhd->bhjqak', q_r, k_r).astype(jnp.float32) * scale + bias
        s_win = jnp.where(col_ok[:, :, None, :], s_win, NEG_INF).reshape(B, H, ncb, NA_QB_W, nwin)
        s_ctx = jnp.einsum('bjqhd,bchd->bhjqc', q_r, kc).astype(jnp.float32) * scale
        p = jax.nn.softmax(jnp.concatenate([s_win, s_ctx], -1), -1).astype(v.dtype)
        p_win = p[..., :nwin].reshape(B, H, ncb, NA_QB_W, kh, NA_KB_W)
        o = jnp.einsum('bhjqak,bajkhd->bjqhd', p_win, v_r) + jnp.einsum('bhjqc,bchd->bjqhd', p[..., nwin:], vc)
        return o.reshape(B, GRID_W, H * d)

    o = lax.map(row_block, (jnp.arange(rows), q_rows))
    return jnp.moveaxis(o, 0, 1).reshape(B, T, H * d)


def centred_short_conv(x, w):
    K = w.shape[0]
    T = x.shape[1]
    xp = jnp.pad(x, ((0, 0), (K // 2, K // 2), (0, 0)))
    return jax.nn.silu(sum(xp[:, j:j + T] * w[j] for j in range(K)))


def chunk_gated_delta_rule(q, k, v, g, beta, S0):
    B, T, H, dk = k.shape
    dv = v.shape[-1]
    C = DN_CHUNK
    N = T // C

    def chunks(t):
        return jnp.moveaxis(t.astype(jnp.float32).reshape((B, N, C) + t.shape[2:]), 3, 2)

    q = chunks(q) * dk ** -0.5
    k = chunks(k)
    v = chunks(v)
    beta = chunks(beta)
    g = jnp.cumsum(chunks(g), -1)
    idx = jnp.arange(C)
    incl = idx[:, None] >= idx[None, :]
    strict = idx[:, None] > idx[None, :]
    diff = g[..., :, None] - g[..., None, :]
    decay = jnp.where(incl, jnp.exp(jnp.where(incl, diff, 0.0)), 0.0)
    kb = k * beta[..., None]
    L = jnp.where(strict, jnp.einsum('bnhcd,bnhsd->bnhcs', kb, k) * decay, 0.0)
    A = L + jnp.eye(C, dtype=jnp.float32)
    rhs = jnp.concatenate([v * beta[..., None], kb * jnp.exp(g)[..., None]], -1)
    sol = lax.linalg.triangular_solve(A, rhs, left_side=True, lower=True, unit_diagonal=True)
    u, w = sol[..., :dv], sol[..., dv:]
    qk = jnp.einsum('bnhcd,bnhsd->bnhcs', q, k) * decay
    q_dec = q * jnp.exp(g)[..., None]
    k_dec = k * jnp.exp(g[..., -1:] - g)[..., None]
    g_last = jnp.exp(g[..., -1])

    def step(S, xs):
        q_i, k_i, u_i, w_i, qk_i, gl = xs
        v_new = u_i - jnp.einsum('bhcd,bhde->bhce', w_i, S)
        o = jnp.einsum('bhcd,bhde->bhce', q_i, S) + jnp.einsum('bhcs,bhse->bhce', qk_i, v_new)
        S = S * gl[..., None, None] + jnp.einsum('bhcd,bhce->bhde', k_i, v_new)
        return S, o

    xs = tuple(jnp.moveaxis(t, 1, 0) for t in (q_dec, k_dec, u, w, qk, g_last))
    S, o = lax.scan(step, S0.astype(jnp.float32), xs)
    o = jnp.moveaxis(jnp.moveaxis(o, 0, 1), 2, 3).reshape(B, T, H, dv)
    return o, S


def gated_rmsnorm(o, z, w):
    of = o.astype(jnp.float32)
    y = of * lax.rsqrt(jnp.mean(of * of, -1, keepdims=True) + NORM_EPS) * w.astype(jnp.float32) * jax.nn.silu(z.astype(jnp.float32))
    return y.astype(z.dtype).reshape(z.shape[0], z.shape[1], -1)


def maybe_flip(t, d):
    return jnp.flip(t, 1) if d else t


def gated_deltanet(p, pc, conv_w, a_log, dt_bias, norm_w, need_ctx):
    def prep(t):
        B, T, _ = t.shape
        qkv = centred_short_conv(t[..., :DN_QKV], conv_w)
        q, k, v = jnp.split(qkv, [DN_HEADS * DN_DK, 2 * DN_HEADS * DN_DK], -1)
        q = l2norm(q.reshape(B, T, DN_HEADS, DN_DK))
        k = l2norm(k.reshape(B, T, DN_HEADS, DN_DK))
        v = v.reshape(B, T, DN_HEADS, DN_DV)
        z = t[..., DN_QKV:DN_QKV + DN_HEADS * DN_DV].reshape(B, T, DN_HEADS, DN_DV)
        ab = t[..., DN_QKV + DN_HEADS * DN_DV:].astype(jnp.float32).reshape(B, T, 2, 2, DN_HEADS)
        g = -jnp.exp(a_log.astype(jnp.float32)) * jax.nn.softplus(ab[:, :, 0] + dt_bias.astype(jnp.float32))
        beta = jax.nn.sigmoid(ab[:, :, 1])
        return q, k, v, z, g, beta

    q, k, v, z, g, beta = prep(p)
    qc, kc, vc, zc, gc, betac = prep(pc)
    B = p.shape[0]
    S0 = jnp.zeros((B, DN_HEADS, DN_DK, DN_DV), jnp.float32)
    o = 0.0
    o_c = 0.0
    for d in range(2):
        oc_d, S_ctx = chunk_gated_delta_rule(maybe_flip(qc, d), maybe_flip(kc, d), maybe_flip(vc, d),
                                             maybe_flip(gc[:, :, d], d), maybe_flip(betac[:, :, d], d), S0)
        o_d, _ = chunk_gated_delta_rule(maybe_flip(q, d), maybe_flip(k, d), maybe_flip(v, d),
                                        maybe_flip(g[:, :, d], d), maybe_flip(beta[:, :, d], d), S_ctx)
        o = o + maybe_flip(o_d, d)
        if need_ctx:
            o_c = o_c + maybe_flip(oc_d, d)
    out = gated_rmsnorm(o, z, norm_w)
    out_c = gated_rmsnorm(o_c, zc, norm_w) if need_ctx else None
    return out, out_c


def even_mixer(u, uc, w_in, w_out, rpb, conv_w, a_log, dt_bias, dn_norm_w, need_ctx):
    B, T, _ = u.shape
    Tc = uc.shape[1]
    p = u @ w_in
    pc = uc @ w_in
    q, k, v = [t.reshape(B, T, NA_HEADS, NA_HD) for t in jnp.split(p[..., :NA_PROJ], 3, -1)]
    qc, kc, vc = [t.reshape(B, Tc, NA_HEADS, NA_HD) for t in jnp.split(pc[..., :NA_PROJ], 3, -1)]
    y_na = neighbourhood_attention(q, k, v, kc, vc, rpb)
    y_dn, y_dn_c = gated_deltanet(p[..., NA_PROJ:], pc[..., NA_PROJ:], conv_w, a_log, dt_bias, dn_norm_w, need_ctx)
    y = jnp.concatenate([y_na, y_dn], -1) @ w_out
    if not need_ctx:
        return y, None
    y_na_c = context_attention(qc[:, :, :, None, :], kc, vc)
    yc = jnp.concatenate([y_na_c, y_dn_c], -1) @ w_out
    return y, yc


def windowed_gqa_sink(q, k, v, kc, vc, sink):
    B, T, Hq, d = q.shape
    G = k.shape[2]
    R = Hq // G
    nb = T // SWA_BLOCK
    pad = ((0, 0), (SWA_BLOCK, SWA_BLOCK), (0, 0), (0, 0))
    kp = jnp.pad(k, pad)
    vp = jnp.pad(v, pad)
    qb = jnp.moveaxis(q.reshape(B, nb, SWA_BLOCK, G, R, d), 1, 0)
    qi = jnp.arange(SWA_BLOCK)
    ki = jnp.arange(3 * SWA_BLOCK) - SWA_BLOCK
    rel = ki[None, :] - qi[:, None]
    sink_f = sink.astype(jnp.float32).reshape(G, R)
    scale = d ** -0.5
    nwin = 3 * SWA_BLOCK
    Tc = kc.shape[1]

    def block(args):
        n, q_n = args
        k_n = lax.dynamic_slice_in_dim(kp, n * SWA_BLOCK, nwin, axis=1)
        v_n = lax.dynamic_slice_in_dim(vp, n * SWA_BLOCK, nwin, axis=1)
        kpos = n * SWA_BLOCK + ki
        ok = (jnp.abs(rel) <= SWA_WINDOW) & ((kpos >= 0) & (kpos < T))[None, :]
        s_win = jnp.where(ok, jnp.einsum('bqgrd,bkgd->bgrqk', q_n, k_n).astype(jnp.float32) * scale, NEG_INF)
        s_ctx = jnp.einsum('bqgrd,bcgd->bgrqc', q_n, kc).astype(jnp.float32) * scale
        s_sink = jnp.broadcast_to(sink_f[None, :, :, None, None], s_ctx.shape[:-1] + (1,))
        p = jax.nn.softmax(jnp.concatenate([s_win, s_ctx, s_sink], -1), -1).astype(v.dtype)
        o = (jnp.einsum('bgrqk,bkgd->bqgrd', p[..., :nwin], v_n)
             + jnp.einsum('bgrqc,bcgd->bqgrd', p[..., nwin:nwin + Tc], vc))
        return o.reshape(B, SWA_BLOCK, Hq * d)

    o = lax.map(block, (jnp.arange(nb), qb))
    return jnp.moveaxis(o, 0, 1).reshape(B, T, Hq * d)


def odd_mixer(u, uc, w_in, w_out, sink, rope, need_ctx):
    B, T, _ = u.shape
    Tc = uc.shape[1]
    QW = SWA_HEADS * SWA_HD
    KW = SWA_KV_HEADS * SWA_HD
    R = SWA_HEADS // SWA_KV_HEADS
    p = u @ w_in
    q = axial_rope(p[..., :QW].reshape(B, T, SWA_HEADS, SWA_HD), rope)
    k = axial_rope(p[..., QW:QW + KW].reshape(B, T, SWA_KV_HEADS, SWA_HD), rope)
    v = p[..., QW + KW:].reshape(B, T, SWA_KV_HEADS, SWA_HD)
    kvc = uc @ w_in[:, QW:]
    kc = kvc[..., :KW].reshape(B, Tc, SWA_KV_HEADS, SWA_HD)
    vc = kvc[..., KW:].reshape(B, Tc, SWA_KV_HEADS, SWA_HD)
    y = windowed_gqa_sink(q, k, v, kc, vc, sink) @ w_out
    if not need_ctx:
        return y, None
    qc = (uc @ w_in[:, :QW]).reshape(B, Tc, SWA_KV_HEADS, R, SWA_HD)
    yc = context_attention(qc, kc, vc, sink.reshape(SWA_KV_HEADS, R)) @ w_out
    return y, yc


def setup_inputs(seed: int = 0) -> dict:
    key = jax.random.key(seed)
    ks = jax.random.split(key, 20)
    f32 = jnp.float32

    def nrm(k, shape, fan_in, gain=1.0):
        return jax.random.normal(k, shape, f32) * (gain * fan_in ** -0.5)

    dt = jnp.exp(jax.random.uniform(ks[14], (N_EVEN, 2, DN_HEADS), f32, math.log(1e-3), math.log(1e-1)))
    return {
        "x": jax.random.normal(ks[0], (BATCH, SEQ, D_MODEL), f32),
        "c": jax.random.normal(ks[1], (BATCH, D_MODEL), f32),
        "ctx": jax.random.normal(ks[2], (BATCH, CTX_LEN, D_MODEL), f32),
        "c_ctx": jax.random.normal(ks[3], (D_MODEL,), f32),
        "ada_w": nrm(ks[4], (DEPTH, D_MODEL, N_MOD * D_MODEL), D_MODEL, 0.5),
        "ada_b": 0.02 * jax.random.normal(ks[5], (DEPTH, N_MOD * D_MODEL), f32),
        "norm_g": 1.0 + 0.02 * jax.random.normal(ks[6], (DEPTH, 3, D_MODEL), f32),
        "ffn_w_up": nrm(ks[7], (DEPTH, 2, D_MODEL, 2 * D_FF), D_MODEL),
        "ffn_w_down": nrm(ks[8], (DEPTH, 2, D_FF, D_MODEL), D_FF),
        "even_w_in": nrm(ks[9], (N_EVEN, D_MODEL, EVEN_PROJ), D_MODEL),
        "even_w_out": nrm(ks[10], (N_EVEN, EVEN_MIX_W, D_MODEL), EVEN_MIX_W),
        "na_rpb": 0.1 * jax.random.normal(ks[11], (N_EVEN, NA_HEADS, 2 * NA_WIN_H - 1, 2 * NA_WIN_W - 1), f32),
        "dn_conv_w": nrm(ks[12], (N_EVEN, DN_CONV, DN_QKV), DN_CONV),
        "dn_a_log": jnp.log(jax.random.uniform(ks[13], (N_EVEN, 2, DN_HEADS), f32, 1.0, 16.0)),
        "dn_dt_bias": dt + jnp.log(-jnp.expm1(-dt)),
        "dn_norm_w": 1.0 + 0.02 * jax.random.normal(ks[15], (N_EVEN, DN_DV), f32),
        "odd_w_in": nrm(ks[16], (N_ODD, D_MODEL, ODD_PROJ), D_MODEL),
        "odd_w_out": nrm(ks[17], (N_ODD, ODD_MIX_W, D_MODEL), ODD_MIX_W),
        "swa_sink": jax.random.normal(ks[18], (N_ODD, SWA_HEADS), f32),
        "final_norm_g": 1.0 + 0.02 * jax.random.normal(ks[19], (D_MODEL,), f32),
    }


def reference(x, c, ctx, c_ctx, ada_w, ada_b, norm_g, ffn_w_up, ffn_w_down, even_w_in, even_w_out, na_rpb,
              dn_conv_w, dn_a_log, dn_dt_bias, dn_norm_w, odd_w_in, odd_w_out, swa_sink, final_norm_g):
    T = x.shape[1]
    rope = axial_rope_tables(T)
    h, hc = x, ctx
    for i in range(DEPTH):
        need_ctx = i < DEPTH - 1
        j = i // 2
        ml = adaln_terms(c[:, None, :], ada_w[i], ada_b[i])
        mc = adaln_terms(c_ctx, ada_w[i], ada_b[i])
        h = h + 0.5 * ml[2] * swiglu(modulate(h, norm_g[i, 0], ml[0], ml[1]), ffn_w_up[i, 0], ffn_w_down[i, 0])
        hc = hc + 0.5 * mc[2] * swiglu(modulate(hc, norm_g[i, 0], mc[0], mc[1]), ffn_w_up[i, 0], ffn_w_down[i, 0])
        u = modulate(h, norm_g[i, 1], ml[3], ml[4])
        uc = modulate(hc, norm_g[i, 1], mc[3], mc[4])
        if i % 2 == 0:
            y, yc = even_mixer(u, uc, even_w_in[j], even_w_out[j], na_rpb[j], dn_conv_w[j], dn_a_log[j],
                               dn_dt_bias[j], dn_norm_w[j], need_ctx)
        else:
            y, yc = odd_mixer(u, uc, odd_w_in[j], odd_w_out[j], swa_sink[j], rope, need_ctx)
        h = h + ml[5] * y
        h = h + 0.5 * ml[8] * swiglu(modulate(h, norm_g[i, 2], ml[6], ml[7]), ffn_w_up[i, 1], ffn_w_down[i, 1])
        if need_ctx:
            hc = hc + mc[5] * yc
            hc = hc + 0.5 * mc[8] * swiglu(modulate(hc, norm_g[i, 2], mc[6], mc[7]), ffn_w_up[i, 1], ffn_w_down[i, 1])
    return rmsnorm(h, final_norm_g)
```

```python
import functools
import math

import jax
import jax.numpy as jnp
from jax import lax
from jax.experimental import pallas as pl
from jax.experimental.pallas import tpu as pltpu

F32 = jnp.float32
BF16 = jnp.bfloat16

GRID_W = 64
N_MOD = 9
NORM_EPS = 1e-6
NEG_INF = -1e30
ROPE_BASE = 10000.0
NA_HEADS = 8
NA_HD = 64
NA_WIN_H = 8
NA_WIN_W = 16
NA_PROJ = 3 * NA_HEADS * NA_HD
DN_HEADS = 4
DN_DK = 128
DN_DV = 128
DN_CONV = 5
DN_CHUNK = 64
DN_QKV = DN_HEADS * (2 * DN_DK + DN_DV)
SWA_HEADS = 16
SWA_KV_HEADS = 4
SWA_HD = 64
SWA_BLOCK = 128

LANES = 128
SUBLANES = 8
VMEM_LIMIT = 56 * 1024 * 1024

TOK_TILE = 512
FF_CHUNK = 256
MOD_ROWS = 16


def _params(*sem):
    return pltpu.CompilerParams(dimension_semantics=sem, vmem_limit_bytes=VMEM_LIMIT)


def _dot(a, b):
    return jnp.dot(a, b, preferred_element_type=F32)


def _dot_nt(a, b):
    return lax.dot_general(a, b, (((1,), (1,)), ((), ())), preferred_element_type=F32)


def _dot_tn(a, b):
    return lax.dot_general(a, b, (((0,), (0,)), ((), ())), preferred_element_type=F32)


def _sigmoid(x):
    return 1.0 / (1.0 + jnp.exp(-x))


def _silu(x):
    return x * _sigmoid(x)


def _softplus(x):
    return jnp.maximum(x, 0.0) + jnp.log1p(jnp.exp(-jnp.abs(x)))


def _modulate(x, g, shift, scale):
    y = x * lax.rsqrt(jnp.mean(x * x, axis=-1, keepdims=True) + NORM_EPS) * g
    return y * (1.0 + scale) + shift


def _split3(x):
    h1 = x.astype(BF16)
    r1 = x - h1.astype(F32)
    h2 = r1.astype(BF16)
    h3 = (r1 - h2.astype(F32)).astype(BF16)
    return h1, h2, h3


def _split2(x):
    hi = x.astype(BF16)
    return hi, (x - hi.astype(F32)).astype(BF16)


def _mm3(a, b):
    ah, al = _split2(a)
    bh, bl = _split2(b)
    return _dot(ah, bh) + (_dot(al, bh) + _dot(ah, bl))


def _adaln_kernel(c_ref, w_ref, b_ref, o_ref):
    s = _silu(c_ref[...]).astype(BF16)
    o_ref[...] = _dot(s, w_ref[...].astype(BF16)) + b_ref[...]


def _adaln_call(c16, ada_w, ada_b):
    depth, d, _ = ada_w.shape
    b4 = ada_b.reshape(depth * N_MOD, 1, d)
    return pl.pallas_call(
        _adaln_kernel,
        out_shape=jax.ShapeDtypeStruct((depth, N_MOD, MOD_ROWS, d), F32),
        grid=(depth, N_MOD),
        in_specs=[
            pl.BlockSpec((MOD_ROWS, d), lambda i, k: (0, 0)),
            pl.BlockSpec((None, d, d), lambda i, k: (i, 0, k)),
            pl.BlockSpec((None, 1, d), lambda i, k: (i * N_MOD + k, 0, 0)),
        ],
        out_specs=pl.BlockSpec((None, None, MOD_ROWS, d), lambda i, k: (i, k, 0, 0)),
        compiler_params=_params("arbitrary", "arbitrary"),
        name="adaln",
    )(c16, ada_w, b4)


def _mod_spec(layer, k, ctx_row, d):
    if ctx_row is None:
        return pl.BlockSpec((None, None, None, 1, d), lambda b, t: (layer, k, b, 0, 0))
    return pl.BlockSpec((None, None, None, 1, d), lambda b, t: (layer, k, ctx_row, 0, 0))


def _const_spec(shape):
    nd = len(shape)
    return pl.BlockSpec(shape, lambda b, t: (0,) * nd, pipeline_mode=pl.Buffered(1))


def _ffn_kernel(x_ref, g_ref, sh_ref, sc_ref, gt_ref, wg_ref, wu_ref, wd_ref, *rest, n_chunks, final):
    if final:
        fg_ref, o_ref, u_ref, acc_ref = rest
    else:
        o_ref, u_ref, acc_ref = rest
    u_ref[...] = _modulate(x_ref[...], g_ref[...], sh_ref[...], sc_ref[...]).astype(BF16)
    acc_ref[...] = jnp.zeros_like(acc_ref)

    def body(c, carry):
        u = u_ref[...]
        gate = _dot(u, wg_ref[c])
        up = _dot(u, wu_ref[c])
        a = (_silu(gate) * up).astype(BF16)
        acc_ref[...] += _dot(a, wd_ref[c])
        return carry

    lax.fori_loop(0, n_chunks, body, 0)
    y = x_ref[...] + 0.5 * gt_ref[...] * acc_ref[...]
    if final:
        y = y * lax.rsqrt(jnp.mean(y * y, axis=-1, keepdims=True) + NORM_EPS) * fg_ref[...]
    o_ref[...] = y


def _ffn_call(h, mods, norm_g3, layer, which, ctx_row, wg, wu, wd, final_g=None):
    bsz, t, d = h.shape
    tm = min(TOK_TILE, t)
    n_chunks = wg.shape[0]
    k0 = 6 if which else 0
    in_specs = [
        pl.BlockSpec((None, tm, d), lambda b, i: (b, i, 0)),
        pl.BlockSpec((None, 1, d), lambda b, i: (layer * 3 + (2 if which else 0), 0, 0)),
        _mod_spec(layer, k0, ctx_row, d),
        _mod_spec(layer, k0 + 1, ctx_row, d),
        _mod_spec(layer, k0 + 2, ctx_row, d),
        _const_spec(wg.shape),
        _const_spec(wu.shape),
        _const_spec(wd.shape),
    ]
    args = [h, norm_g3, mods, mods, mods, wg, wu, wd]
    if final_g is not None:
        in_specs.append(pl.BlockSpec((1, d), lambda b, i: (0, 0)))
        args.append(final_g)
    return pl.pallas_call(
        functools.partial(_ffn_kernel, n_chunks=n_chunks, final=final_g is not None),
        out_shape=jax.ShapeDtypeStruct(h.shape, F32),
        grid=(bsz, t // tm),
        in_specs=in_specs,
        out_specs=pl.BlockSpec((None, tm, d), lambda b, i: (b, i, 0)),
        scratch_shapes=[pltpu.VMEM((tm, d), BF16), pltpu.VMEM((tm, d), F32)],
        compiler_params=_params("parallel", "parallel"),
        name="ffn",
    )(*args)


def _proj_even_kernel(x_ref, g_ref, sh_ref, sc_ref, wna_ref, wdn_ref, wz_ref, wab_ref,
                      na_ref, dn_ref, z_ref, ab_ref):
    u = _modulate(x_ref[...], g_ref[...], sh_ref[...], sc_ref[...]).astype(BF16)
    na_ref[...] = _dot(u, wna_ref[...]).astype(BF16)
    dn_ref[...] = _dot(u, wdn_ref[...])
    z_ref[...] = _dot(u, wz_ref[...])
    ab_ref[...] = _dot(u, wab_ref[...])


def _proj_even_call(h, mods, norm_g3, layer, ctx_row, wna, wdn, wz, wab):
    bsz, t, d = h.shape
    tm = min(TOK_TILE, t)
    widths = (wna.shape[1], wdn.shape[1], wz.shape[1], wab.shape[1])
    dtypes = (BF16, F32, F32, F32)
    return pl.pallas_call(
        _proj_even_kernel,
        out_shape=[jax.ShapeDtypeStruct((bsz, t, w), dt) for w, dt in zip(widths, dtypes)],
        grid=(bsz, t // tm),
        in_specs=[
            pl.BlockSpec((None, tm, d), lambda b, i: (b, i, 0)),
            pl.BlockSpec((None, 1, d), lambda b, i: (layer * 3 + 1, 0, 0)),
            _mod_spec(layer, 3, ctx_row, d),
            _mod_spec(layer, 4, ctx_row, d),
            _const_spec(wna.shape), _const_spec(wdn.shape), _const_spec(wz.shape), _const_spec(wab.shape),
        ],
        out_specs=[pl.BlockSpec((None, tm, w), lambda b, i: (b, i, 0)) for w in widths],
        compiler_params=_params("parallel", "parallel"),
        name="proj_even",
    )(h, norm_g3, mods, mods, wna, wdn, wz, wab)


def _rope_slab(x, cos, sin, first):
    swapped = jnp.where(first, pltpu.roll(x, LANES - 16, 1), pltpu.roll(x, 16, 1))
    return x * cos + swapped * sin


def _proj_odd_kernel(x_ref, g_ref, sh_ref, sc_ref, *rest, with_q):
    if with_q:
        wq_ref, wk_ref, wv_ref, cos_ref, sin_ref, q_ref, k_ref, v_ref = rest
    else:
        wk_ref, wv_ref, k_ref, v_ref = rest
    u = _modulate(x_ref[...], g_ref[...], sh_ref[...], sc_ref[...]).astype(BF16)
    v_ref[...] = _dot(u, wv_ref[...]).astype(BF16)
    k = _dot(u, wk_ref[...])
    if not with_q:
        k_ref[...] = k.astype(BF16)
        return
    cos = cos_ref[...]
    sin = sin_ref[...]
    lane = lax.broadcasted_iota(jnp.int32, cos.shape, 1)
    first = (lane % 32) < 16
    for j in range(k.shape[1] // LANES):
        sl = slice(j * LANES, (j + 1) * LANES)
        k_ref[:, sl] = _rope_slab(k[:, sl], cos, sin, first).astype(BF16)
    q = _dot(u, wq_ref[...])
    for j in range(q.shape[1] // LANES):
        sl = slice(j * LANES, (j + 1) * LANES)
        q_ref[:, sl] = _rope_slab(q[:, sl], cos, sin, first).astype(BF16)


def _proj_odd_call(h, mods, norm_g3, layer, ctx_row, wq, wk, wv, cos, sin):
    bsz, t, d = h.shape
    tm = min(TOK_TILE, t)
    with_q = wq is not None
    in_specs = [
        pl.BlockSpec((None, tm, d), lambda b, i: (b, i, 0)),
        pl.BlockSpec((None, 1, d), lambda b, i: (layer * 3 + 1, 0, 0)),
        _mod_spec(layer, 3, ctx_row, d),
        _mod_spec(layer, 4, ctx_row, d),
    ]
    args = [h, norm_g3, mods, mods]
    widths = []
    if with_q:
        in_specs.append(_const_spec(wq.shape))
        args.append(wq)
        widths.append(wq.shape[1])
    in_specs += [_const_spec(wk.shape), _const_spec(wv.shape)]
    args += [wk, wv]
    widths += [wk.shape[1], wv.shape[1]]
    if with_q:
        in_specs += [pl.BlockSpec((tm, LANES), lambda b, i: (i, 0))] * 2
        args += [cos, sin]
    return pl.pallas_call(
        functools.partial(_proj_odd_kernel, with_q=with_q),
        out_shape=[jax.ShapeDtypeStruct((bsz, t, w), BF16) for w in widths],
        grid=(bsz, t // tm),
        in_specs=in_specs,
        out_specs=[pl.BlockSpec((None, tm, w), lambda b, i: (b, i, 0)) for w in widths],
        compiler_params=_params("parallel", "parallel"),
        name="proj_odd",
    )(*args)


def _outproj_kernel(h_ref, gt_ref, *rest, n_in):
    y_refs, w_refs, o_ref = rest[:n_in], rest[n_in:2 * n_in], rest[2 * n_in]
    acc = _dot(y_refs[0][...], w_refs[0][...])
    for y_ref, w_ref in zip(y_refs[1:], w_refs[1:]):
        acc += _dot(y_ref[...], w_ref[...])
    o_ref[...] = h_ref[...] + gt_ref[...] * acc


def _outproj_call(h, mods, layer, ctx_row, ys, ws):
    bsz, t, d = h.shape
    tm = min(TOK_TILE, t)
    in_specs = [pl.BlockSpec((None, tm, d), lambda b, i: (b, i, 0)), _mod_spec(layer, 5, ctx_row, d)]
    in_specs += [pl.BlockSpec((None, tm, y.shape[2]), lambda b, i: (b, i, 0)) for y in ys]
    in_specs += [_const_spec(w.shape) for w in ws]
    return pl.pallas_call(
        functools.partial(_outproj_kernel, n_in=len(ys)),
        out_shape=jax.ShapeDtypeStruct(h.shape, F32),
        grid=(bsz, t // tm),
        in_specs=in_specs,
        out_specs=pl.BlockSpec((None, tm, d), lambda b, i: (b, i, 0)),
        compiler_params=_params("parallel", "parallel"),
        name="outproj",
    )(h, mods, *ys, *ws)


def _softmax_pv(parts, extra_logit=None):
    m = functools.reduce(jnp.maximum, [jnp.max(s, axis=-1, keepdims=True) for s, _ in parts])
    if extra_logit is not None:
        m = jnp.maximum(m, extra_logit)
    den = None
    acc = None
    for s, v in parts:
        p = jnp.exp(s - m)
        ps = jnp.sum(p, axis=-1, keepdims=True)
        den = ps if den is None else den + ps
        pv = _dot(p.astype(BF16), v)
        acc = pv if acc is None else acc + pv
    if extra_logit is not None:
        den = den + jnp.exp(extra_logit - m)
    return acc / den


def _na_kernel(q_ref, k_ref, v_ref, qc_ref, kc_ref, vc_ref, bias_ref, o_ref, oc_ref, *, rows):
    scale = NA_HD ** -0.5
    n_pairs = q_ref.shape[1] // LANES
    tc = qc_ref.shape[0]
    win = NA_WIN_H * GRID_W
    lane_q = lax.broadcasted_iota(jnp.int32, (GRID_W, LANES), 1)
    lane_c = lax.broadcasted_iota(jnp.int32, (tc, LANES), 1)

    def row_body(r, carry):
        r0 = jnp.clip(r - NA_WIN_H // 2, 0, rows - NA_WIN_H)
        doff = r0 - r + NA_WIN_H - 1
        qs = pl.ds(pl.multiple_of(r * GRID_W, GRID_W), GRID_W)
        ks = pl.ds(pl.multiple_of(r0 * GRID_W, GRID_W), win)
        for hp in range(n_pairs):
            sl = slice(hp * LANES, (hp + 1) * LANES)
            q2 = q_ref[qs, sl]
            zero = jnp.zeros_like(q2)
            qm = jnp.concatenate([jnp.where(lane_q < NA_HD, q2, zero), jnp.where(lane_q >= NA_HD, q2, zero)], axis=0)
            bias = jnp.concatenate([bias_ref[hp, doff + 2 * j] for j in range(NA_WIN_H // 2)], axis=1)
            s_win = _dot_nt(qm, k_ref[ks, sl]) * scale + bias
            s_ctx = _dot_nt(qm, kc_ref[:, sl]) * scale
            o = _softmax_pv([(s_win, v_ref[ks, sl]), (s_ctx, vc_ref[:, sl])])
            o_ref[qs, sl] = jnp.where(lane_q < NA_HD, o[:GRID_W], o[GRID_W:]).astype(BF16)
        return carry

    lax.fori_loop(0, rows, row_body, 0)

    for hp in range(n_pairs):
        sl = slice(hp * LANES, (hp + 1) * LANES)
        q2 = qc_ref[:, sl]
        zero = jnp.zeros_like(q2)
        halves = []
        for half in range(2):
            keep = (lane_c >= NA_HD) if half else (lane_c < NA_HD)
            s = _dot_nt(jnp.where(keep, q2, zero), kc_ref[:, sl]) * scale
            halves.append(_softmax_pv([(s, vc_ref[:, sl])]))
        oc_ref[:, sl] = jnp.where(lane_c < NA_HD, halves[0], halves[1]).astype(BF16)


def _na_call(qkv, qkvc, bias):
    bsz, t, w3 = qkv.shape
    tc = qkvc.shape[1]
    w = w3 // 3

    def col(j):
        return lambda b: (b, 0, j)

    return pl.pallas_call(
        functools.partial(_na_kernel, rows=t // GRID_W),
        out_shape=[jax.ShapeDtypeStruct((bsz, t, w), BF16), jax.ShapeDtypeStruct((bsz, tc, w), BF16)],
        grid=(bsz,),
        in_specs=[pl.BlockSpec((None, t, w), col(j)) for j in range(3)]
        + [pl.BlockSpec((None, tc, w), col(j)) for j in range(3)]
        + [pl.BlockSpec(bias.shape, lambda b: (0, 0, 0, 0), pipeline_mode=pl.Buffered(1))],
        out_specs=[pl.BlockSpec((None, t, w), lambda b: (b, 0, 0)), pl.BlockSpec((None, tc, w), lambda b: (b, 0, 0))],
        compiler_params=_params("parallel"),
        name="na_attn",
    )(qkv, qkv, qkv, qkvc, qkvc, qkvc, bias)


def _na_bias_table(rpb):
    c = jnp.arange(GRID_W)[:, None]
    kc = jnp.arange(GRID_W)[None, :]
    cstart = jnp.clip(c - NA_WIN_W // 2, 0, GRID_W - NA_WIN_W)
    ok = (kc >= cstart) & (kc < cstart + NA_WIN_W)
    dc = jnp.clip(kc - c + NA_WIN_W - 1, 0, 2 * NA_WIN_W - 2)
    t = jnp.where(ok[None, None], rpb.astype(F32)[:, :, dc], NEG_INF)
    nd = 2 * NA_WIN_H - 2
    pair = jnp.concatenate([t[:, :nd], t[:, 1:nd + 1]], axis=-1)
    h = rpb.shape[0]
    pair = pair.reshape(h // 2, 2, nd, GRID_W, 2 * GRID_W)
    return jnp.moveaxis(pair, 1, 2).reshape(h // 2, nd, 2 * GRID_W, 2 * GRID_W)


def _dn_kernel(qr_ref, kr_ref, vr_ref, z_ref, ab_ref, qcr_ref, kcr_ref, vcr_ref, zc_ref, abc_ref,
               cwq_ref, cwk_ref, cwv_ref, alog_ref, dtb_ref, nw_ref,
               y_ref, yc_ref,
               pad_ref, abs_ref, u_ref, wq_ref, qk_ref, kd_ref, gl_ref, o_ref):
    c = DN_CHUNK
    t = qr_ref.shape[0]
    tc = qcr_ref.shape[0]
    ncc, ncl = tc // c, t // c
    lat0 = 2 * SUBLANES + tc
    hid = pl.program_id(1)

    pad_ref[...] = jnp.zeros_like(pad_ref)
    for s, (cr, lr) in enumerate(((qcr_ref, qr_ref), (kcr_ref, kr_ref), (vcr_ref, vr_ref))):
        pad_ref[s, SUBLANES:SUBLANES + tc, :] = cr[...]
        pad_ref[s, lat0:lat0 + t, :] = lr[...]
    abs_ref[0:tc, :] = abc_ref[...]
    abs_ref[tc:tc + t, :] = ab_ref[...]

    lane = lax.broadcasted_iota(jnp.int32, (c, LANES), 1)
    lane8 = lax.broadcasted_iota(jnp.int32, (SUBLANES, LANES), 1)
    ri = lax.broadcasted_iota(jnp.int32, (c, c), 0)
    ci_ = lax.broadcasted_iota(jnp.int32, (c, c), 1)
    eye = jnp.where(ri == ci_, 1.0, 0.0).astype(F32)
    neg_a = -jnp.exp(alog_ref[...])
    dtb = dtb_ref[...]
    half = DN_CONV // 2

    def conv(s, cw_ref, base):
        win = pad_ref[s, pl.ds(pl.multiple_of(base - SUBLANES, SUBLANES), c + 2 * SUBLANES), :]
        acc = win[SUBLANES - half:SUBLANES - half + c] * cw_ref[0:1, :]
        for j in range(1, DN_CONV):
            acc += win[SUBLANES - half + j:SUBLANES - half + j + c] * cw_ref[j:j + 1, :]
        return _silu(acc)

    def l2n(x):
        return x * lax.rsqrt(jnp.sum(x * x, axis=-1, keepdims=True) + NORM_EPS)

    def pick(x, col):
        return jnp.sum(jnp.where(lane == col, x, 0.0), axis=-1, keepdims=True)

    def prep(ch, carry):
        base = ch * c + jnp.where(ch < ncc, SUBLANES, 2 * SUBLANES)
        q = l2n(conv(0, cwq_ref, base))
        k = l2n(conv(1, cwk_ref, base))
        v = conv(2, cwv_ref, base)
        abt = abs_ref[pl.ds(pl.multiple_of(ch * c, c), c), :]
        g_all = neg_a * _softplus(abt + dtb)
        b_all = _sigmoid(abt)
        g_parts = _split3(g_all)
        kb16 = k.astype(BF16)
        qs = q * DN_DK ** -0.5
        qs16 = qs.astype(BF16)
        for d in range(2):
            incl = (ri >= ci_) if d == 0 else (ri <= ci_)
            strict = (ri > ci_) if d == 0 else (ri < ci_)
            last = c - 1 if d == 0 else 0
            col_g = d * DN_HEADS + hid
            tri = jnp.where(incl, 1.0, 0.0).astype(BF16)
            gc = _dot(tri, g_parts[0]) + (_dot(tri, g_parts[1]) + _dot(tri, g_parts[2]))
            gcol = pick(gc, col_g)
            sel = jnp.where(lane8 == col_g, 1.0, 0.0).astype(BF16)
            gc_parts = _split3(gc)
            grow = (_dot_nt(sel, gc_parts[0]) + (_dot_nt(sel, gc_parts[1]) + _dot_nt(sel, gc_parts[2])))[0:1, :]
            beta = pick(b_all, 2 * DN_HEADS + col_g)
            decay = jnp.where(incl, jnp.exp(jnp.where(incl, gcol - grow, 0.0)), 0.0)
            kb = k * beta
            lmat = jnp.where(strict, _dot_nt(kb.astype(BF16), kb16) * decay, 0.0)
            p = eye - lmat
            m = _mm3(lmat, lmat)
            n_sq = int(math.log2(c)) - 1
            for s in range(n_sq):
                p = p + _mm3(p, m)
                if s < n_sq - 1:
                    m = _mm3(m, m)
            egc = jnp.exp(gcol)
            rhs = jnp.concatenate([v * beta, kb * egc], axis=1).astype(BF16)
            sol = _dot(p.astype(BF16), rhs)
            g_last = gcol[last:last + 1, :]
            u_ref[d, ch] = sol[:, :DN_DV]
            wq_ref[d, ch, 0:c, :] = sol[:, DN_DV:].astype(BF16)
            wq_ref[d, ch, c:2 * c, :] = (qs * egc).astype(BF16)
            qk_ref[d, ch] = (_dot_nt(qs16, kb16) * decay).astype(BF16)
            kd_ref[d, ch] = (k * jnp.exp(g_last - gcol)).astype(BF16)
            gl_ref[d, ch] = jnp.broadcast_to(jnp.exp(g_last), (SUBLANES, LANES))
        return carry

    lax.fori_loop(0, ncc + ncl, prep, 0)

    def step(d, ch, s_mat):
        res = _dot(wq_ref[d, ch], s_mat.astype(BF16))
        v_new = (u_ref[d, ch] - res[:c]).astype(BF16)
        o_ref[d, pl.ds(pl.multiple_of(ch * c, c), c), :] = res[c:] + _dot(qk_ref[d, ch], v_new)
        return s_mat * gl_ref[d, ch][0:1, :] + _dot_tn(kd_ref[d, ch], v_new)

    def scan(first, n):
        def body(i, carry):
            return step(0, first + i, carry[0]), step(1, first + n - 1 - i, carry[1])
        return body

    s0 = jnp.zeros((DN_DK, DN_DV), F32)
    carry = lax.fori_loop(0, ncc, scan(0, ncc), (s0, s0))
    lax.fori_loop(0, ncl, scan(ncc, ncl), carry)

    def gated_norm(o, z):
        return (o * lax.rsqrt(jnp.mean(o * o, axis=-1, keepdims=True) + NORM_EPS) * nw_ref[...] * _silu(z))

    yc_ref[...] = gated_norm(o_ref[0, 0:tc, :] + o_ref[1, 0:tc, :], zc_ref[...]).astype(BF16)
    piece = 4 * c
    for j in range(t // piece):
        rows = slice(j * piece, (j + 1) * piece)
        orow = slice(tc + j * piece, tc + (j + 1) * piece)
        y_ref[rows, :] = gated_norm(o_ref[0, orow, :] + o_ref[1, orow, :], z_ref[rows, :]).astype(BF16)


def _dn_call(dn, z, ab, dnc, zc, abc, conv_w, alog, dtb, norm_w):
    bsz, t, _ = dn.shape
    tc = dnc.shape[1]
    h = DN_HEADS
    nch = (t + tc) // DN_CHUNK
    c = DN_CHUNK

    def slab(rows, j0):
        return pl.BlockSpec((None, rows, LANES), lambda b, hh: (b, 0, j0 + hh))

    def whole(rows):
        return pl.BlockSpec((None, rows, LANES), lambda b, hh: (b, 0, 0))

    def cw(j0):
        return pl.BlockSpec((DN_CONV, LANES), lambda b, hh: (0, j0 + hh))

    vec = pl.BlockSpec((1, LANES), lambda b, hh: (0, 0))
    return pl.pallas_call(
        _dn_kernel,
        out_shape=[jax.ShapeDtypeStruct((bsz, t, h * DN_DV), BF16), jax.ShapeDtypeStruct((bsz, tc, h * DN_DV), BF16)],
        grid=(bsz, h),
        in_specs=[slab(t, 0), slab(t, h), slab(t, 2 * h), slab(t, 0), whole(t),
                  slab(tc, 0), slab(tc, h), slab(tc, 2 * h), slab(tc, 0), whole(tc),
                  cw(0), cw(h), cw(2 * h), vec, vec, vec],
        out_specs=[slab(t, 0), slab(tc, 0)],
        scratch_shapes=[
            pltpu.VMEM((3, 3 * SUBLANES + tc + t, LANES), F32),
            pltpu.VMEM((tc + t, LANES), F32),
            pltpu.VMEM((2, nch, c, DN_DV), F32),
            pltpu.VMEM((2, nch, 2 * c, DN_DK), BF16),
            pltpu.VMEM((2, nch, c, c), BF16),
            pltpu.VMEM((2, nch, c, DN_DK), BF16),
            pltpu.VMEM((2, nch, SUBLANES, LANES), F32),
            pltpu.VMEM((2, tc + t, DN_DV), F32),
        ],
        compiler_params=_params("parallel", "parallel"),
        name="deltanet",
    )(dn, dn, dn, z, ab, dnc, dnc, dnc, zc, abc, conv_w, conv_w, conv_w, alog, dtb, norm_w)


def _swa_kernel(sink_ref, q_ref, k_ref, v_ref, kc_ref, vc_ref, o_ref, *, n_blocks):
    scale = SWA_HD ** -0.5
    blk = SWA_BLOCK
    n_kv_pairs = k_ref.shape[1] // LANES
    slabs = q_ref.shape[1] // LANES // n_kv_pairs
    rows = slabs * blk
    lane = lax.broadcasted_iota(jnp.int32, (rows, LANES), 1)
    qi = lax.broadcasted_iota(jnp.int32, (rows, blk), 0) % blk
    kj = lax.broadcasted_iota(jnp.int32, (rows, blk), 1)

    def block_body(n, carry):
        lo = jnp.maximum(n - 1, 0)
        hi = jnp.minimum(n + 1, n_blocks - 1)
        ok_lo = (kj >= qi) & (n > 0)
        ok_hi = (kj <= qi) & (n < n_blocks - 1)

        def tok(i):
            return pl.ds(pl.multiple_of(i * blk, blk), blk)

        for p in range(n_kv_pairs):
            kvl = slice(p * LANES, (p + 1) * LANES)
            k3 = jnp.concatenate([k_ref[tok(lo), kvl], k_ref[tok(n), kvl], k_ref[tok(hi), kvl]], axis=0)
            v3 = jnp.concatenate([v_ref[tok(lo), kvl], v_ref[tok(n), kvl], v_ref[tok(hi), kvl]], axis=0)
            qs = jnp.concatenate(
                [q_ref[tok(n), (p * slabs + j) * LANES:(p * slabs + j + 1) * LANES] for j in range(slabs)], axis=0)
            zero = jnp.zeros_like(qs)
            halves = []
            for half in range(2):
                keep = (lane >= SWA_HD) if half else (lane < SWA_HD)
                qm = jnp.where(keep, qs, zero)
                s3 = _dot_nt(qm, k3) * scale
                s_win = jnp.concatenate([
                    jnp.where(ok_lo, s3[:, :blk], NEG_INF),
                    s3[:, blk:2 * blk],
                    jnp.where(ok_hi, s3[:, 2 * blk:], NEG_INF)], axis=1)
                s_ctx = _dot_nt(qm, kc_ref[:, kvl]) * scale
                sink = jnp.concatenate(
                    [jnp.full((blk, 1), sink_ref[(2 * p + half) * slabs + j], F32) for j in range(slabs)], axis=0)
                halves.append(_softmax_pv([(s_win, v3), (s_ctx, vc_ref[:, kvl])], extra_logit=sink))
            o = jnp.where(lane < SWA_HD, halves[0], halves[1]).astype(BF16)
            for j in range(slabs):
                o_ref[tok(n), (p * slabs + j) * LANES:(p * slabs + j + 1) * LANES] = o[j * blk:(j + 1) * blk]
        return carry

    lax.fori_loop(0, n_blocks, block_body, 0)


def _swa_call(sink, q, k, v, kc, vc):
    bsz, t, wq = q.shape
    wk = k.shape[2]
    tc = kc.shape[1]
    return pl.pallas_call(
        functools.partial(_swa_kernel, n_blocks=t // SWA_BLOCK),
        out_shape=jax.ShapeDtypeStruct((bsz, t, wq), BF16),
        grid=(bsz,),
        in_specs=[
            pl.BlockSpec(memory_space=pltpu.SMEM),
            pl.BlockSpec((None, t, wq), lambda b: (b, 0, 0)),
            pl.BlockSpec((None, t, wk), lambda b: (b, 0, 0)),
            pl.BlockSpec((None, t, wk), lambda b: (b, 0, 0)),
            pl.BlockSpec((None, tc, wk), lambda b: (b, 0, 0)),
            pl.BlockSpec((None, tc, wk), lambda b: (b, 0, 0)),
        ],
        out_specs=pl.BlockSpec((None, t, wq), lambda b: (b, 0, 0)),
        compiler_params=_params("parallel"),
        name="swa_attn",
    )(sink, q, k, v, kc, vc)


def _swa_head_order():
    rep = SWA_HEADS // SWA_KV_HEADS
    order = []
    for p in range(SWA_KV_HEADS // 2):
        for j in range(rep):
            order += [2 * p * rep + j, (2 * p + 1) * rep + j]
    return order


def _rope_tables(t):
    pos = jnp.arange(t)
    half = SWA_HD // 2
    inv = jnp.power(ROPE_BASE, -jnp.arange(0, half, 2, dtype=F32) / half)
    ang_r = (pos // GRID_W).astype(F32)[:, None] * inv
    ang_c = (pos % GRID_W).astype(F32)[:, None] * inv
    cos = jnp.concatenate([jnp.cos(ang_r)] * 2 + [jnp.cos(ang_c)] * 2, axis=-1)
    sin = jnp.concatenate([-jnp.sin(ang_r), jnp.sin(ang_r), -jnp.sin(ang_c), jnp.sin(ang_c)], axis=-1)
    return jnp.tile(cos, (1, LANES // SWA_HD)), jnp.tile(sin, (1, LANES // SWA_HD))


def _ffn_weights(w_up, w_down):
    d, f2 = w_up.shape
    f = f2 // 2
    nc = f // FF_CHUNK
    wg = w_up[:, :f].reshape(d, nc, FF_CHUNK).transpose(1, 0, 2).astype(BF16)
    wu = w_up[:, f:].reshape(d, nc, FF_CHUNK).transpose(1, 0, 2).astype(BF16)
    wd = w_down.reshape(nc, FF_CHUNK, d).astype(BF16)
    return wg, wu, wd


def kernel(x, c, ctx, c_ctx, ada_w, ada_b, norm_g, ffn_w_up, ffn_w_down, even_w_in, even_w_out, na_rpb,
           dn_conv_w, dn_a_log, dn_dt_bias, dn_norm_w, odd_w_in, odd_w_out, swa_sink, final_norm_g):
    bsz, t, d = x.shape
    tc = ctx.shape[1]
    depth = ada_w.shape[0]
    ctx_row = bsz

    c16 = jnp.concatenate([c, c_ctx[None, :], jnp.zeros((MOD_ROWS - bsz - 1, d), F32)], axis=0)
    mods = _adaln_call(c16, ada_w, ada_b).reshape(depth, N_MOD, MOD_ROWS, 1, d)
    norm_g3 = norm_g.reshape(depth * 3, 1, d)

    h = x
    hc = ctx.reshape(1, bsz * tc, d)
    for i in range(depth):
        need_ctx = i < depth - 1
        j = i // 2
        w1 = _ffn_weights(ffn_w_up[i, 0], ffn_w_down[i, 0])
        w2 = _ffn_weights(ffn_w_up[i, 1], ffn_w_down[i, 1])
        h = _ffn_call(h, mods, norm_g3, i, 0, None, *w1)
        hc = _ffn_call(hc, mods, norm_g3, i, 0, ctx_row, *w1)
        if i % 2 == 0:
            w_in = even_w_in[j]
            z0 = NA_PROJ + DN_QKV
            z1 = z0 + DN_HEADS * DN_DV
            wab = jnp.pad(w_in[:, z1:], ((0, 0), (0, LANES - (w_in.shape[1] - z1))))
            w_parts = [w.astype(BF16) for w in (w_in[:, :NA_PROJ], w_in[:, NA_PROJ:z0], w_in[:, z0:z1], wab)]
            na, dn, z, ab = _proj_even_call(h, mods, norm_g3, i, None, *w_parts)
            nac, dnc, zc, abc = [a.reshape(bsz, tc, a.shape[-1])
                                 for a in _proj_even_call(hc, mods, norm_g3, i, ctx_row, *w_parts)]
            y_na, y_na_c = _na_call(na, nac, _na_bias_table(na_rpb[j]))
            lane_pad = (0, LANES - 2 * DN_HEADS)
            alog = jnp.pad(dn_a_log[j].reshape(-1), lane_pad)[None, :]
            dtb = jnp.pad(dn_dt_bias[j].reshape(-1), lane_pad)[None, :]
            y_dn, y_dn_c = _dn_call(dn, z, ab, dnc, zc, abc, dn_conv_w[j], alog, dtb, dn_norm_w[j][None, :])
            w_out = even_w_out[j].astype(BF16)
            ws = [w_out[:NA_HEADS * NA_HD], w_out[NA_HEADS * NA_HD:]]
            h = _outproj_call(h, mods, i, None, [y_na, y_dn], ws)
            if need_ctx:
                ycs = [y.reshape(1, bsz * tc, y.shape[-1]) for y in (y_na_c, y_dn_c)]
                hc = _outproj_call(hc, mods, i, ctx_row, ycs, ws)
        else:
            w_in = odd_w_in[j]
            qw = SWA_HEADS * SWA_HD
            kw = SWA_KV_HEADS * SWA_HD
            cols = jnp.asarray([hd * SWA_HD + e for hd in _swa_head_order() for e in range(SWA_HD)], jnp.int32)
            wq = w_in[:, :qw][:, cols].astype(BF16)
            wk = w_in[:, qw:qw + kw].astype(BF16)
            wv = w_in[:, qw + kw:].astype(BF16)
            cos, sin = _rope_tables(t)
            q, k, v = _proj_odd_call(h, mods, norm_g3, i, None, wq, wk, wv, cos, sin)
            kc, vc = [a.reshape(bsz, tc, kw) for a in _proj_odd_call(hc, mods, norm_g3, i, ctx_row, None, wk, wv, None, None)]
            y = _swa_call(swa_sink[j], q, k, v, kc, vc)
            h = _outproj_call(h, mods, i, None, [y], [odd_w_out[j][cols].astype(BF16)])
            if need_ctx:
                raise NotImplementedError("context queries of a windowed layer are only needed before a later layer")
        last = i == depth - 1
        h = _ffn_call(h, mods, norm_g3, i, 1, None, *w2, final_g=final_norm_g[None, :] if last else None)
        if need_ctx:
            hc = _ffn_call(hc, mods, norm_g3, i, 1, ctx_row, *w2)
    return h
```

```python
import functools
import math

import jax
import jax.numpy as jnp
from jax import lax
from jax.experimental import pallas as pl
from jax.experimental.pallas import tpu as pltpu

F32 = jnp.float32
BF16 = jnp.bfloat16

GRID_W = 64
N_MOD = 9
NORM_EPS = 1e-6
NEG_INF = -1e30
ROPE_BASE = 10000.0
NA_HEADS = 8
NA_HD = 64
NA_WIN_H = 8
NA_WIN_W = 16
NA_PROJ = 3 * NA_HEADS * NA_HD
DN_HEADS = 4
DN_DK = 128
DN_DV = 128
DN_CONV = 5
DN_CHUNK = 64
DN_QKV = DN_HEADS * (2 * DN_DK + DN_DV)
SWA_HEADS = 16
SWA_KV_HEADS = 4
SWA_HD = 64
SWA_BLOCK = 128

LANES = 128
SUBLANES = 8
VMEM_LIMIT = 56 * 1024 * 1024

TOK_TILE = 512
FF_CHUNK = 256
MOD_ROWS = 16
DN_HEAD_GROUP = 2
DN_PREP_CHUNKS = 4


def _params(*sem):
    return pltpu.CompilerParams(dimension_semantics=sem, vmem_limit_bytes=VMEM_LIMIT)


def _dot(a, b):
    return jnp.dot(a, b, preferred_element_type=F32)


def _dot_nt(a, b):
    return lax.dot_general(a, b, (((1,), (1,)), ((), ())), preferred_element_type=F32)


def _dot_tn(a, b):
    return lax.dot_general(a, b, (((0,), (0,)), ((), ())), preferred_element_type=F32)


def _sigmoid(x):
    return 1.0 / (1.0 + jnp.exp(-x))


def _silu(x):
    return x * _sigmoid(x)


def _softplus(x):
    return jnp.maximum(x, 0.0) + jnp.log1p(jnp.exp(-jnp.abs(x)))


def _modulate(x, g, shift, scale):
    y = x * lax.rsqrt(jnp.mean(x * x, axis=-1, keepdims=True) + NORM_EPS) * g
    return y * (1.0 + scale) + shift


def _split3(x):
    h1 = x.astype(BF16)
    r1 = x - h1.astype(F32)
    h2 = r1.astype(BF16)
    h3 = (r1 - h2.astype(F32)).astype(BF16)
    return h1, h2, h3


def _split2(x):
    hi = x.astype(BF16)
    return hi, (x - hi.astype(F32)).astype(BF16)


def _adaln_kernel(c_ref, w_ref, b_ref, o_ref):
    s = _silu(c_ref[...]).astype(BF16)
    o_ref[...] = _dot(s, w_ref[...].astype(BF16)) + b_ref[...]


def _adaln_call(c16, ada_w, ada_b):
    depth, d, _ = ada_w.shape
    b4 = ada_b.reshape(depth * N_MOD, 1, d)
    return pl.pallas_call(
        _adaln_kernel,
        out_shape=jax.ShapeDtypeStruct((depth, N_MOD, MOD_ROWS, d), F32),
        grid=(depth, N_MOD),
        in_specs=[
            pl.BlockSpec((MOD_ROWS, d), lambda i, k: (0, 0)),
            pl.BlockSpec((None, d, d), lambda i, k: (i, 0, k)),
            pl.BlockSpec((None, 1, d), lambda i, k: (i * N_MOD + k, 0, 0)),
        ],
        out_specs=pl.BlockSpec((None, None, MOD_ROWS, d), lambda i, k: (i, k, 0, 0)),
        compiler_params=_params("arbitrary", "arbitrary"),
        name="adaln",
    )(c16, ada_w, b4)


def _mod_spec(layer, k, ctx_row, d):
    if ctx_row is None:
        return pl.BlockSpec((None, None, None, 1, d), lambda b, t: (layer, k, b, 0, 0))
    return pl.BlockSpec((None, None, None, 1, d), lambda b, t: (layer, k, ctx_row, 0, 0))


def _const_spec(shape):
    nd = len(shape)
    return pl.BlockSpec(shape, lambda b, t: (0,) * nd, pipeline_mode=pl.Buffered(1))


def _ffn_kernel(x_ref, g_ref, sh_ref, sc_ref, gt_ref, wg_ref, wu_ref, wd_ref, *rest, n_chunks, final):
    if final:
        fg_ref, o_ref, u_ref, acc_ref = rest
    else:
        o_ref, u_ref, acc_ref = rest
    u_ref[...] = _modulate(x_ref[...], g_ref[...], sh_ref[...], sc_ref[...]).astype(BF16)
    acc_ref[...] = jnp.zeros_like(acc_ref)

    def body(c, carry):
        u = u_ref[...]
        gate = _dot(u, wg_ref[c])
        up = _dot(u, wu_ref[c])
        a = (_silu(gate) * up).astype(BF16)
        acc_ref[...] += _dot(a, wd_ref[c])
        return carry

    lax.fori_loop(0, n_chunks, body, 0)
    y = x_ref[...] + 0.5 * gt_ref[...] * acc_ref[...]
    if final:
        y = y * lax.rsqrt(jnp.mean(y * y, axis=-1, keepdims=True) + NORM_EPS) * fg_ref[...]
    o_ref[...] = y


def _ffn_call(h, mods, norm_g3, layer, which, ctx_row, wg, wu, wd, final_g=None):
    bsz, t, d = h.shape
    tm = min(TOK_TILE, t)
    n_chunks = wg.shape[0]
    k0 = 6 if which else 0
    in_specs = [
        pl.BlockSpec((None, tm, d), lambda b, i: (b, i, 0)),
        pl.BlockSpec((None, 1, d), lambda b, i: (layer * 3 + (2 if which else 0), 0, 0)),
        _mod_spec(layer, k0, ctx_row, d),
        _mod_spec(layer, k0 + 1, ctx_row, d),
        _mod_spec(layer, k0 + 2, ctx_row, d),
        _const_spec(wg.shape),
        _const_spec(wu.shape),
        _const_spec(wd.shape),
    ]
    args = [h, norm_g3, mods, mods, mods, wg, wu, wd]
    if final_g is not None:
        in_specs.append(pl.BlockSpec((1, d), lambda b, i: (0, 0)))
        args.append(final_g)
    return pl.pallas_call(
        functools.partial(_ffn_kernel, n_chunks=n_chunks, final=final_g is not None),
        out_shape=jax.ShapeDtypeStruct(h.shape, F32),
        grid=(bsz, t // tm),
        in_specs=in_specs,
        out_specs=pl.BlockSpec((None, tm, d), lambda b, i: (b, i, 0)),
        scratch_shapes=[pltpu.VMEM((tm, d), BF16), pltpu.VMEM((tm, d), F32)],
        compiler_params=_params("parallel", "parallel"),
        name="ffn",
    )(*args)


def _proj_even_kernel(x_ref, g_ref, sh_ref, sc_ref, wna_ref, wdn_ref, wz_ref, wab_ref,
                      na_ref, dn_ref, z_ref, ab_ref):
    u = _modulate(x_ref[...], g_ref[...], sh_ref[...], sc_ref[...]).astype(BF16)
    na_ref[...] = _dot(u, wna_ref[...]).astype(BF16)
    dn_ref[...] = _dot(u, wdn_ref[...])
    z_ref[...] = _dot(u, wz_ref[...])
    ab_ref[...] = _dot(u, wab_ref[...])


def _proj_even_call(h, mods, norm_g3, layer, ctx_row, wna, wdn, wz, wab):
    bsz, t, d = h.shape
    tm = min(TOK_TILE, t)
    widths = (wna.shape[1], wdn.shape[1], wz.shape[1], wab.shape[1])
    dtypes = (BF16, F32, F32, F32)
    return pl.pallas_call(
        _proj_even_kernel,
        out_shape=[jax.ShapeDtypeStruct((bsz, t, w), dt) for w, dt in zip(widths, dtypes)],
        grid=(bsz, t // tm),
        in_specs=[
            pl.BlockSpec((None, tm, d), lambda b, i: (b, i, 0)),
            pl.BlockSpec((None, 1, d), lambda b, i: (layer * 3 + 1, 0, 0)),
            _mod_spec(layer, 3, ctx_row, d),
            _mod_spec(layer, 4, ctx_row, d),
            _const_spec(wna.shape), _const_spec(wdn.shape), _const_spec(wz.shape), _const_spec(wab.shape),
        ],
        out_specs=[pl.BlockSpec((None, tm, w), lambda b, i: (b, i, 0)) for w in widths],
        compiler_params=_params("parallel", "parallel"),
        name="proj_even",
    )(h, norm_g3, mods, mods, wna, wdn, wz, wab)


def _rope_slab(x, cos, sin, first):
    swapped = jnp.where(first, pltpu.roll(x, LANES - 16, 1), pltpu.roll(x, 16, 1))
    return x * cos + swapped * sin


def _proj_odd_kernel(x_ref, g_ref, sh_ref, sc_ref, *rest, with_q):
    if with_q:
        wq_ref, wk_ref, wv_ref, cos_ref, sin_ref, q_ref, k_ref, v_ref = rest
    else:
        wk_ref, wv_ref, k_ref, v_ref = rest
    u = _modulate(x_ref[...], g_ref[...], sh_ref[...], sc_ref[...]).astype(BF16)
    v_ref[...] = _dot(u, wv_ref[...]).astype(BF16)
    k = _dot(u, wk_ref[...])
    if not with_q:
        k_ref[...] = k.astype(BF16)
        return
    cos = cos_ref[...]
    sin = sin_ref[...]
    lane = lax.broadcasted_iota(jnp.int32, cos.shape, 1)
    first = (lane % 32) < 16
    for j in range(k.shape[1] // LANES):
        sl = slice(j * LANES, (j + 1) * LANES)
        k_ref[:, sl] = _rope_slab(k[:, sl], cos, sin, first).astype(BF16)
    q = _dot(u, wq_ref[...])
    for j in range(q.shape[1] // LANES):
        sl = slice(j * LANES, (j + 1) * LANES)
        q_ref[:, sl] = _rope_slab(q[:, sl], cos, sin, first).astype(BF16)


def _proj_odd_call(h, mods, norm_g3, layer, ctx_row, wq, wk, wv, cos, sin):
    bsz, t, d = h.shape
    tm = min(TOK_TILE, t)
    with_q = wq is not None
    in_specs = [
        pl.BlockSpec((None, tm, d), lambda b, i: (b, i, 0)),
        pl.BlockSpec((None, 1, d), lambda b, i: (layer * 3 + 1, 0, 0)),
        _mod_spec(layer, 3, ctx_row, d),
        _mod_spec(layer, 4, ctx_row, d),
    ]
    args = [h, norm_g3, mods, mods]
    widths = []
    if with_q:
        in_specs.append(_const_spec(wq.shape))
        args.append(wq)
        widths.append(wq.shape[1])
    in_specs += [_const_spec(wk.shape), _const_spec(wv.shape)]
    args += [wk, wv]
    widths += [wk.shape[1], wv.shape[1]]
    if with_q:
        in_specs += [pl.BlockSpec((tm, LANES), lambda b, i: (i, 0))] * 2
        args += [cos, sin]
    return pl.pallas_call(
        functools.partial(_proj_odd_kernel, with_q=with_q),
        out_shape=[jax.ShapeDtypeStruct((bsz, t, w), BF16) for w in widths],
        grid=(bsz, t // tm),
        in_specs=in_specs,
        out_specs=[pl.BlockSpec((None, tm, w), lambda b, i: (b, i, 0)) for w in widths],
        compiler_params=_params("parallel", "parallel"),
        name="proj_odd",
    )(*args)


def _outproj_kernel(h_ref, gt_ref, *rest, n_in):
    y_refs, w_refs, o_ref = rest[:n_in], rest[n_in:2 * n_in], rest[2 * n_in]
    acc = _dot(y_refs[0][...], w_refs[0][...])
    for y_ref, w_ref in zip(y_refs[1:], w_refs[1:]):
        acc += _dot(y_ref[...], w_ref[...])
    o_ref[...] = h_ref[...] + gt_ref[...] * acc


def _outproj_call(h, mods, layer, ctx_row, ys, ws):
    bsz, t, d = h.shape
    tm = min(TOK_TILE, t)
    in_specs = [pl.BlockSpec((None, tm, d), lambda b, i: (b, i, 0)), _mod_spec(layer, 5, ctx_row, d)]
    in_specs += [pl.BlockSpec((None, tm, y.shape[2]), lambda b, i: (b, i, 0)) for y in ys]
    in_specs += [_const_spec(w.shape) for w in ws]
    return pl.pallas_call(
        functools.partial(_outproj_kernel, n_in=len(ys)),
        out_shape=jax.ShapeDtypeStruct(h.shape, F32),
        grid=(bsz, t // tm),
        in_specs=in_specs,
        out_specs=pl.BlockSpec((None, tm, d), lambda b, i: (b, i, 0)),
        compiler_params=_params("parallel", "parallel"),
        name="outproj",
    )(h, mods, *ys, *ws)


def _softmax_pv(parts, extra_logit=None):
    m = functools.reduce(jnp.maximum, [jnp.max(s, axis=-1, keepdims=True) for s, _ in parts])
    if extra_logit is not None:
        m = jnp.maximum(m, extra_logit)
    den = None
    acc = None
    for s, v in parts:
        p = jnp.exp(s - m)
        ps = jnp.sum(p, axis=-1, keepdims=True)
        den = ps if den is None else den + ps
        pv = _dot(p.astype(BF16), v)
        acc = pv if acc is None else acc + pv
    if extra_logit is not None:
        den = den + jnp.exp(extra_logit - m)
    return acc / den


def _na_kernel(q_ref, k_ref, v_ref, qc_ref, kc_ref, vc_ref, bias_ref, o_ref, oc_ref, *, rows):
    scale = NA_HD ** -0.5
    n_pairs = q_ref.shape[1] // LANES
    tc = qc_ref.shape[0]
    win = NA_WIN_H * GRID_W
    lane_q = lax.broadcasted_iota(jnp.int32, (GRID_W, LANES), 1)
    lane_c = lax.broadcasted_iota(jnp.int32, (tc, LANES), 1)

    def row_body(r, carry):
        r0 = jnp.clip(r - NA_WIN_H // 2, 0, rows - NA_WIN_H)
        doff = r0 - r + NA_WIN_H - 1
        qs = pl.ds(pl.multiple_of(r * GRID_W, GRID_W), GRID_W)
        ks = pl.ds(pl.multiple_of(r0 * GRID_W, GRID_W), win)
        for hp in range(n_pairs):
            sl = slice(hp * LANES, (hp + 1) * LANES)
            q2 = q_ref[qs, sl]
            zero = jnp.zeros_like(q2)
            qm = jnp.concatenate([jnp.where(lane_q < NA_HD, q2, zero), jnp.where(lane_q >= NA_HD, q2, zero)], axis=0)
            bias = jnp.concatenate([bias_ref[hp, doff + 2 * j] for j in range(NA_WIN_H // 2)], axis=1)
            s_win = _dot_nt(qm, k_ref[ks, sl]) * scale + bias
            s_ctx = _dot_nt(qm, kc_ref[:, sl]) * scale
            o = _softmax_pv([(s_win, v_ref[ks, sl]), (s_ctx, vc_ref[:, sl])])
            o_ref[qs, sl] = jnp.where(lane_q < NA_HD, o[:GRID_W], o[GRID_W:]).astype(BF16)
        return carry

    lax.fori_loop(0, rows, row_body, 0)

    for hp in range(n_pairs):
        sl = slice(hp * LANES, (hp + 1) * LANES)
        q2 = qc_ref[:, sl]
        zero = jnp.zeros_like(q2)
        halves = []
        for half in range(2):
            keep = (lane_c >= NA_HD) if half else (lane_c < NA_HD)
            s = _dot_nt(jnp.where(keep, q2, zero), kc_ref[:, sl]) * scale
            halves.append(_softmax_pv([(s, vc_ref[:, sl])]))
        oc_ref[:, sl] = jnp.where(lane_c < NA_HD, halves[0], halves[1]).astype(BF16)


def _na_call(qkv, qkvc, bias):
    bsz, t, w3 = qkv.shape
    tc = qkvc.shape[1]
    w = w3 // 3

    def col(j):
        return lambda b: (b, 0, j)

    return pl.pallas_call(
        functools.partial(_na_kernel, rows=t // GRID_W),
        out_shape=[jax.ShapeDtypeStruct((bsz, t, w), BF16), jax.ShapeDtypeStruct((bsz, tc, w), BF16)],
        grid=(bsz,),
        in_specs=[pl.BlockSpec((None, t, w), col(j)) for j in range(3)]
        + [pl.BlockSpec((None, tc, w), col(j)) for j in range(3)]
        + [pl.BlockSpec(bias.shape, lambda b: (0, 0, 0, 0), pipeline_mode=pl.Buffered(1))],
        out_specs=[pl.BlockSpec((None, t, w), lambda b: (b, 0, 0)), pl.BlockSpec((None, tc, w), lambda b: (b, 0, 0))],
        compiler_params=_params("parallel"),
        name="na_attn",
    )(qkv, qkv, qkv, qkvc, qkvc, qkvc, bias)


def _na_bias_table(rpb):
    c = jnp.arange(GRID_W)[:, None]
    kc = jnp.arange(GRID_W)[None, :]
    cstart = jnp.clip(c - NA_WIN_W // 2, 0, GRID_W - NA_WIN_W)
    ok = (kc >= cstart) & (kc < cstart + NA_WIN_W)
    dc = jnp.clip(kc - c + NA_WIN_W - 1, 0, 2 * NA_WIN_W - 2)
    t = jnp.where(ok[None, None], rpb.astype(F32)[:, :, dc], NEG_INF)
    nd = 2 * NA_WIN_H - 2
    pair = jnp.concatenate([t[:, :nd], t[:, 1:nd + 1]], axis=-1)
    h = rpb.shape[0]
    pair = pair.reshape(h // 2, 2, nd, GRID_W, 2 * GRID_W)
    return jnp.moveaxis(pair, 1, 2).reshape(h // 2, nd, 2 * GRID_W, 2 * GRID_W)


def _dn_kernel(qr_ref, kr_ref, vr_ref, z_ref, ab_ref, qcr_ref, kcr_ref, vcr_ref, zc_ref, abc_ref,
               cwq_ref, cwk_ref, cwv_ref, alog_ref, dtb_ref, nw_ref,
               y_ref, yc_ref,
               pad_ref, abs_ref, aq_ref, b_ref, gl_ref, o_ref, s_ref, *, hg):
    c = DN_CHUNK
    t = qr_ref.shape[0]
    tc = qcr_ref.shape[0]
    ncc, ncl = tc // c, t // c
    lat0 = 2 * SUBLANES + tc
    half = DN_CONV // 2
    n_sq = int(math.log2(c)) - 1

    zeros8 = jnp.zeros((3, SUBLANES, LANES), F32)
    pad_ref[:, 0:SUBLANES, :] = zeros8
    pad_ref[:, SUBLANES + tc:lat0, :] = zeros8
    pad_ref[:, lat0 + t:lat0 + t + SUBLANES, :] = zeros8
    abs_ref[0:tc, :] = abc_ref[...]
    abs_ref[tc:tc + t, :] = ab_ref[...]
    s_ref[...] = jnp.zeros_like(s_ref)

    lane = lax.broadcasted_iota(jnp.int32, (c, LANES), 1)
    row = lax.broadcasted_iota(jnp.int32, (c, LANES), 0)
    col = jnp.bitwise_and(lane, c - 1)
    fwd = lane < c
    incl = (fwd & (row >= col)) | (~fwd & (row <= col))
    strict = (fwd & (row > col)) | (~fwd & (row < col))
    diag = row == col
    eye2 = jnp.where(diag, 1.0, 0.0).astype(F32)
    fwd16 = jnp.where(fwd, 1.0, 0.0).astype(BF16)
    bwd16 = jnp.where(fwd, 0.0, 1.0).astype(BF16)
    ri = lax.broadcasted_iota(jnp.int32, (2 * c, c), 0)
    ci_ = lax.broadcasted_iota(jnp.int32, (2 * c, c), 1)
    tri2 = jnp.where(((ri < c) & (ci_ <= ri)) | ((ri >= c) & (ci_ >= ri - c)), 1.0, 0.0).astype(BF16)
    tri6 = jnp.concatenate([tri2, tri2, tri2], axis=1)
    neg_a = -jnp.exp(alog_ref[...])
    dtb = dtb_ref[...]

    def l2n(x):
        return x * lax.rsqrt(jnp.sum(x * x, axis=-1, keepdims=True) + NORM_EPS)

    def bd(m16):
        return jnp.concatenate([m16 * fwd16, m16 * bwd16], axis=0)

    def mm3(lhs_parts, bh, bl):
        bdh = bd(bh)
        rhs = jnp.concatenate([bdh, bdh, bd(bl)], axis=0)
        lhs = jnp.concatenate([jnp.concatenate([ah, al, ah], axis=1) for ah, al in lhs_parts], axis=0)
        return _dot(lhs, rhs)

    def prep_head(hh):
        hid = pl.program_id(1) * hg + hh
        hs = slice(hh * LANES, (hh + 1) * LANES)
        for s, (cr, lr) in enumerate(((qcr_ref, qr_ref), (kcr_ref, kr_ref), (vcr_ref, vr_ref))):
            pad_ref[s, SUBLANES:SUBLANES + tc, :] = cr[:, hs]
            pad_ref[s, lat0:lat0 + t, :] = lr[:, hs]

        def conv(s, cw_ref, base):
            win = pad_ref[s, pl.ds(pl.multiple_of(base - SUBLANES, SUBLANES), c + 2 * SUBLANES), :]
            acc = win[SUBLANES - half:SUBLANES - half + c] * cw_ref[0:1, hs]
            for j in range(1, DN_CONV):
                acc += win[SUBLANES - half + j:SUBLANES - half + j + c] * cw_ref[j:j + 1, hs]
            return _silu(acc)

        def pick(x, idx):
            return jnp.broadcast_to(jnp.sum(jnp.where(lane == idx, x, 0.0), axis=-1, keepdims=True), (c, LANES))

        def stage_inputs(ch):
            base = ch * c + jnp.where(ch < ncc, SUBLANES, 2 * SUBLANES)
            q = l2n(conv(0, cwq_ref, base))
            k = l2n(conv(1, cwk_ref, base))
            v = conv(2, cwv_ref, base)
            abt = abs_ref[pl.ds(pl.multiple_of(ch * c, c), c), :]
            g_all = neg_a * _softplus(abt + dtb)
            b_all = _sigmoid(abt)
            return q * DN_DK ** -0.5, k, v, g_all, b_all

        def stage_decay(g_all):
            parts = []
            for gp in _split3(g_all):
                gp = gp.astype(F32)
                pf = pick(gp, hid)
                pb = pick(gp, DN_HEADS + hid)
                parts.append(jnp.concatenate([jnp.where(strict, jnp.where(fwd, pf, pb), 0.0), pf, pb], axis=1).astype(BF16))
            res = _dot(tri6, jnp.concatenate(parts, axis=0))
            return jnp.where(fwd, res[:c, 0:LANES], res[c:, 0:LANES]), res[:c, LANES:2 * LANES], res[c:, 2 * LANES:]

        def stage_finish(ch, qs, k, v, qk16, p, gc_f, gc_b, bt_f, bt_b):
            e_f = jnp.exp(gc_f)
            e_b = jnp.exp(gc_b)
            rhs = jnp.concatenate([
                jnp.concatenate([v * bt_f, k * (bt_f * e_f)], axis=1),
                jnp.concatenate([v * bt_b, k * (bt_b * e_b)], axis=1)], axis=0).astype(BF16)
            sol16 = _dot(bd(p.astype(BF16)), rhs).astype(BF16)
            z16 = jnp.zeros((c, 2 * DN_DV), BF16)
            both = jnp.concatenate([jnp.concatenate([sol16[:c], z16], axis=1),
                                    jnp.concatenate([z16, sol16[c:]], axis=1)], axis=0)
            gl_f = gc_f[c - 1:c, :]
            gl_b = gc_b[0:1, :]
            kd = jnp.concatenate([k * jnp.exp(gl_f - gc_f), k * jnp.exp(gl_b - gc_b)], axis=0).astype(BF16)
            qsol = _dot(qk16, both)
            ksol = _dot_tn(kd, both)
            rows = pl.ds(pl.multiple_of(ch * c, c), c)
            for d, (e, gl) in enumerate(((e_f, gl_f), (e_b, gl_b))):
                lo = 2 * DN_DV * d
                aq_ref[hh, d, ch, 0:DN_DK, :] = ksol[:, lo + DN_DV:lo + 2 * DN_DV].astype(BF16)
                aq_ref[hh, d, ch, DN_DK:DN_DK + c, :] = (qs * e - qsol[:, lo + DN_DV:lo + 2 * DN_DV]).astype(BF16)
                b_ref[hh, d, ch] = ksol[:, lo:lo + DN_DV]
                o_ref[hh, d, rows, :] = qsol[:, lo:lo + DN_DV]
                gl_ref[hh, d, ch] = jnp.broadcast_to(jnp.exp(gl), (SUBLANES, LANES))

        def body(i, carry):
            chs = [i * DN_PREP_CHUNKS + j for j in range(DN_PREP_CHUNKS)]
            ins = [stage_inputs(ch) for ch in chs]
            dec = [stage_decay(x[3]) for x in ins]
            bts = [(pick(x[4], 2 * DN_HEADS + hid), pick(x[4], 3 * DN_HEADS + hid)) for x in ins]
            decay = [jnp.where(incl, jnp.exp(jnp.where(incl, d[0], 0.0)), 0.0) for d in dec]
            k16 = [x[1].astype(BF16) for x in ins]
            kk = [jnp.concatenate([a, a], axis=0) for a in k16]
            qk16 = [(_dot_nt(x[0].astype(BF16), b) * dc).astype(BF16) for x, b, dc in zip(ins, kk, decay)]
            lmat = [jnp.where(strict, jnp.where(fwd, bt[0], bt[1]) * _dot_nt(a, b) * dc, 0.0)
                    for a, b, bt, dc in zip(k16, kk, bts, decay)]
            lsp = [_split2(x) for x in lmat]
            m = [mm3([s], s[0], s[1]) for s in lsp]
            p = [eye2 - x for x in lmat]
            for s in range(n_sq):
                msp = [_split2(x) for x in m]
                psp = [_split2(x) for x in p]
                if s < n_sq - 1:
                    res = [mm3([a, b], b[0], b[1]) for a, b in zip(psp, msp)]
                    p = [a + r[:c] for a, r in zip(p, res)]
                    m = [r[c:] for r in res]
                else:
                    p = [a + mm3([b], e[0], e[1]) for a, b, e in zip(p, psp, msp)]
            for j, ch in enumerate(chs):
                stage_finish(ch, ins[j][0], ins[j][1], ins[j][2], qk16[j], p[j], dec[j][1], dec[j][2], *bts[j])
            return carry

        lax.fori_loop(0, (ncc + ncl) // DN_PREP_CHUNKS, body, 0)

    for hh in range(hg):
        prep_head(hh)

    def scan(first, n):
        def body(i, carry):
            for hh in range(hg):
                for d in range(2):
                    ch = first + i if d == 0 else first + n - 1 - i
                    s_mat = s_ref[hh, d]
                    res = _dot(aq_ref[hh, d, ch], s_mat.astype(BF16))
                    s_ref[hh, d] = s_mat * gl_ref[hh, d, ch][0:1, :] + (b_ref[hh, d, ch] - res[:DN_DK])
                    rows = pl.ds(pl.multiple_of(ch * c, c), c)
                    o_ref[hh, d, rows, :] = o_ref[hh, d, rows, :] + res[DN_DK:]
            return carry
        return body

    lax.fori_loop(0, ncc, scan(0, ncc), 0)
    lax.fori_loop(0, ncl, scan(ncc, ncl), 0)

    def gated_norm(o, z):
        return (o * lax.rsqrt(jnp.mean(o * o, axis=-1, keepdims=True) + NORM_EPS) * nw_ref[...] * _silu(z))

    piece = 4 * c
    for hh in range(hg):
        hs = slice(hh * LANES, (hh + 1) * LANES)
        yc_ref[:, hs] = gated_norm(o_ref[hh, 0, 0:tc, :] + o_ref[hh, 1, 0:tc, :], zc_ref[:, hs]).astype(BF16)
        for j in range(t // piece):
            rows = slice(j * piece, (j + 1) * piece)
            orow = slice(tc + j * piece, tc + (j + 1) * piece)
            y_ref[rows, hs] = gated_norm(o_ref[hh, 0, orow, :] + o_ref[hh, 1, orow, :], z_ref[rows, hs]).astype(BF16)


def _dn_call(dn, z, ab, dnc, zc, abc, conv_w, alog, dtb, norm_w):
    bsz, t, _ = dn.shape
    tc = dnc.shape[1]
    h = DN_HEADS
    hg = DN_HEAD_GROUP
    nch = (t + tc) // DN_CHUNK
    c = DN_CHUNK
    wide = hg * LANES

    def slab(rows, j0):
        return pl.BlockSpec((None, rows, wide), lambda b, g: (b, 0, j0 // hg + g))

    def whole(rows):
        return pl.BlockSpec((None, rows, LANES), lambda b, g: (b, 0, 0))

    def cw(j0):
        return pl.BlockSpec((DN_CONV, wide), lambda b, g: (0, j0 // hg + g))

    vec = pl.BlockSpec((1, LANES), lambda b, g: (0, 0))
    return pl.pallas_call(
        functools.partial(_dn_kernel, hg=hg),
        out_shape=[jax.ShapeDtypeStruct((bsz, t, h * DN_DV), BF16), jax.ShapeDtypeStruct((bsz, tc, h * DN_DV), BF16)],
        grid=(bsz, h // hg),
        in_specs=[slab(t, 0), slab(t, h), slab(t, 2 * h), slab(t, 0), whole(t),
                  slab(tc, 0), slab(tc, h), slab(tc, 2 * h), slab(tc, 0), whole(tc),
                  cw(0), cw(h), cw(2 * h), vec, vec, vec],
        out_specs=[slab(t, 0), slab(tc, 0)],
        scratch_shapes=[
            pltpu.VMEM((3, 3 * SUBLANES + tc + t, LANES), F32),
            pltpu.VMEM((tc + t, LANES), F32),
            pltpu.VMEM((hg, 2, nch, DN_DK + c, DN_DV), BF16),
            pltpu.VMEM((hg, 2, nch, DN_DK, DN_DV), F32),
            pltpu.VMEM((hg, 2, nch, SUBLANES, LANES), F32),
            pltpu.VMEM((hg, 2, tc + t, DN_DV), F32),
            pltpu.VMEM((hg, 2, DN_DK, DN_DV), F32),
        ],
        compiler_params=_params("parallel", "parallel"),
        name="deltanet",
    )(dn, dn, dn, z, ab, dnc, dnc, dnc, zc, abc, conv_w, conv_w, conv_w, alog, dtb, norm_w)


def _swa_kernel(sink_ref, q_ref, k_ref, v_ref, kc_ref, vc_ref, o_ref, *, n_blocks):
    scale = SWA_HD ** -0.5
    blk = SWA_BLOCK
    n_kv_pairs = k_ref.shape[1] // LANES
    slabs = q_ref.shape[1] // LANES // n_kv_pairs
    rows = slabs * blk
    lane = lax.broadcasted_iota(jnp.int32, (rows, LANES), 1)
    qi = lax.broadcasted_iota(jnp.int32, (rows, blk), 0) % blk
    kj = lax.broadcasted_iota(jnp.int32, (rows, blk), 1)

    def block_body(n, carry):
        lo = jnp.maximum(n - 1, 0)
        hi = jnp.minimum(n + 1, n_blocks - 1)
        ok_lo = (kj >= qi) & (n > 0)
        ok_hi = (kj <= qi) & (n < n_blocks - 1)

        def tok(i):
            return pl.ds(pl.multiple_of(i * blk, blk), blk)

        for p in range(n_kv_pairs):
            kvl = slice(p * LANES, (p + 1) * LANES)
            k3 = jnp.concatenate([k_ref[tok(lo), kvl], k_ref[tok(n), kvl], k_ref[tok(hi), kvl]], axis=0)
            v3 = jnp.concatenate([v_ref[tok(lo), kvl], v_ref[tok(n), kvl], v_ref[tok(hi), kvl]], axis=0)
            qs = jnp.concatenate(
                [q_ref[tok(n), (p * slabs + j) * LANES:(p * slabs + j + 1) * LANES] for j in range(slabs)], axis=0)
            zero = jnp.zeros_like(qs)
            halves = []
            for half in range(2):
                keep = (lane >= SWA_HD) if half else (lane < SWA_HD)
                qm = jnp.where(keep, qs, zero)
                s3 = _dot_nt(qm, k3) * scale
                s_win = jnp.concatenate([
                    jnp.where(ok_lo, s3[:, :blk], NEG_INF),
                    s3[:, blk:2 * blk],
                    jnp.where(ok_hi, s3[:, 2 * blk:], NEG_INF)], axis=1)
                s_ctx = _dot_nt(qm, kc_ref[:, kvl]) * scale
                sink = jnp.concatenate(
                    [jnp.full((blk, 1), sink_ref[(2 * p + half) * slabs + j], F32) for j in range(slabs)], axis=0)
                halves.append(_softmax_pv([(s_win, v3), (s_ctx, vc_ref[:, kvl])], extra_logit=sink))
            o = jnp.where(lane < SWA_HD, halves[0], halves[1]).astype(BF16)
            for j in range(slabs):
                o_ref[tok(n), (p * slabs + j) * LANES:(p * slabs + j + 1) * LANES] = o[j * blk:(j + 1) * blk]
        return carry

    lax.fori_loop(0, n_blocks, block_body, 0)


def _swa_call(sink, q, k, v, kc, vc):
    bsz, t, wq = q.shape
    wk = k.shape[2]
    tc = kc.shape[1]
    return pl.pallas_call(
        functools.partial(_swa_kernel, n_blocks=t // SWA_BLOCK),
        out_shape=jax.ShapeDtypeStruct((bsz, t, wq), BF16),
        grid=(bsz,),
        in_specs=[
            pl.BlockSpec(memory_space=pltpu.SMEM),
            pl.BlockSpec((None, t, wq), lambda b: (b, 0, 0)),
            pl.BlockSpec((None, t, wk), lambda b: (b, 0, 0)),
            pl.BlockSpec((None, t, wk), lambda b: (b, 0, 0)),
            pl.BlockSpec((None, tc, wk), lambda b: (b, 0, 0)),
            pl.BlockSpec((None, tc, wk), lambda b: (b, 0, 0)),
        ],
        out_specs=pl.BlockSpec((None, t, wq), lambda b: (b, 0, 0)),
        compiler_params=_params("parallel"),
        name="swa_attn",
    )(sink, q, k, v, kc, vc)


def _swa_head_order():
    rep = SWA_HEADS // SWA_KV_HEADS
    order = []
    for p in range(SWA_KV_HEADS // 2):
        for j in range(rep):
            order += [2 * p * rep + j, (2 * p + 1) * rep + j]
    return order


def _rope_tables(t):
    pos = jnp.arange(t)
    half = SWA_HD // 2
    inv = jnp.power(ROPE_BASE, -jnp.arange(0, half, 2, dtype=F32) / half)
    ang_r = (pos // GRID_W).astype(F32)[:, None] * inv
    ang_c = (pos % GRID_W).astype(F32)[:, None] * inv
    cos = jnp.concatenate([jnp.cos(ang_r)] * 2 + [jnp.cos(ang_c)] * 2, axis=-1)
    sin = jnp.concatenate([-jnp.sin(ang_r), jnp.sin(ang_r), -jnp.sin(ang_c), jnp.sin(ang_c)], axis=-1)
    return jnp.tile(cos, (1, LANES // SWA_HD)), jnp.tile(sin, (1, LANES // SWA_HD))


def _ffn_weights(w_up, w_down):
    d, f2 = w_up.shape
    f = f2 // 2
    nc = f // FF_CHUNK
    wg = w_up[:, :f].reshape(d, nc, FF_CHUNK).transpose(1, 0, 2).astype(BF16)
    wu = w_up[:, f:].reshape(d, nc, FF_CHUNK).transpose(1, 0, 2).astype(BF16)
    wd = w_down.reshape(nc, FF_CHUNK, d).astype(BF16)
    return wg, wu, wd


def kernel(x, c, ctx, c_ctx, ada_w, ada_b, norm_g, ffn_w_up, ffn_w_down, even_w_in, even_w_out, na_rpb,
           dn_conv_w, dn_a_log, dn_dt_bias, dn_norm_w, odd_w_in, odd_w_out, swa_sink, final_norm_g):
    bsz, t, d = x.shape
    tc = ctx.shape[1]
    depth = ada_w.shape[0]
    ctx_row = bsz

    c16 = jnp.concatenate([c, c_ctx[None, :], jnp.zeros((MOD_ROWS - bsz - 1, d), F32)], axis=0)
    mods = _adaln_call(c16, ada_w, ada_b).reshape(depth, N_MOD, MOD_ROWS, 1, d)
    norm_g3 = norm_g.reshape(depth * 3, 1, d)

    h = x
    hc = ctx.reshape(1, bsz * tc, d)
    for i in range(depth):
        need_ctx = i < depth - 1
        j = i // 2
        w1 = _ffn_weights(ffn_w_up[i, 0], ffn_w_down[i, 0])
        w2 = _ffn_weights(ffn_w_up[i, 1], ffn_w_down[i, 1])
        h = _ffn_call(h, mods, norm_g3, i, 0, None, *w1)
        hc = _ffn_call(hc, mods, norm_g3, i, 0, ctx_row, *w1)
        if i % 2 == 0:
            w_in = even_w_in[j]
            z0 = NA_PROJ + DN_QKV
            z1 = z0 + DN_HEADS * DN_DV
            wab = jnp.pad(w_in[:, z1:], ((0, 0), (0, LANES - (w_in.shape[1] - z1))))
            w_parts = [w.astype(BF16) for w in (w_in[:, :NA_PROJ], w_in[:, NA_PROJ:z0], w_in[:, z0:z1], wab)]
            na, dn, z, ab = _proj_even_call(h, mods, norm_g3, i, None, *w_parts)
            nac, dnc, zc, abc = [a.reshape(bsz, tc, a.shape[-1])
                                 for a in _proj_even_call(hc, mods, norm_g3, i, ctx_row, *w_parts)]
            y_na, y_na_c = _na_call(na, nac, _na_bias_table(na_rpb[j]))
            lane_pad = (0, LANES - 2 * DN_HEADS)
            alog = jnp.pad(dn_a_log[j].reshape(-1), lane_pad)[None, :]
            dtb = jnp.pad(dn_dt_bias[j].reshape(-1), lane_pad)[None, :]
            y_dn, y_dn_c = _dn_call(dn, z, ab, dnc, zc, abc, dn_conv_w[j], alog, dtb, dn_norm_w[j][None, :])
            w_out = even_w_out[j].astype(BF16)
            ws = [w_out[:NA_HEADS * NA_HD], w_out[NA_HEADS * NA_HD:]]
            h = _outproj_call(h, mods, i, None, [y_na, y_dn], ws)
            if need_ctx:
                ycs = [y.reshape(1, bsz * tc, y.shape[-1]) for y in (y_na_c, y_dn_c)]
                hc = _outproj_call(hc, mods, i, ctx_row, ycs, ws)
        else:
            w_in = odd_w_in[j]
            qw = SWA_HEADS * SWA_HD
            kw = SWA_KV_HEADS * SWA_HD
            cols = jnp.asarray([hd * SWA_HD + e for hd in _swa_head_order() for e in range(SWA_HD)], jnp.int32)
            wq = w_in[:, :qw][:, cols].astype(BF16)
            wk = w_in[:, qw:qw + kw].astype(BF16)
            wv = w_in[:, qw + kw:].astype(BF16)
            cos, sin = _rope_tables(t)
            q, k, v = _proj_odd_call(h, mods, norm_g3, i, None, wq, wk, wv, cos, sin)
            kc, vc = [a.reshape(bsz, tc, kw) for a in _proj_odd_call(hc, mods, norm_g3, i, ctx_row, None, wk, wv, None, None)]
            y = _swa_call(swa_sink[j], q, k, v, kc, vc)
            h = _outproj_call(h, mods, i, None, [y], [odd_w_out[j][cols].astype(BF16)])
            if need_ctx:
                raise NotImplementedError("context queries of a windowed layer are only needed before a later layer")
        last = i == depth - 1
        h = _ffn_call(h, mods, norm_g3, i, 1, None, *w2, final_g=final_norm_g[None, :] if last else None)
        if need_ctx:
            hc = _ffn_call(hc, mods, norm_g3, i, 1, ctx_row, *w2)
    return h
```

```python
import functools
import math

import jax
import jax.numpy as jnp
from jax import lax
from jax.experimental import pallas as pl
from jax.experimental.pallas import tpu as pltpu

F32 = jnp.float32
BF16 = jnp.bfloat16

GRID_W = 64
N_MOD = 9
NORM_EPS = 1e-6
NEG_INF = -1e30
LOG2E = math.log2(math.e)
ROPE_BASE = 10000.0
NA_HEADS = 8
NA_HD = 64
NA_WIN_H = 8
NA_WIN_W = 16
NA_PROJ = 3 * NA_HEADS * NA_HD
DN_HEADS = 4
DN_DK = 128
DN_DV = 128
DN_CONV = 5
DN_CHUNK = 64
DN_QKV = DN_HEADS * (2 * DN_DK + DN_DV)
SWA_HEADS = 16
SWA_KV_HEADS = 4
SWA_HD = 64
SWA_BLOCK = 128

LANES = 128
SUBLANES = 8
VMEM_LIMIT = 56 * 1024 * 1024

TOK_TILE = 512
FF_CHUNK = 256
MOD_ROWS = 16
DN_HEAD_GROUP = 2
DN_PREP_CHUNKS = 4


def _params(*sem):
    return pltpu.CompilerParams(dimension_semantics=sem, vmem_limit_bytes=VMEM_LIMIT)


def _dot(a, b):
    return jnp.dot(a, b, preferred_element_type=F32)


def _dot_nt(a, b):
    return lax.dot_general(a, b, (((1,), (1,)), ((), ())), preferred_element_type=F32)


def _dot_tn(a, b):
    return lax.dot_general(a, b, (((0,), (0,)), ((), ())), preferred_element_type=F32)


def _sigmoid(x):
    return 1.0 / (1.0 + jnp.exp(-x))


def _silu(x):
    return x * _sigmoid(x)


def _softplus(x):
    return jnp.maximum(x, 0.0) + jnp.log1p(jnp.exp(-jnp.abs(x)))


def _modulate(x, g, shift, scale):
    y = x * lax.rsqrt(jnp.mean(x * x, axis=-1, keepdims=True) + NORM_EPS) * g
    return y * (1.0 + scale) + shift


def _split3(x):
    h1 = x.astype(BF16)
    r1 = x - h1.astype(F32)
    h2 = r1.astype(BF16)
    h3 = (r1 - h2.astype(F32)).astype(BF16)
    return h1, h2, h3


def _split2(x):
    hi = x.astype(BF16)
    return hi, (x - hi.astype(F32)).astype(BF16)


def _adaln_kernel(c_ref, w_ref, b_ref, o_ref):
    s = _silu(c_ref[...]).astype(BF16)
    o_ref[...] = _dot(s, w_ref[...].astype(BF16)) + b_ref[...]


def _adaln_call(c16, ada_w, ada_b):
    depth, d, _ = ada_w.shape
    b4 = ada_b.reshape(depth * N_MOD, 1, d)
    return pl.pallas_call(
        _adaln_kernel,
        out_shape=jax.ShapeDtypeStruct((depth, N_MOD, MOD_ROWS, d), F32),
        grid=(depth, N_MOD),
        in_specs=[
            pl.BlockSpec((MOD_ROWS, d), lambda i, k: (0, 0)),
            pl.BlockSpec((None, d, d), lambda i, k: (i, 0, k)),
            pl.BlockSpec((None, 1, d), lambda i, k: (i * N_MOD + k, 0, 0)),
        ],
        out_specs=pl.BlockSpec((None, None, MOD_ROWS, d), lambda i, k: (i, k, 0, 0)),
        compiler_params=_params("arbitrary", "arbitrary"),
        name="adaln",
    )(c16, ada_w, b4)


def _mod_spec(layer, k, ctx_row, d):
    if ctx_row is None:
        return pl.BlockSpec((None, None, None, 1, d), lambda b, t: (layer, k, b, 0, 0))
    return pl.BlockSpec((None, None, None, 1, d), lambda b, t: (layer, k, ctx_row, 0, 0))


def _const_spec(shape):
    nd = len(shape)
    return pl.BlockSpec(shape, lambda b, t: (0,) * nd, pipeline_mode=pl.Buffered(1))


def _ffn_kernel(x_ref, g_ref, sh_ref, sc_ref, gt_ref, wg_ref, wu_ref, wd_ref, *rest, n_chunks, final):
    if final:
        fg_ref, o_ref, u_ref, acc_ref = rest
    else:
        o_ref, u_ref, acc_ref = rest
    u_ref[...] = _modulate(x_ref[...], g_ref[...], sh_ref[...], sc_ref[...]).astype(BF16)
    acc_ref[...] = jnp.zeros_like(acc_ref)

    def body(c, carry):
        u = u_ref[...]
        gate = _dot(u, wg_ref[c])
        up = _dot(u, wu_ref[c])
        a = (_silu(gate) * up).astype(BF16)
        acc_ref[...] += _dot(a, wd_ref[c])
        return carry

    lax.fori_loop(0, n_chunks, body, 0)
    y = x_ref[...] + 0.5 * gt_ref[...] * acc_ref[...]
    if final:
        y = y * lax.rsqrt(jnp.mean(y * y, axis=-1, keepdims=True) + NORM_EPS) * fg_ref[...]
    o_ref[...] = y


def _ffn_call(h, mods, norm_g3, layer, which, ctx_row, wg, wu, wd, final_g=None):
    bsz, t, d = h.shape
    tm = min(TOK_TILE, t)
    n_chunks = wg.shape[0]
    k0 = 6 if which else 0
    in_specs = [
        pl.BlockSpec((None, tm, d), lambda b, i: (b, i, 0)),
        pl.BlockSpec((None, 1, d), lambda b, i: (layer * 3 + (2 if which else 0), 0, 0)),
        _mod_spec(layer, k0, ctx_row, d),
        _mod_spec(layer, k0 + 1, ctx_row, d),
        _mod_spec(layer, k0 + 2, ctx_row, d),
        _const_spec(wg.shape),
        _const_spec(wu.shape),
        _const_spec(wd.shape),
    ]
    args = [h, norm_g3, mods, mods, mods, wg, wu, wd]
    if final_g is not None:
        in_specs.append(pl.BlockSpec((1, d), lambda b, i: (0, 0)))
        args.append(final_g)
    return pl.pallas_call(
        functools.partial(_ffn_kernel, n_chunks=n_chunks, final=final_g is not None),
        out_shape=jax.ShapeDtypeStruct(h.shape, F32),
        grid=(bsz, t // tm),
        in_specs=in_specs,
        out_specs=pl.BlockSpec((None, tm, d), lambda b, i: (b, i, 0)),
        scratch_shapes=[pltpu.VMEM((tm, d), BF16), pltpu.VMEM((tm, d), F32)],
        compiler_params=_params("parallel", "parallel"),
        name="ffn",
    )(*args)


def _proj_even_kernel(x_ref, g_ref, sh_ref, sc_ref, wna_ref, wdn_ref, wz_ref, wab_ref,
                      na_ref, dn_ref, z_ref, ab_ref):
    u = _modulate(x_ref[...], g_ref[...], sh_ref[...], sc_ref[...]).astype(BF16)
    na = _dot(u, wna_ref[...])
    qw = NA_HEADS * NA_HD
    na_ref[:, :qw] = (na[:, :qw] * (NA_HD ** -0.5 * LOG2E)).astype(BF16)
    na_ref[:, qw:] = na[:, qw:].astype(BF16)
    dn_ref[...] = _dot(u, wdn_ref[...])
    z_ref[...] = _dot(u, wz_ref[...])
    ab_ref[...] = _dot(u, wab_ref[...])


def _proj_even_call(h, mods, norm_g3, layer, ctx_row, wna, wdn, wz, wab):
    bsz, t, d = h.shape
    tm = min(TOK_TILE, t)
    widths = (wna.shape[1], wdn.shape[1], wz.shape[1], wab.shape[1])
    dtypes = (BF16, F32, F32, F32)
    return pl.pallas_call(
        _proj_even_kernel,
        out_shape=[jax.ShapeDtypeStruct((bsz, t, w), dt) for w, dt in zip(widths, dtypes)],
        grid=(bsz, t // tm),
        in_specs=[
            pl.BlockSpec((None, tm, d), lambda b, i: (b, i, 0)),
            pl.BlockSpec((None, 1, d), lambda b, i: (layer * 3 + 1, 0, 0)),
            _mod_spec(layer, 3, ctx_row, d),
            _mod_spec(layer, 4, ctx_row, d),
            _const_spec(wna.shape), _const_spec(wdn.shape), _const_spec(wz.shape), _const_spec(wab.shape),
        ],
        out_specs=[pl.BlockSpec((None, tm, w), lambda b, i: (b, i, 0)) for w in widths],
        compiler_params=_params("parallel", "parallel"),
        name="proj_even",
    )(h, norm_g3, mods, mods, wna, wdn, wz, wab)


def _rope_slab(x, cos, sin, first):
    swapped = jnp.where(first, pltpu.roll(x, LANES - 16, 1), pltpu.roll(x, 16, 1))
    return x * cos + swapped * sin


def _proj_odd_kernel(x_ref, g_ref, sh_ref, sc_ref, *rest, with_q):
    if with_q:
        wq_ref, wk_ref, wv_ref, cos_ref, sin_ref, q_ref, k_ref, v_ref = rest
    else:
        wk_ref, wv_ref, k_ref, v_ref = rest
    u = _modulate(x_ref[...], g_ref[...], sh_ref[...], sc_ref[...]).astype(BF16)
    v_ref[...] = _dot(u, wv_ref[...]).astype(BF16)
    k = _dot(u, wk_ref[...])
    if not with_q:
        k_ref[...] = k.astype(BF16)
        return
    cos = cos_ref[...]
    sin = sin_ref[...]
    lane = lax.broadcasted_iota(jnp.int32, cos.shape, 1)
    first = (lane % 32) < 16
    for j in range(k.shape[1] // LANES):
        sl = slice(j * LANES, (j + 1) * LANES)
        k_ref[:, sl] = _rope_slab(k[:, sl], cos, sin, first).astype(BF16)
    q = _dot(u, wq_ref[...])
    for j in range(q.shape[1] // LANES):
        sl = slice(j * LANES, (j + 1) * LANES)
        q_ref[:, sl] = (_rope_slab(q[:, sl], cos, sin, first) * (SWA_HD ** -0.5 * LOG2E)).astype(BF16)


def _proj_odd_call(h, mods, norm_g3, layer, ctx_row, wq, wk, wv, cos, sin):
    bsz, t, d = h.shape
    tm = min(TOK_TILE, t)
    with_q = wq is not None
    in_specs = [
        pl.BlockSpec((None, tm, d), lambda b, i: (b, i, 0)),
        pl.BlockSpec((None, 1, d), lambda b, i: (layer * 3 + 1, 0, 0)),
        _mod_spec(layer, 3, ctx_row, d),
        _mod_spec(layer, 4, ctx_row, d),
    ]
    args = [h, norm_g3, mods, mods]
    widths = []
    if with_q:
        in_specs.append(_const_spec(wq.shape))
        args.append(wq)
        widths.append(wq.shape[1])
    in_specs += [_const_spec(wk.shape), _const_spec(wv.shape)]
    args += [wk, wv]
    widths += [wk.shape[1], wv.shape[1]]
    if with_q:
        in_specs += [pl.BlockSpec((tm, LANES), lambda b, i: (i, 0))] * 2
        args += [cos, sin]
    return pl.pallas_call(
        functools.partial(_proj_odd_kernel, with_q=with_q),
        out_shape=[jax.ShapeDtypeStruct((bsz, t, w), BF16) for w in widths],
        grid=(bsz, t // tm),
        in_specs=in_specs,
        out_specs=[pl.BlockSpec((None, tm, w), lambda b, i: (b, i, 0)) for w in widths],
        compiler_params=_params("parallel", "parallel"),
        name="proj_odd",
    )(*args)


def _outproj_kernel(h_ref, gt_ref, *rest, n_in):
    y_refs, w_refs, o_ref = rest[:n_in], rest[n_in:2 * n_in], rest[2 * n_in]
    acc = _dot(y_refs[0][...], w_refs[0][...])
    for y_ref, w_ref in zip(y_refs[1:], w_refs[1:]):
        acc += _dot(y_ref[...], w_ref[...])
    o_ref[...] = h_ref[...] + gt_ref[...] * acc


def _outproj_call(h, mods, layer, ctx_row, ys, ws):
    bsz, t, d = h.shape
    tm = min(TOK_TILE, t)
    in_specs = [pl.BlockSpec((None, tm, d), lambda b, i: (b, i, 0)), _mod_spec(layer, 5, ctx_row, d)]
    in_specs += [pl.BlockSpec((None, tm, y.shape[2]), lambda b, i: (b, i, 0)) for y in ys]
    in_specs += [_const_spec(w.shape) for w in ws]
    return pl.pallas_call(
        functools.partial(_outproj_kernel, n_in=len(ys)),
        out_shape=jax.ShapeDtypeStruct(h.shape, F32),
        grid=(bsz, t // tm),
        in_specs=in_specs,
        out_specs=pl.BlockSpec((None, tm, d), lambda b, i: (b, i, 0)),
        compiler_params=_params("parallel", "parallel"),
        name="outproj",
    )(h, mods, *ys, *ws)


def _softmax2_pv(cols, v):
    m = jnp.max(functools.reduce(jnp.maximum, cols), axis=-1, keepdims=True)
    p = [jnp.exp2(c - m) for c in cols]
    den = jnp.sum(functools.reduce(jnp.add, p), axis=-1, keepdims=True)
    return _dot(jnp.concatenate([x.astype(BF16) for x in p], axis=1), v) / den


def _na_kernel(q_ref, k_ref, v_ref, qc_ref, kc_ref, vc_ref, bias_ref, o_ref, oc_ref, *, rows):
    n_pairs = q_ref.shape[1] // LANES
    tc = qc_ref.shape[0]
    win = NA_WIN_H * GRID_W
    n_bias = win // LANES
    low1 = lax.broadcasted_iota(jnp.int32, (1, LANES), 1) < NA_HD
    low16 = jnp.where(low1, 1.0, 0.0).astype(BF16)
    high16 = jnp.where(low1, 0.0, 1.0).astype(BF16)
    low_q = lax.broadcasted_iota(jnp.int32, (GRID_W, LANES), 1) < NA_HD
    low_c = lax.broadcasted_iota(jnp.int32, (tc, LANES), 1) < NA_HD
    pairs = [slice(hp * LANES, (hp + 1) * LANES) for hp in range(n_pairs)]

    def row_body(r, carry):
        r0 = jnp.clip(r - NA_WIN_H // 2, 0, rows - NA_WIN_H)
        doff = r0 - r + NA_WIN_H - 1
        qs = pl.ds(pl.multiple_of(r * GRID_W, GRID_W), GRID_W)
        ks = pl.ds(pl.multiple_of(r0 * GRID_W, GRID_W), win)
        q2 = [q_ref[qs, sl] for sl in pairs]
        qm = [jnp.concatenate([x * low16, x * high16], axis=0) for x in q2]
        s = [_dot_nt(x, jnp.concatenate([k_ref[ks, sl], kc_ref[:, sl]], axis=0)) for x, sl in zip(qm, pairs)]
        cols = [[x[:, j * LANES:(j + 1) * LANES] + bias_ref[hp, doff + 2 * j] for j in range(n_bias)]
                + [x[:, j * LANES:(j + 1) * LANES] for j in range(n_bias, x.shape[1] // LANES)]
                for hp, x in enumerate(s)]
        o = [_softmax2_pv(c, jnp.concatenate([v_ref[ks, sl], vc_ref[:, sl]], axis=0)) for c, sl in zip(cols, pairs)]
        for x, sl in zip(o, pairs):
            o_ref[qs, sl] = jnp.where(low_q, x[:GRID_W], x[GRID_W:]).astype(BF16)
        return carry

    lax.fori_loop(0, rows, row_body, 0)

    for sl in pairs:
        halves = []
        for keep16 in (low16, high16):
            s = _dot_nt(qc_ref[:, sl] * keep16, kc_ref[:, sl])
            halves.append(_softmax2_pv([s[:, j * LANES:(j + 1) * LANES] for j in range(tc // LANES)], vc_ref[:, sl]))
        oc_ref[:, sl] = jnp.where(low_c, halves[0], halves[1]).astype(BF16)


def _na_call(qkv, qkvc, bias):
    bsz, t, w3 = qkv.shape
    tc = qkvc.shape[1]
    w = w3 // 3

    def col(j):
        return lambda b: (b, 0, j)

    return pl.pallas_call(
        functools.partial(_na_kernel, rows=t // GRID_W),
        out_shape=[jax.ShapeDtypeStruct((bsz, t, w), BF16), jax.ShapeDtypeStruct((bsz, tc, w), BF16)],
        grid=(bsz,),
        in_specs=[pl.BlockSpec((None, t, w), col(j)) for j in range(3)]
        + [pl.BlockSpec((None, tc, w), col(j)) for j in range(3)]
        + [pl.BlockSpec(bias.shape, lambda b: (0, 0, 0, 0), pipeline_mode=pl.Buffered(1))],
        out_specs=[pl.BlockSpec((None, t, w), lambda b: (b, 0, 0)), pl.BlockSpec((None, tc, w), lambda b: (b, 0, 0))],
        compiler_params=_params("parallel"),
        name="na_attn",
    )(qkv, qkv, qkv, qkvc, qkvc, qkvc, bias)


def _na_bias_table(rpb):
    c = jnp.arange(GRID_W)[:, None]
    kc = jnp.arange(GRID_W)[None, :]
    cstart = jnp.clip(c - NA_WIN_W // 2, 0, GRID_W - NA_WIN_W)
    ok = (kc >= cstart) & (kc < cstart + NA_WIN_W)
    dc = jnp.clip(kc - c + NA_WIN_W - 1, 0, 2 * NA_WIN_W - 2)
    t = jnp.where(ok[None, None], rpb.astype(F32)[:, :, dc] * LOG2E, NEG_INF)
    nd = 2 * NA_WIN_H - 2
    pair = jnp.concatenate([t[:, :nd], t[:, 1:nd + 1]], axis=-1)
    h = rpb.shape[0]
    pair = pair.reshape(h // 2, 2, nd, GRID_W, 2 * GRID_W)
    return jnp.moveaxis(pair, 1, 2).reshape(h // 2, nd, 2 * GRID_W, 2 * GRID_W)


def _dn_kernel(qr_ref, kr_ref, vr_ref, z_ref, ab_ref, qcr_ref, kcr_ref, vcr_ref, zc_ref, abc_ref,
               cwq_ref, cwk_ref, cwv_ref, alog_ref, dtb_ref, nw_ref,
               y_ref, yc_ref,
               pad_ref, abs_ref, aq_ref, b_ref, gl_ref, o_ref, s_ref, *, hg):
    c = DN_CHUNK
    t = qr_ref.shape[0]
    tc = qcr_ref.shape[0]
    ncc, ncl = tc // c, t // c
    lat0 = 2 * SUBLANES + tc
    half = DN_CONV // 2
    n_sq = int(math.log2(c)) - 1

    zeros8 = jnp.zeros((3, SUBLANES, LANES), F32)
    pad_ref[:, 0:SUBLANES, :] = zeros8
    pad_ref[:, SUBLANES + tc:lat0, :] = zeros8
    pad_ref[:, lat0 + t:lat0 + t + SUBLANES, :] = zeros8
    abs_ref[0:tc, :] = abc_ref[...]
    abs_ref[tc:tc + t, :] = ab_ref[...]
    s_ref[...] = jnp.zeros_like(s_ref)

    lane = lax.broadcasted_iota(jnp.int32, (c, LANES), 1)
    row = lax.broadcasted_iota(jnp.int32, (c, LANES), 0)
    col = jnp.bitwise_and(lane, c - 1)
    fwd = lane < c
    incl = (fwd & (row >= col)) | (~fwd & (row <= col))
    strict = (fwd & (row > col)) | (~fwd & (row < col))
    diag = row == col
    eye2 = jnp.where(diag, 1.0, 0.0).astype(F32)
    fwd16 = jnp.where(fwd, 1.0, 0.0).astype(BF16)
    bwd16 = jnp.where(fwd, 0.0, 1.0).astype(BF16)
    ri = lax.broadcasted_iota(jnp.int32, (2 * c, c), 0)
    ci_ = lax.broadcasted_iota(jnp.int32, (2 * c, c), 1)
    tri2 = jnp.where(((ri < c) & (ci_ <= ri)) | ((ri >= c) & (ci_ >= ri - c)), 1.0, 0.0).astype(BF16)
    tri6 = jnp.concatenate([tri2, tri2, tri2], axis=1)
    neg_a = -jnp.exp(alog_ref[...])
    dtb = dtb_ref[...]

    def l2n(x):
        return x * lax.rsqrt(jnp.sum(x * x, axis=-1, keepdims=True) + NORM_EPS)

    def bd(m16):
        return jnp.concatenate([m16 * fwd16, m16 * bwd16], axis=0)

    def mm3(lhs_parts, bh, bl):
        bdh = bd(bh)
        rhs = jnp.concatenate([bdh, bdh, bd(bl)], axis=0)
        lhs = jnp.concatenate([jnp.concatenate([ah, al, ah], axis=1) for ah, al in lhs_parts], axis=0)
        return _dot(lhs, rhs)

    def prep_head(hh):
        hid = pl.program_id(1) * hg + hh
        hs = slice(hh * LANES, (hh + 1) * LANES)
        for s, (cr, lr) in enumerate(((qcr_ref, qr_ref), (kcr_ref, kr_ref), (vcr_ref, vr_ref))):
            pad_ref[s, SUBLANES:SUBLANES + tc, :] = cr[:, hs]
            pad_ref[s, lat0:lat0 + t, :] = lr[:, hs]

        def conv(s, cw_ref, base):
            win = pad_ref[s, pl.ds(pl.multiple_of(base - SUBLANES, SUBLANES), c + 2 * SUBLANES), :]
            acc = win[SUBLANES - half:SUBLANES - half + c] * cw_ref[0:1, hs]
            for j in range(1, DN_CONV):
                acc += win[SUBLANES - half + j:SUBLANES - half + j + c] * cw_ref[j:j + 1, hs]
            return _silu(acc)

        def pick(x, idx):
            return jnp.broadcast_to(jnp.sum(jnp.where(lane == idx, x, 0.0), axis=-1, keepdims=True), (c, LANES))

        def stage_inputs(ch):
            base = ch * c + jnp.where(ch < ncc, SUBLANES, 2 * SUBLANES)
            q = l2n(conv(0, cwq_ref, base))
            k = l2n(conv(1, cwk_ref, base))
            v = conv(2, cwv_ref, base)
            abt = abs_ref[pl.ds(pl.multiple_of(ch * c, c), c), :]
            g_all = neg_a * _softplus(abt + dtb)
            b_all = _sigmoid(abt)
            return q * DN_DK ** -0.5, k, v, g_all, b_all

        def stage_decay(g_all):
            parts = []
            for gp in _split3(g_all):
                gp = gp.astype(F32)
                pf = pick(gp, hid)
                pb = pick(gp, DN_HEADS + hid)
                parts.append(jnp.concatenate([jnp.where(strict, jnp.where(fwd, pf, pb), 0.0), pf, pb], axis=1).astype(BF16))
            res = _dot(tri6, jnp.concatenate(parts, axis=0))
            return jnp.where(fwd, res[:c, 0:LANES], res[c:, 0:LANES]), res[:c, LANES:2 * LANES], res[c:, 2 * LANES:]

        def stage_finish(ch, qs, k, v, qk16, p, gc_f, gc_b, bt_f, bt_b):
            e_f = jnp.exp(gc_f)
            e_b = jnp.exp(gc_b)
            rhs = jnp.concatenate([
                jnp.concatenate([v * bt_f, k * (bt_f * e_f)], axis=1),
                jnp.concatenate([v * bt_b, k * (bt_b * e_b)], axis=1)], axis=0).astype(BF16)
            sol16 = _dot(bd(p.astype(BF16)), rhs).astype(BF16)
            z16 = jnp.zeros((c, 2 * DN_DV), BF16)
            both = jnp.concatenate([jnp.concatenate([sol16[:c], z16], axis=1),
                                    jnp.concatenate([z16, sol16[c:]], axis=1)], axis=0)
            gl_f = gc_f[c - 1:c, :]
            gl_b = gc_b[0:1, :]
            kd = jnp.concatenate([k * jnp.exp(gl_f - gc_f), k * jnp.exp(gl_b - gc_b)], axis=0).astype(BF16)
            qsol = _dot(qk16, both)
            ksol = _dot_tn(kd, both)
            rows = pl.ds(pl.multiple_of(ch * c, c), c)
            for d, (e, gl) in enumerate(((e_f, gl_f), (e_b, gl_b))):
                lo = 2 * DN_DV * d
                aq_ref[hh, d, ch, 0:DN_DK, :] = ksol[:, lo + DN_DV:lo + 2 * DN_DV].astype(BF16)
                aq_ref[hh, d, ch, DN_DK:DN_DK + c, :] = (qs * e - qsol[:, lo + DN_DV:lo + 2 * DN_DV]).astype(BF16)
                b_ref[hh, d, ch] = ksol[:, lo:lo + DN_DV]
                o_ref[hh, d, rows, :] = qsol[:, lo:lo + DN_DV]
                gl_ref[hh, d, ch] = jnp.broadcast_to(jnp.exp(gl), (SUBLANES, LANES))

        def body(i, carry):
            chs = [i * DN_PREP_CHUNKS + j for j in range(DN_PREP_CHUNKS)]
            ins = [stage_inputs(ch) for ch in chs]
            dec = [stage_decay(x[3]) for x in ins]
            bts = [(pick(x[4], 2 * DN_HEADS + hid), pick(x[4], 3 * DN_HEADS + hid)) for x in ins]
            decay = [jnp.where(incl, jnp.exp(jnp.where(incl, d[0], 0.0)), 0.0) for d in dec]
            k16 = [x[1].astype(BF16) for x in ins]
            kk = [jnp.concatenate([a, a], axis=0) for a in k16]
            qk16 = [(_dot_nt(x[0].astype(BF16), b) * dc).astype(BF16) for x, b, dc in zip(ins, kk, decay)]
            lmat = [jnp.where(strict, jnp.where(fwd, bt[0], bt[1]) * _dot_nt(a, b) * dc, 0.0)
                    for a, b, bt, dc in zip(k16, kk, bts, decay)]
            lsp = [_split2(x) for x in lmat]
            m = [mm3([s], s[0], s[1]) for s in lsp]
            p = [eye2 - x for x in lmat]
            for s in range(n_sq):
                msp = [_split2(x) for x in m]
                psp = [_split2(x) for x in p]
                if s < n_sq - 1:
                    res = [mm3([a, b], b[0], b[1]) for a, b in zip(psp, msp)]
                    p = [a + r[:c] for a, r in zip(p, res)]
                    m = [r[c:] for r in res]
                else:
                    p = [a + mm3([b], e[0], e[1]) for a, b, e in zip(p, psp, msp)]
            for j, ch in enumerate(chs):
                stage_finish(ch, ins[j][0], ins[j][1], ins[j][2], qk16[j], p[j], dec[j][1], dec[j][2], *bts[j])
            return carry

        lax.fori_loop(0, (ncc + ncl) // DN_PREP_CHUNKS, body, 0)

    for hh in range(hg):
        prep_head(hh)

    def scan(first, n):
        def body(i, carry):
            for hh in range(hg):
                for d in range(2):
                    ch = first + i if d == 0 else first + n - 1 - i
                    s_mat = s_ref[hh, d]
                    res = _dot(aq_ref[hh, d, ch], s_mat.astype(BF16))
                    s_ref[hh, d] = s_mat * gl_ref[hh, d, ch][0:1, :] + (b_ref[hh, d, ch] - res[:DN_DK])
                    rows = pl.ds(pl.multiple_of(ch * c, c), c)
                    o_ref[hh, d, rows, :] = o_ref[hh, d, rows, :] + res[DN_DK:]
            return carry
        return body

    lax.fori_loop(0, ncc, scan(0, ncc), 0)
    lax.fori_loop(0, ncl, scan(ncc, ncl), 0)

    def gated_norm(o, z):
        return (o * lax.rsqrt(jnp.mean(o * o, axis=-1, keepdims=True) + NORM_EPS) * nw_ref[...] * _silu(z))

    piece = 4 * c
    for hh in range(hg):
        hs = slice(hh * LANES, (hh + 1) * LANES)
        yc_ref[:, hs] = gated_norm(o_ref[hh, 0, 0:tc, :] + o_ref[hh, 1, 0:tc, :], zc_ref[:, hs]).astype(BF16)
        for j in range(t // piece):
            rows = slice(j * piece, (j + 1) * piece)
            orow = slice(tc + j * piece, tc + (j + 1) * piece)
            y_ref[rows, hs] = gated_norm(o_ref[hh, 0, orow, :] + o_ref[hh, 1, orow, :], z_ref[rows, hs]).astype(BF16)


def _dn_call(dn, z, ab, dnc, zc, abc, conv_w, alog, dtb, norm_w):
    bsz, t, _ = dn.shape
    tc = dnc.shape[1]
    h = DN_HEADS
    hg = DN_HEAD_GROUP
    nch = (t + tc) // DN_CHUNK
    c = DN_CHUNK
    wide = hg * LANES

    def slab(rows, j0):
        return pl.BlockSpec((None, rows, wide), lambda b, g: (b, 0, j0 // hg + g))

    def whole(rows):
        return pl.BlockSpec((None, rows, LANES), lambda b, g: (b, 0, 0))

    def cw(j0):
        return pl.BlockSpec((DN_CONV, wide), lambda b, g: (0, j0 // hg + g))

    vec = pl.BlockSpec((1, LANES), lambda b, g: (0, 0))
    return pl.pallas_call(
        functools.partial(_dn_kernel, hg=hg),
        out_shape=[jax.ShapeDtypeStruct((bsz, t, h * DN_DV), BF16), jax.ShapeDtypeStruct((bsz, tc, h * DN_DV), BF16)],
        grid=(bsz, h // hg),
        in_specs=[slab(t, 0), slab(t, h), slab(t, 2 * h), slab(t, 0), whole(t),
                  slab(tc, 0), slab(tc, h), slab(tc, 2 * h), slab(tc, 0), whole(tc),
                  cw(0), cw(h), cw(2 * h), vec, vec, vec],
        out_specs=[slab(t, 0), slab(tc, 0)],
        scratch_shapes=[
            pltpu.VMEM((3, 3 * SUBLANES + tc + t, LANES), F32),
            pltpu.VMEM((tc + t, LANES), F32),
            pltpu.VMEM((hg, 2, nch, DN_DK + c, DN_DV), BF16),
            pltpu.VMEM((hg, 2, nch, DN_DK, DN_DV), F32),
            pltpu.VMEM((hg, 2, nch, SUBLANES, LANES), F32),
            pltpu.VMEM((hg, 2, tc + t, DN_DV), F32),
            pltpu.VMEM((hg, 2, DN_DK, DN_DV), F32),
        ],
        compiler_params=_params("parallel", "parallel"),
        name="deltanet",
    )(dn, dn, dn, z, ab, dnc, dnc, dnc, zc, abc, conv_w, conv_w, conv_w, alog, dtb, norm_w)


def _swa_kernel(sink_ref, q_ref, k_ref, v_ref, kc_ref, vc_ref, o_ref, *, n_blocks):
    blk = SWA_BLOCK
    n_kv_pairs = k_ref.shape[1] // LANES
    slabs = q_ref.shape[1] // LANES // n_kv_pairs
    rows = slabs * blk
    low = lax.broadcasted_iota(jnp.int32, (rows, LANES), 1) < SWA_HD
    low_blk = lax.broadcasted_iota(jnp.int32, (blk, LANES), 1) < SWA_HD
    low1 = lax.broadcasted_iota(jnp.int32, (1, LANES), 1) < SWA_HD
    low16 = jnp.where(low1, 1.0, 0.0).astype(BF16)
    high16 = jnp.where(low1, 0.0, 1.0).astype(BF16)
    qi = lax.broadcasted_iota(jnp.int32, (rows, blk), 0) % blk
    kj = lax.broadcasted_iota(jnp.int32, (rows, blk), 1)

    def block_body(n, carry):
        lo = jnp.maximum(n - 1, 0)
        hi = jnp.minimum(n + 1, n_blocks - 1)
        ok_lo = (kj >= qi) & (n > 0)
        ok_hi = (kj <= qi) & (n < n_blocks - 1)

        def tok(i):
            return pl.ds(pl.multiple_of(i * blk, blk), blk)

        for p in range(n_kv_pairs):
            kvl = slice(p * LANES, (p + 1) * LANES)
            k_all = jnp.concatenate([k_ref[tok(lo), kvl], k_ref[tok(n), kvl], k_ref[tok(hi), kvl], kc_ref[:, kvl]], axis=0)
            v_all = jnp.concatenate([v_ref[tok(lo), kvl], v_ref[tok(n), kvl], v_ref[tok(hi), kvl], vc_ref[:, kvl]], axis=0)
            qs = jnp.concatenate(
                [q_ref[tok(n), (p * slabs + j) * LANES:(p * slabs + j + 1) * LANES] for j in range(slabs)], axis=0)
            res = []
            for half in range(2):
                keep16, drop16 = (high16, low16) if half else (low16, high16)
                s = _dot_nt(qs * keep16, k_all)
                cols = [s[:, j * blk:(j + 1) * blk] for j in range(s.shape[1] // blk)]
                cols[0] = jnp.where(ok_lo, cols[0], NEG_INF)
                cols[2] = jnp.where(ok_hi, cols[2], NEG_INF)
                probs, e_sink = [], []
                for j in range(slabs):
                    cj = [c[j * blk:(j + 1) * blk] for c in cols]
                    sink = sink_ref[(2 * p + half) * slabs + j] * LOG2E
                    m = jnp.maximum(jnp.max(functools.reduce(jnp.maximum, cj), axis=-1, keepdims=True), sink)
                    probs.append(jnp.concatenate([jnp.exp2(c - m).astype(BF16) for c in cj], axis=1))
                    e_sink.append(jnp.exp2(sink - m))
                r = _dot(jnp.concatenate(probs, axis=0), v_all * keep16 + drop16)
                den_lanes = low_blk if half else ~low_blk
                res.append(jnp.concatenate(
                    [r[j * blk:(j + 1) * blk] + jnp.where(den_lanes, e_sink[j], 0.0) for j in range(slabs)], axis=0))
            num = jnp.where(low, res[0], res[1])
            den = pltpu.roll(jnp.where(low, res[1], res[0]), SWA_HD, 1)
            o = (num / den).astype(BF16)
            for j in range(slabs):
                o_ref[tok(n), (p * slabs + j) * LANES:(p * slabs + j + 1) * LANES] = o[j * blk:(j + 1) * blk]
        return carry

    lax.fori_loop(0, n_blocks, block_body, 0)


def _swa_call(sink, q, k, v, kc, vc):
    bsz, t, wq = q.shape
    wk = k.shape[2]
    tc = kc.shape[1]
    return pl.pallas_call(
        functools.partial(_swa_kernel, n_blocks=t // SWA_BLOCK),
        out_shape=jax.ShapeDtypeStruct((bsz, t, wq), BF16),
        grid=(bsz,),
        in_specs=[
            pl.BlockSpec(memory_space=pltpu.SMEM),
            pl.BlockSpec((None, t, wq), lambda b: (b, 0, 0)),
            pl.BlockSpec((None, t, wk), lambda b: (b, 0, 0)),
            pl.BlockSpec((None, t, wk), lambda b: (b, 0, 0)),
            pl.BlockSpec((None, tc, wk), lambda b: (b, 0, 0)),
            pl.BlockSpec((None, tc, wk), lambda b: (b, 0, 0)),
        ],
        out_specs=pl.BlockSpec((None, t, wq), lambda b: (b, 0, 0)),
        compiler_params=_params("parallel"),
        name="swa_attn",
    )(sink, q, k, v, kc, vc)


def _swa_head_order():
    rep = SWA_HEADS // SWA_KV_HEADS
    order = []
    for p in range(SWA_KV_HEADS // 2):
        for j in range(rep):
            order += [2 * p * rep + j, (2 * p + 1) * rep + j]
    return order


def _rope_tables(t):
    pos = jnp.arange(t)
    half = SWA_HD // 2
    inv = jnp.power(ROPE_BASE, -jnp.arange(0, half, 2, dtype=F32) / half)
    ang_r = (pos // GRID_W).astype(F32)[:, None] * inv
    ang_c = (pos % GRID_W).astype(F32)[:, None] * inv
    cos = jnp.concatenate([jnp.cos(ang_r)] * 2 + [jnp.cos(ang_c)] * 2, axis=-1)
    sin = jnp.concatenate([-jnp.sin(ang_r), jnp.sin(ang_r), -jnp.sin(ang_c), jnp.sin(ang_c)], axis=-1)
    return jnp.tile(cos, (1, LANES // SWA_HD)), jnp.tile(sin, (1, LANES // SWA_HD))


def _ffn_weights(w_up, w_down):
    d, f2 = w_up.shape
    f = f2 // 2
    nc = f // FF_CHUNK
    wg = w_up[:, :f].reshape(d, nc, FF_CHUNK).transpose(1, 0, 2).astype(BF16)
    wu = w_up[:, f:].reshape(d, nc, FF_CHUNK).transpose(1, 0, 2).astype(BF16)
    wd = w_down.reshape(nc, FF_CHUNK, d).astype(BF16)
    return wg, wu, wd


def kernel(x, c, ctx, c_ctx, ada_w, ada_b, norm_g, ffn_w_up, ffn_w_down, even_w_in, even_w_out, na_rpb,
           dn_conv_w, dn_a_log, dn_dt_bias, dn_norm_w, odd_w_in, odd_w_out, swa_sink, final_norm_g):
    bsz, t, d = x.shape
    tc = ctx.shape[1]
    depth = ada_w.shape[0]
    ctx_row = bsz

    c16 = jnp.concatenate([c, c_ctx[None, :], jnp.zeros((MOD_ROWS - bsz - 1, d), F32)], axis=0)
    mods = _adaln_call(c16, ada_w, ada_b).reshape(depth, N_MOD, MOD_ROWS, 1, d)
    norm_g3 = norm_g.reshape(depth * 3, 1, d)

    h = x
    hc = ctx.reshape(1, bsz * tc, d)
    for i in range(depth):
        need_ctx = i < depth - 1
        j = i // 2
        w1 = _ffn_weights(ffn_w_up[i, 0], ffn_w_down[i, 0])
        w2 = _ffn_weights(ffn_w_up[i, 1], ffn_w_down[i, 1])
        h = _ffn_call(h, mods, norm_g3, i, 0, None, *w1)
        hc = _ffn_call(hc, mods, norm_g3, i, 0, ctx_row, *w1)
        if i % 2 == 0:
            w_in = even_w_in[j]
            z0 = NA_PROJ + DN_QKV
            z1 = z0 + DN_HEADS * DN_DV
            wab = jnp.pad(w_in[:, z1:], ((0, 0), (0, LANES - (w_in.shape[1] - z1))))
            w_parts = [w.astype(BF16) for w in (w_in[:, :NA_PROJ], w_in[:, NA_PROJ:z0], w_in[:, z0:z1], wab)]
            na, dn, z, ab = _proj_even_call(h, mods, norm_g3, i, None, *w_parts)
            nac, dnc, zc, abc = [a.reshape(bsz, tc, a.shape[-1])
                                 for a in _proj_even_call(hc, mods, norm_g3, i, ctx_row, *w_parts)]
            y_na, y_na_c = _na_call(na, nac, _na_bias_table(na_rpb[j]))
            lane_pad = (0, LANES - 2 * DN_HEADS)
            alog = jnp.pad(dn_a_log[j].reshape(-1), lane_pad)[None, :]
            dtb = jnp.pad(dn_dt_bias[j].reshape(-1), lane_pad)[None, :]
            y_dn, y_dn_c = _dn_call(dn, z, ab, dnc, zc, abc, dn_conv_w[j], alog, dtb, dn_norm_w[j][None, :])
            w_out = even_w_out[j].astype(BF16)
            ws = [w_out[:NA_HEADS * NA_HD], w_out[NA_HEADS * NA_HD:]]
            h = _outproj_call(h, mods, i, None, [y_na, y_dn], ws)
            if need_ctx:
                ycs = [y.reshape(1, bsz * tc, y.shape[-1]) for y in (y_na_c, y_dn_c)]
                hc = _outproj_call(hc, mods, i, ctx_row, ycs, ws)
        else:
            w_in = odd_w_in[j]
            qw = SWA_HEADS * SWA_HD
            kw = SWA_KV_HEADS * SWA_HD
            cols = jnp.asarray([hd * SWA_HD + e for hd in _swa_head_order() for e in range(SWA_HD)], jnp.int32)
            wq = w_in[:, :qw][:, cols].astype(BF16)
            wk = w_in[:, qw:qw + kw].astype(BF16)
            wv = w_in[:, qw + kw:].astype(BF16)
            cos, sin = _rope_tables(t)
            q, k, v = _proj_odd_call(h, mods, norm_g3, i, None, wq, wk, wv, cos, sin)
            kc, vc = [a.reshape(bsz, tc, kw) for a in _proj_odd_call(hc, mods, norm_g3, i, ctx_row, None, wk, wv, None, None)]
            y = _swa_call(swa_sink[j], q, k, v, kc, vc)
            h = _outproj_call(h, mods, i, None, [y], [odd_w_out[j][cols].astype(BF16)])
            if need_ctx:
                raise NotImplementedError("context queries of a windowed layer are only needed before a later layer")
        last = i == depth - 1
        h = _ffn_call(h, mods, norm_g3, i, 1, None, *w2, final_g=final_norm_g[None, :] if last else None)
        if need_ctx:
            hc = _ffn_call(hc, mods, norm_g3, i, 1, ctx_row, *w2)
    return h
```

```python
import functools
import math

import jax
import jax.numpy as jnp
from jax import lax
from jax.experimental import pallas as pl
from jax.experimental.pallas import tpu as pltpu

F32 = jnp.float32
BF16 = jnp.bfloat16

GRID_W = 64
N_MOD = 9
NORM_EPS = 1e-6
NEG_INF = -1e30
LOG2E = math.log2(math.e)
ROPE_BASE = 10000.0
NA_HEADS = 8
NA_HD = 64
NA_WIN_H = 8
NA_WIN_W = 16
NA_PROJ = 3 * NA_HEADS * NA_HD
DN_HEADS = 4
DN_DK = 128
DN_DV = 128
DN_CONV = 5
DN_CHUNK = 64
DN_QKV = DN_HEADS * (2 * DN_DK + DN_DV)
SWA_HEADS = 16
SWA_KV_HEADS = 4
SWA_HD = 64
SWA_BLOCK = 128

LANES = 128
SUBLANES = 8
VMEM_LIMIT = 56 * 1024 * 1024

TOK_TILE = 1024
FF_CHUNK = 256
MOD_ROWS = 16
DN_HEAD_GROUP = 2
DN_PREP_CHUNKS = 9


def _params(*sem):
    return pltpu.CompilerParams(dimension_semantics=sem, vmem_limit_bytes=VMEM_LIMIT)


def _dot(a, b):
    return jnp.dot(a, b, preferred_element_type=F32)


def _dot_nt(a, b):
    return lax.dot_general(a, b, (((1,), (1,)), ((), ())), preferred_element_type=F32)


def _dot_tn(a, b):
    return lax.dot_general(a, b, (((0,), (0,)), ((), ())), preferred_element_type=F32)


def _sigmoid(x):
    return 0.5 + 0.5 * jnp.tanh(0.5 * x)


def _silu(x):
    h = 0.5 * x
    return h + h * jnp.tanh(h)


def _softplus(x):
    return jnp.maximum(x, 0.0) + jnp.log1p(jnp.exp(-jnp.abs(x)))


def _modulate(x, g, shift, scale):
    y = x * lax.rsqrt(jnp.mean(x * x, axis=-1, keepdims=True) + NORM_EPS) * g
    return y * (1.0 + scale) + shift


def _split3(x):
    h1 = x.astype(BF16)
    r1 = x - h1.astype(F32)
    h2 = r1.astype(BF16)
    h3 = (r1 - h2.astype(F32)).astype(BF16)
    return h1, h2, h3


def _split2(x):
    hi = x.astype(BF16)
    return hi, (x - hi.astype(F32)).astype(BF16)


def _adaln_kernel(c_ref, w_ref, b_ref, o_ref):
    s = _silu(c_ref[...]).astype(BF16)
    o_ref[...] = _dot(s, w_ref[...].astype(BF16)) + b_ref[...]


def _adaln_call(c16, ada_w, ada_b):
    depth, d, _ = ada_w.shape
    b4 = ada_b.reshape(depth * N_MOD, 1, d)
    return pl.pallas_call(
        _adaln_kernel,
        out_shape=jax.ShapeDtypeStruct((depth, N_MOD, MOD_ROWS, d), F32),
        grid=(depth, N_MOD),
        in_specs=[
            pl.BlockSpec((MOD_ROWS, d), lambda i, k: (0, 0)),
            pl.BlockSpec((None, d, d), lambda i, k: (i, 0, k)),
            pl.BlockSpec((None, 1, d), lambda i, k: (i * N_MOD + k, 0, 0)),
        ],
        out_specs=pl.BlockSpec((None, None, MOD_ROWS, d), lambda i, k: (i, k, 0, 0)),
        compiler_params=_params("arbitrary", "arbitrary"),
        name="adaln",
    )(c16, ada_w, b4)


def _mod_spec(layer, k, ctx_row, d):
    if ctx_row is None:
        return pl.BlockSpec((None, None, None, 1, d), lambda b, t: (layer, k, b, 0, 0))
    return pl.BlockSpec((None, None, None, 1, d), lambda b, t: (layer, k, ctx_row, 0, 0))


def _const_spec(shape):
    nd = len(shape)
    return pl.BlockSpec(shape, lambda b, t: (0,) * nd, pipeline_mode=pl.Buffered(1))


def _ffn_kernel(x_ref, g_ref, sh_ref, sc_ref, gt_ref, wg_ref, wu_ref, wd_ref, *rest, n_chunks, final):
    if final:
        fg_ref, o_ref, u_ref, acc_ref = rest
    else:
        o_ref, u_ref, acc_ref = rest
    u_ref[...] = _modulate(x_ref[...], g_ref[...], sh_ref[...], sc_ref[...]).astype(BF16)
    acc_ref[...] = jnp.zeros_like(acc_ref)

    def body(c, carry):
        u = u_ref[...]
        gate = _dot(u, wg_ref[c])
        up = _dot(u, wu_ref[c])
        a = (_silu(gate) * up).astype(BF16)
        acc_ref[...] += _dot(a, wd_ref[c])
        return carry

    lax.fori_loop(0, n_chunks, body, 0)
    y = x_ref[...] + 0.5 * gt_ref[...] * acc_ref[...]
    if final:
        y = y * lax.rsqrt(jnp.mean(y * y, axis=-1, keepdims=True) + NORM_EPS) * fg_ref[...]
    o_ref[...] = y


def _ffn_call(h, mods, norm_g3, layer, which, ctx_row, wg, wu, wd, final_g=None):
    bsz, t, d = h.shape
    tm = min(TOK_TILE, t)
    n_chunks = wg.shape[0]
    k0 = 6 if which else 0
    in_specs = [
        pl.BlockSpec((None, tm, d), lambda b, i: (b, i, 0)),
        pl.BlockSpec((None, 1, d), lambda b, i: (layer * 3 + (2 if which else 0), 0, 0)),
        _mod_spec(layer, k0, ctx_row, d),
        _mod_spec(layer, k0 + 1, ctx_row, d),
        _mod_spec(layer, k0 + 2, ctx_row, d),
        _const_spec(wg.shape),
        _const_spec(wu.shape),
        _const_spec(wd.shape),
    ]
    args = [h, norm_g3, mods, mods, mods, wg, wu, wd]
    if final_g is not None:
        in_specs.append(pl.BlockSpec((1, d), lambda b, i: (0, 0)))
        args.append(final_g)
    return pl.pallas_call(
        functools.partial(_ffn_kernel, n_chunks=n_chunks, final=final_g is not None),
        out_shape=jax.ShapeDtypeStruct(h.shape, F32),
        grid=(bsz, t // tm),
        in_specs=in_specs,
        out_specs=pl.BlockSpec((None, tm, d), lambda b, i: (b, i, 0)),
        scratch_shapes=[pltpu.VMEM((tm, d), BF16), pltpu.VMEM((tm, d), F32)],
        compiler_params=_params("parallel", "parallel"),
        name="ffn",
    )(*args)


def _proj_even_kernel(x_ref, g_ref, sh_ref, sc_ref, wna_ref, wdn_ref, wz_ref, wab_ref,
                      na_ref, dn_ref, z_ref, ab_ref):
    u = _modulate(x_ref[...], g_ref[...], sh_ref[...], sc_ref[...]).astype(BF16)
    na = _dot(u, wna_ref[...])
    qw = NA_HEADS * NA_HD
    na_ref[:, :qw] = (na[:, :qw] * (NA_HD ** -0.5 * LOG2E)).astype(BF16)
    na_ref[:, qw:] = na[:, qw:].astype(BF16)
    dn_ref[...] = _dot(u, wdn_ref[...])
    z_ref[...] = _dot(u, wz_ref[...])
    ab_ref[...] = _dot(u, wab_ref[...])


def _proj_even_call(h, mods, norm_g3, layer, ctx_row, wna, wdn, wz, wab):
    bsz, t, d = h.shape
    tm = min(TOK_TILE, t)
    widths = (wna.shape[1], wdn.shape[1], wz.shape[1], wab.shape[1])
    dtypes = (BF16, F32, F32, F32)
    return pl.pallas_call(
        _proj_even_kernel,
        out_shape=[jax.ShapeDtypeStruct((bsz, t, w), dt) for w, dt in zip(widths, dtypes)],
        grid=(bsz, t // tm),
        in_specs=[
            pl.BlockSpec((None, tm, d), lambda b, i: (b, i, 0)),
            pl.BlockSpec((None, 1, d), lambda b, i: (layer * 3 + 1, 0, 0)),
            _mod_spec(layer, 3, ctx_row, d),
            _mod_spec(layer, 4, ctx_row, d),
            _const_spec(wna.shape), _const_spec(wdn.shape), _const_spec(wz.shape), _const_spec(wab.shape),
        ],
        out_specs=[pl.BlockSpec((None, tm, w), lambda b, i: (b, i, 0)) for w in widths],
        compiler_params=_params("parallel", "parallel"),
        name="proj_even",
    )(h, norm_g3, mods, mods, wna, wdn, wz, wab)


def _rope_slab(x, cos, sin, first):
    swapped = jnp.where(first, pltpu.roll(x, LANES - 16, 1), pltpu.roll(x, 16, 1))
    return x * cos + swapped * sin


def _proj_odd_kernel(x_ref, g_ref, sh_ref, sc_ref, *rest, with_q):
    if with_q:
        wq_ref, wk_ref, wv_ref, cos_ref, sin_ref, q_ref, k_ref, v_ref = rest
    else:
        wk_ref, wv_ref, k_ref, v_ref = rest
    u = _modulate(x_ref[...], g_ref[...], sh_ref[...], sc_ref[...]).astype(BF16)
    v_ref[...] = _dot(u, wv_ref[...]).astype(BF16)
    k = _dot(u, wk_ref[...])
    if not with_q:
        k_ref[...] = k.astype(BF16)
        return
    cos = cos_ref[...]
    sin = sin_ref[...]
    lane = lax.broadcasted_iota(jnp.int32, cos.shape, 1)
    first = (lane % 32) < 16
    for j in range(k.shape[1] // LANES):
        sl = slice(j * LANES, (j + 1) * LANES)
        k_ref[:, sl] = _rope_slab(k[:, sl], cos, sin, first).astype(BF16)
    q = _dot(u, wq_ref[...])
    for j in range(q.shape[1] // LANES):
        sl = slice(j * LANES, (j + 1) * LANES)
        q_ref[:, sl] = (_rope_slab(q[:, sl], cos, sin, first) * (SWA_HD ** -0.5 * LOG2E)).astype(BF16)


def _proj_odd_call(h, mods, norm_g3, layer, ctx_row, wq, wk, wv, cos, sin):
    bsz, t, d = h.shape
    tm = min(TOK_TILE, t)
    with_q = wq is not None
    in_specs = [
        pl.BlockSpec((None, tm, d), lambda b, i: (b, i, 0)),
        pl.BlockSpec((None, 1, d), lambda b, i: (layer * 3 + 1, 0, 0)),
        _mod_spec(layer, 3, ctx_row, d),
        _mod_spec(layer, 4, ctx_row, d),
    ]
    args = [h, norm_g3, mods, mods]
    widths = []
    if with_q:
        in_specs.append(_const_spec(wq.shape))
        args.append(wq)
        widths.append(wq.shape[1])
    in_specs += [_const_spec(wk.shape), _const_spec(wv.shape)]
    args += [wk, wv]
    widths += [wk.shape[1], wv.shape[1]]
    if with_q:
        in_specs += [pl.BlockSpec((tm, LANES), lambda b, i: (i, 0))] * 2
        args += [cos, sin]
    return pl.pallas_call(
        functools.partial(_proj_odd_kernel, with_q=with_q),
        out_shape=[jax.ShapeDtypeStruct((bsz, t, w), BF16) for w in widths],
        grid=(bsz, t // tm),
        in_specs=in_specs,
        out_specs=[pl.BlockSpec((None, tm, w), lambda b, i: (b, i, 0)) for w in widths],
        compiler_params=_params("parallel", "parallel"),
        name="proj_odd",
    )(*args)


def _outproj_kernel(h_ref, gt_ref, *rest, n_in):
    y_refs, w_refs, o_ref = rest[:n_in], rest[n_in:2 * n_in], rest[2 * n_in]
    acc = _dot(y_refs[0][...], w_refs[0][...])
    for y_ref, w_ref in zip(y_refs[1:], w_refs[1:]):
        acc += _dot(y_ref[...], w_ref[...])
    o_ref[...] = h_ref[...] + gt_ref[...] * acc


def _outproj_call(h, mods, layer, ctx_row, ys, ws):
    bsz, t, d = h.shape
    tm = min(TOK_TILE, t)
    in_specs = [pl.BlockSpec((None, tm, d), lambda b, i: (b, i, 0)), _mod_spec(layer, 5, ctx_row, d)]
    in_specs += [pl.BlockSpec((None, tm, y.shape[2]), lambda b, i: (b, i, 0)) for y in ys]
    in_specs += [_const_spec(w.shape) for w in ws]
    return pl.pallas_call(
        functools.partial(_outproj_kernel, n_in=len(ys)),
        out_shape=jax.ShapeDtypeStruct(h.shape, F32),
        grid=(bsz, t // tm),
        in_specs=in_specs,
        out_specs=pl.BlockSpec((None, tm, d), lambda b, i: (b, i, 0)),
        compiler_params=_params("parallel", "parallel"),
        name="outproj",
    )(h, mods, *ys, *ws)


def _softmax2_pv(cols, v):
    m = jnp.max(functools.reduce(jnp.maximum, cols), axis=-1, keepdims=True)
    p = [jnp.exp2(c - m) for c in cols]
    den = jnp.sum(functools.reduce(jnp.add, p), axis=-1, keepdims=True)
    return _dot(jnp.concatenate([x.astype(BF16) for x in p], axis=1), v) / den


def _na_kernel(q_ref, k_ref, v_ref, qc_ref, kc_ref, vc_ref, bias_ref, o_ref, oc_ref, *, rows):
    n_pairs = q_ref.shape[1] // LANES
    tc = qc_ref.shape[0]
    win = NA_WIN_H * GRID_W
    n_bias = win // LANES
    low1 = lax.broadcasted_iota(jnp.int32, (1, LANES), 1) < NA_HD
    low16 = jnp.where(low1, 1.0, 0.0).astype(BF16)
    high16 = jnp.where(low1, 0.0, 1.0).astype(BF16)
    low_q = lax.broadcasted_iota(jnp.int32, (GRID_W, LANES), 1) < NA_HD
    low_c = lax.broadcasted_iota(jnp.int32, (tc, LANES), 1) < NA_HD
    pairs = [slice(hp * LANES, (hp + 1) * LANES) for hp in range(n_pairs)]

    def row_body(r, carry):
        r0 = jnp.clip(r - NA_WIN_H // 2, 0, rows - NA_WIN_H)
        doff = r0 - r + NA_WIN_H - 1
        qs = pl.ds(pl.multiple_of(r * GRID_W, GRID_W), GRID_W)
        ks = pl.ds(pl.multiple_of(r0 * GRID_W, GRID_W), win)
        q2 = [q_ref[qs, sl] for sl in pairs]
        qm = [jnp.concatenate([x * low16, x * high16], axis=0) for x in q2]
        s = [_dot_nt(x, jnp.concatenate([k_ref[ks, sl], kc_ref[:, sl]], axis=0)) for x, sl in zip(qm, pairs)]
        cols = [[x[:, j * LANES:(j + 1) * LANES] + bias_ref[hp, doff + 2 * j] for j in range(n_bias)]
                + [x[:, j * LANES:(j + 1) * LANES] for j in range(n_bias, x.shape[1] // LANES)]
                for hp, x in enumerate(s)]
        o = [_softmax2_pv(c, jnp.concatenate([v_ref[ks, sl], vc_ref[:, sl]], axis=0)) for c, sl in zip(cols, pairs)]
        for x, sl in zip(o, pairs):
            o_ref[qs, sl] = jnp.where(low_q, x[:GRID_W], x[GRID_W:]).astype(BF16)
        return carry

    lax.fori_loop(0, rows, row_body, 0)

    for sl in pairs:
        halves = []
        for keep16 in (low16, high16):
            s = _dot_nt(qc_ref[:, sl] * keep16, kc_ref[:, sl])
            halves.append(_softmax2_pv([s[:, j * LANES:(j + 1) * LANES] for j in range(tc // LANES)], vc_ref[:, sl]))
        oc_ref[:, sl] = jnp.where(low_c, halves[0], halves[1]).astype(BF16)


def _na_call(qkv, qkvc, bias):
    bsz, t, w3 = qkv.shape
    tc = qkvc.shape[1]
    w = w3 // 3

    def col(j):
        return lambda b: (b, 0, j)

    return pl.pallas_call(
        functools.partial(_na_kernel, rows=t // GRID_W),
        out_shape=[jax.ShapeDtypeStruct((bsz, t, w), BF16), jax.ShapeDtypeStruct((bsz, tc, w), BF16)],
        grid=(bsz,),
        in_specs=[pl.BlockSpec((None, t, w), col(j)) for j in range(3)]
        + [pl.BlockSpec((None, tc, w), col(j)) for j in range(3)]
        + [pl.BlockSpec(bias.shape, lambda b: (0, 0, 0, 0), pipeline_mode=pl.Buffered(1))],
        out_specs=[pl.BlockSpec((None, t, w), lambda b: (b, 0, 0)), pl.BlockSpec((None, tc, w), lambda b: (b, 0, 0))],
        compiler_params=_params("parallel"),
        name="na_attn",
    )(qkv, qkv, qkv, qkvc, qkvc, qkvc, bias)


def _na_bias_table(rpb):
    c = jnp.arange(GRID_W)[:, None]
    kc = jnp.arange(GRID_W)[None, :]
    cstart = jnp.clip(c - NA_WIN_W // 2, 0, GRID_W - NA_WIN_W)
    ok = (kc >= cstart) & (kc < cstart + NA_WIN_W)
    dc = jnp.clip(kc - c + NA_WIN_W - 1, 0, 2 * NA_WIN_W - 2)
    t = jnp.where(ok[None, None], rpb.astype(F32)[:, :, dc] * LOG2E, NEG_INF)
    nd = 2 * NA_WIN_H - 2
    pair = jnp.concatenate([t[:, :nd], t[:, 1:nd + 1]], axis=-1)
    h = rpb.shape[0]
    pair = pair.reshape(h // 2, 2, nd, GRID_W, 2 * GRID_W)
    return jnp.moveaxis(pair, 1, 2).reshape(h // 2, nd, 2 * GRID_W, 2 * GRID_W)


def _dn_kernel(qr_ref, kr_ref, vr_ref, z_ref, ab_ref, qcr_ref, kcr_ref, vcr_ref, zc_ref, abc_ref,
               cwq_ref, cwk_ref, cwv_ref, alog_ref, dtb_ref, nw_ref,
               y_ref, yc_ref,
               pad_ref, abs_ref, aq_ref, b_ref, gl_ref, o_ref, s_ref, *, hg):
    c = DN_CHUNK
    t = qr_ref.shape[0]
    tc = qcr_ref.shape[0]
    ncc, ncl = tc // c, t // c
    lat0 = 2 * SUBLANES + tc
    half = DN_CONV // 2
    n_sq = int(math.log2(c)) - 1

    zeros8 = jnp.zeros((3, SUBLANES, LANES), F32)
    pad_ref[:, 0:SUBLANES, :] = zeros8
    pad_ref[:, SUBLANES + tc:lat0, :] = zeros8
    pad_ref[:, lat0 + t:lat0 + t + SUBLANES, :] = zeros8
    abs_ref[0:tc, :] = abc_ref[...]
    abs_ref[tc:tc + t, :] = ab_ref[...]
    s_ref[...] = jnp.zeros_like(s_ref)

    lane = lax.broadcasted_iota(jnp.int32, (c, LANES), 1)
    row = lax.broadcasted_iota(jnp.int32, (c, LANES), 0)
    col = jnp.bitwise_and(lane, c - 1)
    fwd = lane < c
    incl = (fwd & (row >= col)) | (~fwd & (row <= col))
    strict = (fwd & (row > col)) | (~fwd & (row < col))
    diag = row == col
    eye2 = jnp.where(diag, 1.0, 0.0).astype(F32)
    fwd16 = jnp.where(fwd, 1.0, 0.0).astype(BF16)
    bwd16 = jnp.where(fwd, 0.0, 1.0).astype(BF16)
    ri = lax.broadcasted_iota(jnp.int32, (2 * c, c), 0)
    ci_ = lax.broadcasted_iota(jnp.int32, (2 * c, c), 1)
    tri2 = jnp.where(((ri < c) & (ci_ <= ri)) | ((ri >= c) & (ci_ >= ri - c)), 1.0, 0.0).astype(BF16)
    tri6 = jnp.concatenate([tri2, tri2, tri2], axis=1)
    neg_a = -jnp.exp(alog_ref[...])
    dtb = dtb_ref[...]

    def l2n(x):
        return x * lax.rsqrt(jnp.sum(x * x, axis=-1, keepdims=True) + NORM_EPS)

    def bd(m16):
        return jnp.concatenate([m16 * fwd16, m16 * bwd16], axis=0)

    def mm3(lhs_parts, bh, bl):
        bdh = bd(bh)
        rhs = jnp.concatenate([bdh, bdh, bd(bl)], axis=0)
        lhs = jnp.concatenate([jnp.concatenate([ah, al, ah], axis=1) for ah, al in lhs_parts], axis=0)
        return _dot(lhs, rhs)

    def prep_head(hh):
        hid = pl.program_id(1) * hg + hh
        hs = slice(hh * LANES, (hh + 1) * LANES)
        for s, (cr, lr) in enumerate(((qcr_ref, qr_ref), (kcr_ref, kr_ref), (vcr_ref, vr_ref))):
            pad_ref[s, SUBLANES:SUBLANES + tc, :] = cr[:, hs]
            pad_ref[s, lat0:lat0 + t, :] = lr[:, hs]

        def conv(s, cw_ref, base):
            acc = pad_ref[s, pl.ds(base - half, c), :] * cw_ref[0:1, hs]
            for j in range(1, DN_CONV):
                acc += pad_ref[s, pl.ds(base - half + j, c), :] * cw_ref[j:j + 1, hs]
            return _silu(acc)

        def pick(x, idx):
            return jnp.broadcast_to(jnp.sum(jnp.where(lane == idx, x, 0.0), axis=-1, keepdims=True), (c, LANES))

        def stage_inputs(ch):
            base = ch * c + jnp.where(ch < ncc, SUBLANES, 2 * SUBLANES)
            q = l2n(conv(0, cwq_ref, base))
            k = l2n(conv(1, cwk_ref, base))
            v = conv(2, cwv_ref, base)
            abt = abs_ref[pl.ds(pl.multiple_of(ch * c, c), c), :]
            g_all = neg_a * _softplus(abt + dtb)
            b_all = _sigmoid(abt)
            return q * DN_DK ** -0.5, k, v, g_all, b_all

        def stage_decay(g_all):
            parts = []
            for gp in _split3(g_all):
                gp = gp.astype(F32)
                pf = pick(gp, hid)
                pb = pick(gp, DN_HEADS + hid)
                parts.append(jnp.concatenate([jnp.where(strict, jnp.where(fwd, pf, pb), 0.0), pf, pb], axis=1).astype(BF16))
            res = _dot(tri6, jnp.concatenate(parts, axis=0))
            return jnp.where(fwd, res[:c, 0:LANES], res[c:, 0:LANES]), res[:c, LANES:2 * LANES], res[c:, 2 * LANES:]

        def stage_finish(ch, qs, k, v, qk16, p, gc_f, gc_b, bt_f, bt_b):
            e_f = jnp.exp(gc_f)
            e_b = jnp.exp(gc_b)
            rhs = jnp.concatenate([
                jnp.concatenate([v * bt_f, k * (bt_f * e_f)], axis=1),
                jnp.concatenate([v * bt_b, k * (bt_b * e_b)], axis=1)], axis=0).astype(BF16)
            sol16 = _dot(bd(p.astype(BF16)), rhs).astype(BF16)
            z16 = jnp.zeros((c, 2 * DN_DV), BF16)
            both = jnp.concatenate([jnp.concatenate([sol16[:c], z16], axis=1),
                                    jnp.concatenate([z16, sol16[c:]], axis=1)], axis=0)
            gl_f = gc_f[c - 1:c, :]
            gl_b = gc_b[0:1, :]
            kd = jnp.concatenate([k * jnp.exp(gl_f - gc_f), k * jnp.exp(gl_b - gc_b)], axis=0).astype(BF16)
            qsol = _dot(qk16, both)
            ksol = _dot_tn(kd, both)
            rows = pl.ds(pl.multiple_of(ch * c, c), c)
            for d, (e, gl) in enumerate(((e_f, gl_f), (e_b, gl_b))):
                lo = 2 * DN_DV * d
                aq_ref[hh, d, ch, 0:DN_DK, :] = ksol[:, lo + DN_DV:lo + 2 * DN_DV].astype(BF16)
                aq_ref[hh, d, ch, DN_DK:DN_DK + c, :] = (qs * e - qsol[:, lo + DN_DV:lo + 2 * DN_DV]).astype(BF16)
                b_ref[hh, d, ch] = ksol[:, lo:lo + DN_DV]
                o_ref[hh, d, rows, :] = qsol[:, lo:lo + DN_DV]
                gl_ref[hh, d, ch] = jnp.broadcast_to(jnp.exp(gl), (SUBLANES, LANES))

        def body(i, carry):
            chs = [i * DN_PREP_CHUNKS + j for j in range(DN_PREP_CHUNKS)]
            ins = [stage_inputs(ch) for ch in chs]
            dec = [stage_decay(x[3]) for x in ins]
            bts = [(pick(x[4], 2 * DN_HEADS + hid), pick(x[4], 3 * DN_HEADS + hid)) for x in ins]
            decay = [jnp.where(incl, jnp.exp(jnp.where(incl, d[0], 0.0)), 0.0) for d in dec]
            k16 = [x[1].astype(BF16) for x in ins]
            kk = [jnp.concatenate([a, a], axis=0) for a in k16]
            qk16 = [(_dot_nt(x[0].astype(BF16), b) * dc).astype(BF16) for x, b, dc in zip(ins, kk, decay)]
            lmat = [jnp.where(strict, jnp.where(fwd, bt[0], bt[1]) * _dot_nt(a, b) * dc, 0.0)
                    for a, b, bt, dc in zip(k16, kk, bts, decay)]
            lsp = [_split2(x) for x in lmat]
            m = [mm3([s], s[0], s[1]) for s in lsp]
            p = [eye2 - x for x in lmat]
            for s in range(n_sq):
                msp = [_split2(x) for x in m]
                psp = [_split2(x) for x in p]
                if s < n_sq - 1:
                    res = [mm3([a, b], b[0], b[1]) for a, b in zip(psp, msp)]
                    p = [a + r[:c] for a, r in zip(p, res)]
                    m = [r[c:] for r in res]
                else:
                    p = [a + mm3([b], e[0], e[1]) for a, b, e in zip(p, psp, msp)]
            for j, ch in enumerate(chs):
                stage_finish(ch, ins[j][0], ins[j][1], ins[j][2], qk16[j], p[j], dec[j][1], dec[j][2], *bts[j])
            return carry

        lax.fori_loop(0, (ncc + ncl) // DN_PREP_CHUNKS, body, 0)

    for hh in range(hg):
        prep_head(hh)

    def scan(first, n):
        def body(i, carry):
            for hh in range(hg):
                for d in range(2):
                    ch = first + i if d == 0 else first + n - 1 - i
                    s_mat = s_ref[hh, d]
                    res = _dot(aq_ref[hh, d, ch], s_mat.astype(BF16))
                    s_ref[hh, d] = s_mat * gl_ref[hh, d, ch][0:1, :] + (b_ref[hh, d, ch] - res[:DN_DK])
                    rows = pl.ds(pl.multiple_of(ch * c, c), c)
                    o_ref[hh, d, rows, :] = o_ref[hh, d, rows, :] + res[DN_DK:]
            return carry
        return body

    lax.fori_loop(0, ncc, scan(0, ncc), 0)
    lax.fori_loop(0, ncl, scan(ncc, ncl), 0)

    def gated_norm(o, z):
        return (o * lax.rsqrt(jnp.mean(o * o, axis=-1, keepdims=True) + NORM_EPS) * nw_ref[...] * _silu(z))

    piece = 4 * c
    for hh in range(hg):
        hs = slice(hh * LANES, (hh + 1) * LANES)
        yc_ref[:, hs] = gated_norm(o_ref[hh, 0, 0:tc, :] + o_ref[hh, 1, 0:tc, :], zc_ref[:, hs]).astype(BF16)
        for j in range(t // piece):
            rows = slice(j * piece, (j + 1) * piece)
            orow = slice(tc + j * piece, tc + (j + 1) * piece)
            y_ref[rows, hs] = gated_norm(o_ref[hh, 0, orow, :] + o_ref[hh, 1, orow, :], z_ref[rows, hs]).astype(BF16)


def _dn_call(dn, z, ab, dnc, zc, abc, conv_w, alog, dtb, norm_w):
    bsz, t, _ = dn.shape
    tc = dnc.shape[1]
    h = DN_HEADS
    hg = DN_HEAD_GROUP
    nch = (t + tc) // DN_CHUNK
    c = DN_CHUNK
    wide = hg * LANES

    def slab(rows, j0):
        return pl.BlockSpec((None, rows, wide), lambda b, g: (b, 0, j0 // hg + g))

    def whole(rows):
        return pl.BlockSpec((None, rows, LANES), lambda b, g: (b, 0, 0))

    def cw(j0):
        return pl.BlockSpec((DN_CONV, wide), lambda b, g: (0, j0 // hg + g))

    vec = pl.BlockSpec((1, LANES), lambda b, g: (0, 0))
    return pl.pallas_call(
        functools.partial(_dn_kernel, hg=hg),
        out_shape=[jax.ShapeDtypeStruct((bsz, t, h * DN_DV), BF16), jax.ShapeDtypeStruct((bsz, tc, h * DN_DV), BF16)],
        grid=(bsz, h // hg),
        in_specs=[slab(t, 0), slab(t, h), slab(t, 2 * h), slab(t, 0), whole(t),
                  slab(tc, 0), slab(tc, h), slab(tc, 2 * h), slab(tc, 0), whole(tc),
                  cw(0), cw(h), cw(2 * h), vec, vec, vec],
        out_specs=[slab(t, 0), slab(tc, 0)],
        scratch_shapes=[
            pltpu.VMEM((3, 3 * SUBLANES + tc + t, LANES), F32),
            pltpu.VMEM((tc + t, LANES), F32),
            pltpu.VMEM((hg, 2, nch, DN_DK + c, DN_DV), BF16),
            pltpu.VMEM((hg, 2, nch, DN_DK, DN_DV), F32),
            pltpu.VMEM((hg, 2, nch, SUBLANES, LANES), F32),
            pltpu.VMEM((hg, 2, tc + t, DN_DV), F32),
            pltpu.VMEM((hg, 2, DN_DK, DN_DV), F32),
        ],
        compiler_params=_params("parallel", "parallel"),
        name="deltanet",
    )(dn, dn, dn, z, ab, dnc, dnc, dnc, zc, abc, conv_w, conv_w, conv_w, alog, dtb, norm_w)


def _swa_kernel(sink_ref, q_ref, k_ref, v_ref, kc_ref, vc_ref, o_ref, *, n_blocks):
    blk = SWA_BLOCK
    n_kv_pairs = k_ref.shape[1] // LANES
    slabs = q_ref.shape[1] // LANES // n_kv_pairs
    rows = slabs * blk
    low = lax.broadcasted_iota(jnp.int32, (rows, LANES), 1) < SWA_HD
    low_blk = lax.broadcasted_iota(jnp.int32, (blk, LANES), 1) < SWA_HD
    low1 = lax.broadcasted_iota(jnp.int32, (1, LANES), 1) < SWA_HD
    low16 = jnp.where(low1, 1.0, 0.0).astype(BF16)
    high16 = jnp.where(low1, 0.0, 1.0).astype(BF16)
    qi = lax.broadcasted_iota(jnp.int32, (rows, blk), 0) % blk
    kj = lax.broadcasted_iota(jnp.int32, (rows, blk), 1)

    def block_body(n, carry):
        lo = jnp.maximum(n - 1, 0)
        hi = jnp.minimum(n + 1, n_blocks - 1)
        ok_lo = (kj >= qi) & (n > 0)
        ok_hi = (kj <= qi) & (n < n_blocks - 1)

        def tok(i):
            return pl.ds(pl.multiple_of(i * blk, blk), blk)

        for p in range(n_kv_pairs):
            kvl = slice(p * LANES, (p + 1) * LANES)
            k_all = jnp.concatenate([k_ref[tok(lo), kvl], k_ref[tok(n), kvl], k_ref[tok(hi), kvl], kc_ref[:, kvl]], axis=0)
            v_all = jnp.concatenate([v_ref[tok(lo), kvl], v_ref[tok(n), kvl], v_ref[tok(hi), kvl], vc_ref[:, kvl]], axis=0)
            qs = jnp.concatenate(
                [q_ref[tok(n), (p * slabs + j) * LANES:(p * slabs + j + 1) * LANES] for j in range(slabs)], axis=0)
            res = []
            for half in range(2):
                keep16, drop16 = (high16, low16) if half else (low16, high16)
                s = _dot_nt(qs * keep16, k_all)
                cols = [s[:, j * blk:(j + 1) * blk] for j in range(s.shape[1] // blk)]
                cols[0] = jnp.where(ok_lo, cols[0], NEG_INF)
                cols[2] = jnp.where(ok_hi, cols[2], NEG_INF)
                probs, e_sink = [], []
                for j in range(slabs):
                    cj = [c[j * blk:(j + 1) * blk] for c in cols]
                    sink = sink_ref[(2 * p + half) * slabs + j] * LOG2E
                    m = jnp.maximum(jnp.max(functools.reduce(jnp.maximum, cj), axis=-1, keepdims=True), sink)
                    probs.append(jnp.concatenate([jnp.exp2(c - m).astype(BF16) for c in cj], axis=1))
                    e_sink.append(jnp.exp2(sink - m))
                r = _dot(jnp.concatenate(probs, axis=0), v_all * keep16 + drop16)
                den_lanes = low_blk if half else ~low_blk
                res.append(jnp.concatenate(
                    [r[j * blk:(j + 1) * blk] + jnp.where(den_lanes, e_sink[j], 0.0) for j in range(slabs)], axis=0))
            num = jnp.where(low, res[0], res[1])
            den = pltpu.roll(jnp.where(low, res[1], res[0]), SWA_HD, 1)
            o = (num / den).astype(BF16)
            for j in range(slabs):
                o_ref[tok(n), (p * slabs + j) * LANES:(p * slabs + j + 1) * LANES] = o[j * blk:(j + 1) * blk]
        return carry

    lax.fori_loop(0, n_blocks, block_body, 0)


def _swa_call(sink, q, k, v, kc, vc):
    bsz, t, wq = q.shape
    wk = k.shape[2]
    tc = kc.shape[1]
    return pl.pallas_call(
        functools.partial(_swa_kernel, n_blocks=t // SWA_BLOCK),
        out_shape=jax.ShapeDtypeStruct((bsz, t, wq), BF16),
        grid=(bsz,),
        in_specs=[
            pl.BlockSpec(memory_space=pltpu.SMEM),
            pl.BlockSpec((None, t, wq), lambda b: (b, 0, 0)),
            pl.BlockSpec((None, t, wk), lambda b: (b, 0, 0)),
            pl.BlockSpec((None, t, wk), lambda b: (b, 0, 0)),
            pl.BlockSpec((None, tc, wk), lambda b: (b, 0, 0)),
            pl.BlockSpec((None, tc, wk), lambda b: (b, 0, 0)),
        ],
        out_specs=pl.BlockSpec((None, t, wq), lambda b: (b, 0, 0)),
        compiler_params=_params("parallel"),
        name="swa_attn",
    )(sink, q, k, v, kc, vc)


def _swa_slab_order(w, axis):
    rep = SWA_HEADS // SWA_KV_HEADS
    shape = w.shape
    split = shape[:axis] + (SWA_KV_HEADS // 2, 2, rep, SWA_HD) + shape[axis + 1:]
    return jnp.swapaxes(w.reshape(split), axis + 1, axis + 2).reshape(shape)


def _rope_tables(t):
    pos = jnp.arange(t)
    half = SWA_HD // 2
    inv = jnp.power(ROPE_BASE, -jnp.arange(0, half, 2, dtype=F32) / half)
    ang_r = (pos // GRID_W).astype(F32)[:, None] * inv
    ang_c = (pos % GRID_W).astype(F32)[:, None] * inv
    cos = jnp.concatenate([jnp.cos(ang_r)] * 2 + [jnp.cos(ang_c)] * 2, axis=-1)
    sin = jnp.concatenate([-jnp.sin(ang_r), jnp.sin(ang_r), -jnp.sin(ang_c), jnp.sin(ang_c)], axis=-1)
    return jnp.tile(cos, (1, LANES // SWA_HD)), jnp.tile(sin, (1, LANES // SWA_HD))


def _ffn_weights(w_up, w_down):
    d, f2 = w_up.shape
    f = f2 // 2
    nc = f // FF_CHUNK
    wg = w_up[:, :f].reshape(d, nc, FF_CHUNK).transpose(1, 0, 2).astype(BF16)
    wu = w_up[:, f:].reshape(d, nc, FF_CHUNK).transpose(1, 0, 2).astype(BF16)
    wd = w_down.reshape(nc, FF_CHUNK, d).astype(BF16)
    return wg, wu, wd


def kernel(x, c, ctx, c_ctx, ada_w, ada_b, norm_g, ffn_w_up, ffn_w_down, even_w_in, even_w_out, na_rpb,
           dn_conv_w, dn_a_log, dn_dt_bias, dn_norm_w, odd_w_in, odd_w_out, swa_sink, final_norm_g):
    bsz, t, d = x.shape
    tc = ctx.shape[1]
    depth = ada_w.shape[0]
    ctx_row = bsz

    c16 = jnp.concatenate([c, c_ctx[None, :], jnp.zeros((MOD_ROWS - bsz - 1, d), F32)], axis=0)
    mods = _adaln_call(c16, ada_w, ada_b).reshape(depth, N_MOD, MOD_ROWS, 1, d)
    norm_g3 = norm_g.reshape(depth * 3, 1, d)

    h = x
    hc = ctx.reshape(1, bsz * tc, d)
    for i in range(depth):
        need_ctx = i < depth - 1
        j = i // 2
        w1 = _ffn_weights(ffn_w_up[i, 0], ffn_w_down[i, 0])
        w2 = _ffn_weights(ffn_w_up[i, 1], ffn_w_down[i, 1])
        h = _ffn_call(h, mods, norm_g3, i, 0, None, *w1)
        hc = _ffn_call(hc, mods, norm_g3, i, 0, ctx_row, *w1)
        if i % 2 == 0:
            w_in = even_w_in[j]
            z0 = NA_PROJ + DN_QKV
            z1 = z0 + DN_HEADS * DN_DV
            wab = jnp.pad(w_in[:, z1:], ((0, 0), (0, LANES - (w_in.shape[1] - z1))))
            w_parts = [w.astype(BF16) for w in (w_in[:, :NA_PROJ], w_in[:, NA_PROJ:z0], w_in[:, z0:z1], wab)]
            na, dn, z, ab = _proj_even_call(h, mods, norm_g3, i, None, *w_parts)
            nac, dnc, zc, abc = [a.reshape(bsz, tc, a.shape[-1])
                                 for a in _proj_even_call(hc, mods, norm_g3, i, ctx_row, *w_parts)]
            y_na, y_na_c = _na_call(na, nac, _na_bias_table(na_rpb[j]))
            lane_pad = (0, LANES - 2 * DN_HEADS)
            alog = jnp.pad(dn_a_log[j].reshape(-1), lane_pad)[None, :]
            dtb = jnp.pad(dn_dt_bias[j].reshape(-1), lane_pad)[None, :]
            y_dn, y_dn_c = _dn_call(dn, z, ab, dnc, zc, abc, dn_conv_w[j], alog, dtb, dn_norm_w[j][None, :])
            w_out = even_w_out[j].astype(BF16)
            ws = [w_out[:NA_HEADS * NA_HD], w_out[NA_HEADS * NA_HD:]]
            h = _outproj_call(h, mods, i, None, [y_na, y_dn], ws)
            if need_ctx:
                ycs = [y.reshape(1, bsz * tc, y.shape[-1]) for y in (y_na_c, y_dn_c)]
                hc = _outproj_call(hc, mods, i, ctx_row, ycs, ws)
        else:
            w_in = odd_w_in[j]
            qw = SWA_HEADS * SWA_HD
            kw = SWA_KV_HEADS * SWA_HD
            wq = _swa_slab_order(w_in[:, :qw].astype(BF16), 1)
            wk = w_in[:, qw:qw + kw].astype(BF16)
            wv = w_in[:, qw + kw:].astype(BF16)
            cos, sin = _rope_tables(t)
            q, k, v = _proj_odd_call(h, mods, norm_g3, i, None, wq, wk, wv, cos, sin)
            kc, vc = [a.reshape(bsz, tc, kw) for a in _proj_odd_call(hc, mods, norm_g3, i, ctx_row, None, wk, wv, None, None)]
            y = _swa_call(swa_sink[j], q, k, v, kc, vc)
            h = _outproj_call(h, mods, i, None, [y], [_swa_slab_order(odd_w_out[j].astype(BF16), 0)])
            if need_ctx:
                raise NotImplementedError("context queries of a windowed layer are only needed before a later layer")
        last = i == depth - 1
        h = _ffn_call(h, mods, norm_g3, i, 1, None, *w2, final_g=final_norm_g[None, :] if last else None)
        if need_ctx:
            hc = _ffn_call(hc, mods, norm_g3, i, 1, ctx_row, *w2)
    return h
```

```python
import functools
import math

import jax
import jax.numpy as jnp
from jax import lax
from jax.experimental import pallas as pl
from jax.experimental.pallas import tpu as pltpu

F32 = jnp.float32
BF16 = jnp.bfloat16

GRID_W = 64
N_MOD = 9
NORM_EPS = 1e-6
NEG_INF = -1e30
LOG2E = math.log2(math.e)
ROPE_BASE = 10000.0
NA_HEADS = 8
NA_HD = 64
NA_WIN_H = 8
NA_WIN_W = 16
NA_PROJ = 3 * NA_HEADS * NA_HD
DN_HEADS = 4
DN_DK = 128
DN_DV = 128
DN_CONV = 5
DN_CHUNK = 64
DN_QKV = DN_HEADS * (2 * DN_DK + DN_DV)
SWA_HEADS = 16
SWA_KV_HEADS = 4
SWA_HD = 64
SWA_BLOCK = 128

LANES = 128
SUBLANES = 8
VMEM_LIMIT = 56 * 1024 * 1024

TOK_TILE = 1024
FF_CHUNK = 256
MOD_ROWS = 16
DN_HEAD_GROUP = 2
DN_PREP_CHUNKS = 9


def _params(*sem):
    return pltpu.CompilerParams(dimension_semantics=sem, vmem_limit_bytes=VMEM_LIMIT)


def _dot(a, b):
    return jnp.dot(a, b, preferred_element_type=F32)


def _dot_nt(a, b):
    return lax.dot_general(a, b, (((1,), (1,)), ((), ())), preferred_element_type=F32)


def _dot_tn(a, b):
    return lax.dot_general(a, b, (((0,), (0,)), ((), ())), preferred_element_type=F32)


def _sigmoid(x):
    return 0.5 + 0.5 * jnp.tanh(0.5 * x)


def _silu(x):
    h = 0.5 * x
    return h + h * jnp.tanh(h)


def _softplus(x):
    return jnp.maximum(x, 0.0) + jnp.log1p(jnp.exp(-jnp.abs(x)))


def _modulate(x, g, shift, scale):
    y = x * lax.rsqrt(jnp.mean(x * x, axis=-1, keepdims=True) + NORM_EPS) * g
    return y * (1.0 + scale) + shift


def _split3(x):
    h1 = x.astype(BF16)
    r1 = x - h1.astype(F32)
    h2 = r1.astype(BF16)
    h3 = (r1 - h2.astype(F32)).astype(BF16)
    return h1, h2, h3


def _split2(x):
    hi = x.astype(BF16)
    return hi, (x - hi.astype(F32)).astype(BF16)


def _adaln_kernel(c_ref, w_ref, b_ref, o_ref):
    s = _silu(c_ref[...]).astype(BF16)
    o_ref[...] = _dot(s, w_ref[...].astype(BF16)) + b_ref[...]


def _adaln_call(c16, ada_w, ada_b):
    depth, d, _ = ada_w.shape
    b4 = ada_b.reshape(depth * N_MOD, 1, d)
    return pl.pallas_call(
        _adaln_kernel,
        out_shape=jax.ShapeDtypeStruct((depth, N_MOD, MOD_ROWS, d), F32),
        grid=(depth, N_MOD),
        in_specs=[
            pl.BlockSpec((MOD_ROWS, d), lambda i, k: (0, 0)),
            pl.BlockSpec((None, d, d), lambda i, k: (i, 0, k)),
            pl.BlockSpec((None, 1, d), lambda i, k: (i * N_MOD + k, 0, 0)),
        ],
        out_specs=pl.BlockSpec((None, None, MOD_ROWS, d), lambda i, k: (i, k, 0, 0)),
        compiler_params=_params("arbitrary", "arbitrary"),
        name="adaln",
    )(c16, ada_w, b4)


def _mod_spec(layer, k, ctx_row, d):
    if ctx_row is None:
        return pl.BlockSpec((None, None, None, 1, d), lambda b, t: (layer, k, b, 0, 0))
    return pl.BlockSpec((None, None, None, 1, d), lambda b, t: (layer, k, ctx_row, 0, 0))


def _const_spec(shape):
    nd = len(shape)
    return pl.BlockSpec(shape, lambda b, t: (0,) * nd, pipeline_mode=pl.Buffered(1))


def _ffn_kernel(x_ref, g_ref, sh_ref, sc_ref, gt_ref, wup_ref, wd_ref, *rest, n_mix, final):
    if n_mix:
        mg_ref, rest = rest[0], rest[1:]
        y_refs, w_refs, rest = rest[:n_mix], rest[n_mix:2 * n_mix], rest[2 * n_mix:]
    if final:
        fg_ref, rest = rest[0], rest[1:]
    o_ref, u_ref, acc_ref = rest
    n_chunks, fc, _ = wd_ref.shape
    f = n_chunks * fc

    x = x_ref[...]
    if n_mix:
        mix = _dot(y_refs[0][...], w_refs[0][...])
        for y_ref, w_ref in zip(y_refs[1:], w_refs[1:]):
            mix += _dot(y_ref[...], w_ref[...])
        x = x + mg_ref[...] * mix
        o_ref[...] = x
    u_ref[...] = _modulate(x, g_ref[...], sh_ref[...], sc_ref[...]).astype(BF16)
    acc_ref[...] = jnp.zeros_like(acc_ref)

    def body(c, carry):
        u = u_ref[...]
        off = pl.multiple_of(c * fc, fc)
        gate = _dot(u, wup_ref[:, pl.ds(off, fc)])
        up = _dot(u, wup_ref[:, pl.ds(f + off, fc)])
        a = (_silu(gate) * up).astype(BF16)
        acc_ref[...] += _dot(a, wd_ref[c])
        return carry

    lax.fori_loop(0, n_chunks, body, 0)
    y = (o_ref[...] if n_mix else x_ref[...]) + 0.5 * gt_ref[...] * acc_ref[...]
    if final:
        y = y * lax.rsqrt(jnp.mean(y * y, axis=-1, keepdims=True) + NORM_EPS) * fg_ref[...]
    o_ref[...] = y


def _ffn_call(h, mods, norm_g3, layer, which, ctx_row, w_up, w_down, mix=None, final_g=None):
    bsz, t, d = h.shape
    tm = min(TOK_TILE, t)
    k0 = 6 if which else 0
    in_specs = [
        pl.BlockSpec((None, tm, d), lambda b, i: (b, i, 0)),
        pl.BlockSpec((None, 1, d), lambda b, i: (layer * 3 + (2 if which else 0), 0, 0)),
        _mod_spec(layer, k0, ctx_row, d),
        _mod_spec(layer, k0 + 1, ctx_row, d),
        _mod_spec(layer, k0 + 2, ctx_row, d),
        pl.BlockSpec((None, None) + w_up.shape[2:], lambda b, i: (layer, which, 0, 0), pipeline_mode=pl.Buffered(1)),
        pl.BlockSpec((None, None) + w_down.shape[2:], lambda b, i: (layer, which, 0, 0, 0),
                     pipeline_mode=pl.Buffered(1)),
    ]
    args = [h, norm_g3, mods, mods, mods, w_up, w_down]
    n_mix = 0
    if mix is not None:
        ys, ws = mix
        n_mix = len(ys)
        in_specs.append(_mod_spec(layer, 5, ctx_row, d))
        in_specs += [pl.BlockSpec((None, tm, y.shape[2]), lambda b, i: (b, i, 0)) for y in ys]
        in_specs += [_const_spec(w.shape) for w in ws]
        args += [mods, *ys, *ws]
    if final_g is not None:
        in_specs.append(pl.BlockSpec((1, d), lambda b, i: (0, 0)))
        args.append(final_g)
    return pl.pallas_call(
        functools.partial(_ffn_kernel, n_mix=n_mix, final=final_g is not None),
        out_shape=jax.ShapeDtypeStruct(h.shape, F32),
        grid=(bsz, t // tm),
        in_specs=in_specs,
        out_specs=pl.BlockSpec((None, tm, d), lambda b, i: (b, i, 0)),
        scratch_shapes=[pltpu.VMEM((tm, d), BF16), pltpu.VMEM((tm, d), F32)],
        compiler_params=_params("parallel", "parallel"),
        name="ffn",
    )(*args)


def _proj_even_kernel(x_ref, g_ref, sh_ref, sc_ref, w_ref, na_ref, dn_ref, z_ref, ab_ref):
    u = _modulate(x_ref[...], g_ref[...], sh_ref[...], sc_ref[...]).astype(BF16)
    c0 = 0
    for o_ref in (na_ref, dn_ref, z_ref, ab_ref):
        c1 = c0 + o_ref.shape[1]
        y = _dot(u, w_ref[:, c0:c1])
        if o_ref is na_ref:
            qw = NA_HEADS * NA_HD
            o_ref[:, :qw] = (y[:, :qw] * (NA_HD ** -0.5 * LOG2E)).astype(BF16)
            o_ref[:, qw:] = y[:, qw:].astype(BF16)
        else:
            o_ref[...] = y
        c0 = c1


def _proj_even_call(h, mods, norm_g3, layer, ctx_row, w_in):
    bsz, t, d = h.shape
    tm = min(TOK_TILE, t)
    widths = (NA_PROJ, DN_QKV, DN_HEADS * DN_DV, LANES)
    dtypes = (BF16, F32, F32, F32)
    return pl.pallas_call(
        _proj_even_kernel,
        out_shape=[jax.ShapeDtypeStruct((bsz, t, w), dt) for w, dt in zip(widths, dtypes)],
        grid=(bsz, t // tm),
        in_specs=[
            pl.BlockSpec((None, tm, d), lambda b, i: (b, i, 0)),
            pl.BlockSpec((None, 1, d), lambda b, i: (layer * 3 + 1, 0, 0)),
            _mod_spec(layer, 3, ctx_row, d),
            _mod_spec(layer, 4, ctx_row, d),
            _const_spec(w_in.shape),
        ],
        out_specs=[pl.BlockSpec((None, tm, w), lambda b, i: (b, i, 0)) for w in widths],
        compiler_params=_params("parallel", "parallel"),
        name="proj_even",
    )(h, norm_g3, mods, mods, w_in)


def _rope_slab(x, cos, sin, first):
    swapped = jnp.where(first, pltpu.roll(x, LANES - 16, 1), pltpu.roll(x, 16, 1))
    return x * cos + swapped * sin


def _proj_odd_kernel(x_ref, g_ref, sh_ref, sc_ref, *rest, with_q):
    if with_q:
        wq_ref, wk_ref, wv_ref, cos_ref, sin_ref, q_ref, k_ref, v_ref = rest
    else:
        wk_ref, wv_ref, k_ref, v_ref = rest
    u = _modulate(x_ref[...], g_ref[...], sh_ref[...], sc_ref[...]).astype(BF16)
    v_ref[...] = _dot(u, wv_ref[...]).astype(BF16)
    k = _dot(u, wk_ref[...])
    if not with_q:
        k_ref[...] = k.astype(BF16)
        return
    cos = cos_ref[...]
    sin = sin_ref[...]
    lane = lax.broadcasted_iota(jnp.int32, cos.shape, 1)
    first = (lane % 32) < 16
    for j in range(k.shape[1] // LANES):
        sl = slice(j * LANES, (j + 1) * LANES)
        k_ref[:, sl] = _rope_slab(k[:, sl], cos, sin, first).astype(BF16)
    q = _dot(u, wq_ref[...])
    for j in range(q.shape[1] // LANES):
        sl = slice(j * LANES, (j + 1) * LANES)
        q_ref[:, sl] = (_rope_slab(q[:, sl], cos, sin, first) * (SWA_HD ** -0.5 * LOG2E)).astype(BF16)


def _proj_odd_call(h, mods, norm_g3, layer, ctx_row, wq, wk, wv, cos, sin):
    bsz, t, d = h.shape
    tm = min(TOK_TILE, t)
    with_q = wq is not None
    in_specs = [
        pl.BlockSpec((None, tm, d), lambda b, i: (b, i, 0)),
        pl.BlockSpec((None, 1, d), lambda b, i: (layer * 3 + 1, 0, 0)),
        _mod_spec(layer, 3, ctx_row, d),
        _mod_spec(layer, 4, ctx_row, d),
    ]
    args = [h, norm_g3, mods, mods]
    widths = []
    if with_q:
        in_specs.append(_const_spec(wq.shape))
        args.append(wq)
        widths.append(wq.shape[1])
    in_specs += [_const_spec(wk.shape), _const_spec(wv.shape)]
    args += [wk, wv]
    widths += [wk.shape[1], wv.shape[1]]
    if with_q:
        in_specs += [pl.BlockSpec((tm, LANES), lambda b, i: (i, 0))] * 2
        args += [cos, sin]
    return pl.pallas_call(
        functools.partial(_proj_odd_kernel, with_q=with_q),
        out_shape=[jax.ShapeDtypeStruct((bsz, t, w), BF16) for w in widths],
        grid=(bsz, t // tm),
        in_specs=in_specs,
        out_specs=[pl.BlockSpec((None, tm, w), lambda b, i: (b, i, 0)) for w in widths],
        compiler_params=_params("parallel", "parallel"),
        name="proj_odd",
    )(*args)


def _softmax2_pv(cols, v):
    m = jnp.max(functools.reduce(jnp.maximum, cols), axis=-1, keepdims=True)
    p = [jnp.exp2(c - m) for c in cols]
    den = jnp.sum(functools.reduce(jnp.add, p), axis=-1, keepdims=True)
    return _dot(jnp.concatenate([x.astype(BF16) for x in p], axis=1), v) / den


def _na_kernel(q_ref, k_ref, v_ref, qc_ref, kc_ref, vc_ref, bias_ref, o_ref, oc_ref, *, rows):
    n_pairs = q_ref.shape[1] // LANES
    tc = qc_ref.shape[0]
    win = NA_WIN_H * GRID_W
    n_bias = win // LANES
    low1 = lax.broadcasted_iota(jnp.int32, (1, LANES), 1) < NA_HD
    low16 = jnp.where(low1, 1.0, 0.0).astype(BF16)
    high16 = jnp.where(low1, 0.0, 1.0).astype(BF16)
    low_q = lax.broadcasted_iota(jnp.int32, (GRID_W, LANES), 1) < NA_HD
    low_c = lax.broadcasted_iota(jnp.int32, (tc, LANES), 1) < NA_HD
    pairs = [slice(hp * LANES, (hp + 1) * LANES) for hp in range(n_pairs)]

    def row_body(r, carry):
        r0 = jnp.clip(r - NA_WIN_H // 2, 0, rows - NA_WIN_H)
        doff = r0 - r + NA_WIN_H - 1
        qs = pl.ds(pl.multiple_of(r * GRID_W, GRID_W), GRID_W)
        ks = pl.ds(pl.multiple_of(r0 * GRID_W, GRID_W), win)
        q2 = [q_ref[qs, sl] for sl in pairs]
        qm = [jnp.concatenate([x * low16, x * high16], axis=0) for x in q2]
        s = [_dot_nt(x, jnp.concatenate([k_ref[ks, sl], kc_ref[:, sl]], axis=0)) for x, sl in zip(qm, pairs)]
        cols = [[x[:, j * LANES:(j + 1) * LANES] + bias_ref[hp, doff + 2 * j] for j in range(n_bias)]
                + [x[:, j * LANES:(j + 1) * LANES] for j in range(n_bias, x.shape[1] // LANES)]
                for hp, x in enumerate(s)]
        o = [_softmax2_pv(c, jnp.concatenate([v_ref[ks, sl], vc_ref[:, sl]], axis=0)) for c, sl in zip(cols, pairs)]
        for x, sl in zip(o, pairs):
            o_ref[qs, sl] = jnp.where(low_q, x[:GRID_W], x[GRID_W:]).astype(BF16)
        return carry

    lax.fori_loop(0, rows, row_body, 0)

    for sl in pairs:
        halves = []
        for keep16 in (low16, high16):
            s = _dot_nt(qc_ref[:, sl] * keep16, kc_ref[:, sl])
            halves.append(_softmax2_pv([s[:, j * LANES:(j + 1) * LANES] for j in range(tc // LANES)], vc_ref[:, sl]))
        oc_ref[:, sl] = jnp.where(low_c, halves[0], halves[1]).astype(BF16)


def _na_call(qkv, qkvc, bias):
    bsz, t, w3 = qkv.shape
    tc = qkvc.shape[1]
    w = w3 // 3

    def col(j):
        return lambda b: (b, 0, j)

    return pl.pallas_call(
        functools.partial(_na_kernel, rows=t // GRID_W),
        out_shape=[jax.ShapeDtypeStruct((bsz, t, w), BF16), jax.ShapeDtypeStruct((bsz, tc, w), BF16)],
        grid=(bsz,),
        in_specs=[pl.BlockSpec((None, t, w), col(j)) for j in range(3)]
        + [pl.BlockSpec((None, tc, w), col(j)) for j in range(3)]
        + [pl.BlockSpec(bias.shape, lambda b: (0, 0, 0, 0), pipeline_mode=pl.Buffered(1))],
        out_specs=[pl.BlockSpec((None, t, w), lambda b: (b, 0, 0)), pl.BlockSpec((None, tc, w), lambda b: (b, 0, 0))],
        compiler_params=_params("parallel"),
        name="na_attn",
    )(qkv, qkv, qkv, qkvc, qkvc, qkvc, bias)


def _na_bias_table(rpb):
    c = jnp.arange(GRID_W)[:, None]
    kc = jnp.arange(GRID_W)[None, :]
    cstart = jnp.clip(c - NA_WIN_W // 2, 0, GRID_W - NA_WIN_W)
    ok = (kc >= cstart) & (kc < cstart + NA_WIN_W)
    dc = jnp.clip(kc - c + NA_WIN_W - 1, 0, 2 * NA_WIN_W - 2)
    onehot = (dc[None] == jnp.arange(2 * NA_WIN_W - 1)[:, None, None]).astype(F32)
    h = rpb.shape[0]
    nd = 2 * NA_WIN_H - 2
    rp = (rpb.astype(F32) * LOG2E).reshape(h // 2, 2, 2 * NA_WIN_H - 1, 2 * NA_WIN_W - 1)
    t = jnp.einsum('phdx,xck->pdhck', rp, onehot, precision=lax.Precision.HIGHEST)
    t = jnp.where(ok, t, NEG_INF).reshape(h // 2, 2 * NA_WIN_H - 1, 2 * GRID_W, GRID_W)
    return jnp.concatenate([t[:, :nd], t[:, 1:nd + 1]], axis=-1)


def _dn_kernel(qr_ref, kr_ref, vr_ref, z_ref, ab_ref, qcr_ref, kcr_ref, vcr_ref, zc_ref, abc_ref,
               cwq_ref, cwk_ref, cwv_ref, alog_ref, dtb_ref, nw_ref,
               y_ref, yc_ref,
               pad_ref, abs_ref, aq_ref, b_ref, gl_ref, o_ref, s_ref, *, hg):
    c = DN_CHUNK
    t = qr_ref.shape[0]
    tc = qcr_ref.shape[0]
    ncc, ncl = tc // c, t // c
    lat0 = 2 * SUBLANES + tc
    half = DN_CONV // 2
    n_sq = int(math.log2(c)) - 1

    zeros8 = jnp.zeros((3, SUBLANES, LANES), F32)
    pad_ref[:, 0:SUBLANES, :] = zeros8
    pad_ref[:, SUBLANES + tc:lat0, :] = zeros8
    pad_ref[:, lat0 + t:lat0 + t + SUBLANES, :] = zeros8
    abs_ref[0:tc, :] = abc_ref[...]
    abs_ref[tc:tc + t, :] = ab_ref[...]
    s_ref[...] = jnp.zeros_like(s_ref)

    lane = lax.broadcasted_iota(jnp.int32, (c, LANES), 1)
    row = lax.broadcasted_iota(jnp.int32, (c, LANES), 0)
    col = jnp.bitwise_and(lane, c - 1)
    fwd = lane < c
    incl = (fwd & (row >= col)) | (~fwd & (row <= col))
    strict = (fwd & (row > col)) | (~fwd & (row < col))
    diag = row == col
    eye2 = jnp.where(diag, 1.0, 0.0).astype(F32)
    fwd16 = jnp.where(fwd, 1.0, 0.0).astype(BF16)
    bwd16 = jnp.where(fwd, 0.0, 1.0).astype(BF16)
    ri = lax.broadcasted_iota(jnp.int32, (2 * c, c), 0)
    ci_ = lax.broadcasted_iota(jnp.int32, (2 * c, c), 1)
    tri2 = jnp.where(((ri < c) & (ci_ <= ri)) | ((ri >= c) & (ci_ >= ri - c)), 1.0, 0.0).astype(BF16)
    tri6 = jnp.concatenate([tri2, tri2, tri2], axis=1)
    neg_a = -jnp.exp(alog_ref[...])
    dtb = dtb_ref[...]

    def l2n(x):
        return x * lax.rsqrt(jnp.sum(x * x, axis=-1, keepdims=True) + NORM_EPS)

    def bd(m16):
        return jnp.concatenate([m16 * fwd16, m16 * bwd16], axis=0)

    def mm3(lhs_parts, bh, bl):
        bdh = bd(bh)
        rhs = jnp.concatenate([bdh, bdh, bd(bl)], axis=0)
        lhs = jnp.concatenate([jnp.concatenate([ah, al, ah], axis=1) for ah, al in lhs_parts], axis=0)
        return _dot(lhs, rhs)

    def prep_head(hh):
        hid = pl.program_id(1) * hg + hh
        hs = slice(hh * LANES, (hh + 1) * LANES)
        for s, (cr, lr) in enumerate(((qcr_ref, qr_ref), (kcr_ref, kr_ref), (vcr_ref, vr_ref))):
            pad_ref[s, SUBLANES:SUBLANES + tc, :] = cr[:, hs]
            pad_ref[s, lat0:lat0 + t, :] = lr[:, hs]

        def conv(s, cw_ref, base):
            acc = pad_ref[s, pl.ds(base - half, c), :] * cw_ref[0:1, hs]
            for j in range(1, DN_CONV):
                acc += pad_ref[s, pl.ds(base - half + j, c), :] * cw_ref[j:j + 1, hs]
            return _silu(acc)

        def pick(x, idx):
            return jnp.broadcast_to(jnp.sum(jnp.where(lane == idx, x, 0.0), axis=-1, keepdims=True), (c, LANES))

        def stage_inputs(ch):
            base = ch * c + jnp.where(ch < ncc, SUBLANES, 2 * SUBLANES)
            q = l2n(conv(0, cwq_ref, base))
            k = l2n(conv(1, cwk_ref, base))
            v = conv(2, cwv_ref, base)
            abt = abs_ref[pl.ds(pl.multiple_of(ch * c, c), c), :]
            g_all = neg_a * _softplus(abt + dtb)
            b_all = _sigmoid(abt)
            return q * DN_DK ** -0.5, k, v, g_all, b_all

        def stage_decay(g_all):
            parts = []
            for gp in _split3(g_all):
                gp = gp.astype(F32)
                pf = pick(gp, hid)
                pb = pick(gp, DN_HEADS + hid)
                parts.append(jnp.concatenate([jnp.where(strict, jnp.where(fwd, pf, pb), 0.0), pf, pb], axis=1).astype(BF16))
            res = _dot(tri6, jnp.concatenate(parts, axis=0))
            return jnp.where(fwd, res[:c, 0:LANES], res[c:, 0:LANES]), res[:c, LANES:2 * LANES], res[c:, 2 * LANES:]

        def stage_finish(ch, qs, k, v, qk16, p, gc_f, gc_b, bt_f, bt_b):
            e_f = jnp.exp(gc_f)
            e_b = jnp.exp(gc_b)
            rhs = jnp.concatenate([
                jnp.concatenate([v * bt_f, k * (bt_f * e_f)], axis=1),
                jnp.concatenate([v * bt_b, k * (bt_b * e_b)], axis=1)], axis=0).astype(BF16)
            sol16 = _dot(bd(p.astype(BF16)), rhs).astype(BF16)
            z16 = jnp.zeros((c, 2 * DN_DV), BF16)
            both = jnp.concatenate([jnp.concatenate([sol16[:c], z16], axis=1),
                                    jnp.concatenate([z16, sol16[c:]], axis=1)], axis=0)
            gl_f = gc_f[c - 1:c, :]
            gl_b = gc_b[0:1, :]
            kd = jnp.concatenate([k * jnp.exp(gl_f - gc_f), k * jnp.exp(gl_b - gc_b)], axis=0).astype(BF16)
            qsol = _dot(qk16, both)
            ksol = _dot_tn(kd, both)
            rows = pl.ds(pl.multiple_of(ch * c, c), c)
            for d, (e, gl) in enumerate(((e_f, gl_f), (e_b, gl_b))):
                lo = 2 * DN_DV * d
                aq_ref[hh, d, ch, 0:DN_DK, :] = ksol[:, lo + DN_DV:lo + 2 * DN_DV].astype(BF16)
                aq_ref[hh, d, ch, DN_DK:DN_DK + c, :] = (qs * e - qsol[:, lo + DN_DV:lo + 2 * DN_DV]).astype(BF16)
                b_ref[hh, d, ch] = ksol[:, lo:lo + DN_DV]
                o_ref[hh, d, rows, :] = qsol[:, lo:lo + DN_DV]
                gl_ref[hh, d, ch] = jnp.broadcast_to(jnp.exp(gl), (SUBLANES, LANES))

        def body(i, carry):
            chs = [i * DN_PREP_CHUNKS + j for j in range(DN_PREP_CHUNKS)]
            ins = [stage_inputs(ch) for ch in chs]
            dec = [stage_decay(x[3]) for x in ins]
            bts = [(pick(x[4], 2 * DN_HEADS + hid), pick(x[4], 3 * DN_HEADS + hid)) for x in ins]
            decay = [jnp.where(incl, jnp.exp(jnp.where(incl, d[0], 0.0)), 0.0) for d in dec]
            k16 = [x[1].astype(BF16) for x in ins]
            kk = [jnp.concatenate([a, a], axis=0) for a in k16]
            qk16 = [(_dot_nt(x[0].astype(BF16), b) * dc).astype(BF16) for x, b, dc in zip(ins, kk, decay)]
            lmat = [jnp.where(strict, jnp.where(fwd, bt[0], bt[1]) * _dot_nt(a, b) * dc, 0.0)
                    for a, b, bt, dc in zip(k16, kk, bts, decay)]
            lsp = [_split2(x) for x in lmat]
            m = [mm3([s], s[0], s[1]) for s in lsp]
            p = [eye2 - x for x in lmat]
            for s in range(n_sq):
                msp = [_split2(x) for x in m]
                psp = [_split2(x) for x in p]
                if s < n_sq - 1:
                    res = [mm3([a, b], b[0], b[1]) for a, b in zip(psp, msp)]
                    p = [a + r[:c] for a, r in zip(p, res)]
                    m = [r[c:] for r in res]
                else:
                    p = [a + mm3([b], e[0], e[1]) for a, b, e in zip(p, psp, msp)]
            for j, ch in enumerate(chs):
                stage_finish(ch, ins[j][0], ins[j][1], ins[j][2], qk16[j], p[j], dec[j][1], dec[j][2], *bts[j])
            return carry

        lax.fori_loop(0, (ncc + ncl) // DN_PREP_CHUNKS, body, 0)

    for hh in range(hg):
        prep_head(hh)

    def scan(first, n):
        def body(i, carry):
            for hh in range(hg):
                for d in range(2):
                    ch = first + i if d == 0 else first + n - 1 - i
                    s_mat = s_ref[hh, d]
                    res = _dot(aq_ref[hh, d, ch], s_mat.astype(BF16))
                    s_ref[hh, d] = s_mat * gl_ref[hh, d, ch][0:1, :] + (b_ref[hh, d, ch] - res[:DN_DK])
                    rows = pl.ds(pl.multiple_of(ch * c, c), c)
                    o_ref[hh, d, rows, :] = o_ref[hh, d, rows, :] + res[DN_DK:]
            return carry
        return body

    lax.fori_loop(0, ncc, scan(0, ncc), 0)
    lax.fori_loop(0, ncl, scan(ncc, ncl), 0)

    def gated_norm(o, z):
        return (o * lax.rsqrt(jnp.mean(o * o, axis=-1, keepdims=True) + NORM_EPS) * nw_ref[...] * _silu(z))

    piece = 4 * c
    for hh in range(hg):
        hs = slice(hh * LANES, (hh + 1) * LANES)
        yc_ref[:, hs] = gated_norm(o_ref[hh, 0, 0:tc, :] + o_ref[hh, 1, 0:tc, :], zc_ref[:, hs]).astype(BF16)
        for j in range(t // piece):
            rows = slice(j * piece, (j + 1) * piece)
            orow = slice(tc + j * piece, tc + (j + 1) * piece)
            y_ref[rows, hs] = gated_norm(o_ref[hh, 0, orow, :] + o_ref[hh, 1, orow, :], z_ref[rows, hs]).astype(BF16)


def _dn_call(dn, z, ab, dnc, zc, abc, conv_w, alog, dtb, norm_w):
    bsz, t, _ = dn.shape
    tc = dnc.shape[1]
    h = DN_HEADS
    hg = DN_HEAD_GROUP
    nch = (t + tc) // DN_CHUNK
    c = DN_CHUNK
    wide = hg * LANES

    def slab(rows, j0):
        return pl.BlockSpec((None, rows, wide), lambda b, g: (b, 0, j0 // hg + g))

    def whole(rows):
        return pl.BlockSpec((None, rows, LANES), lambda b, g: (b, 0, 0))

    def cw(j0):
        return pl.BlockSpec((DN_CONV, wide), lambda b, g: (0, j0 // hg + g))

    vec = pl.BlockSpec((1, LANES), lambda b, g: (0, 0))
    return pl.pallas_call(
        functools.partial(_dn_kernel, hg=hg),
        out_shape=[jax.ShapeDtypeStruct((bsz, t, h * DN_DV), BF16), jax.ShapeDtypeStruct((bsz, tc, h * DN_DV), BF16)],
        grid=(bsz, h // hg),
        in_specs=[slab(t, 0), slab(t, h), slab(t, 2 * h), slab(t, 0), whole(t),
                  slab(tc, 0), slab(tc, h), slab(tc, 2 * h), slab(tc, 0), whole(tc),
                  cw(0), cw(h), cw(2 * h), vec, vec, vec],
        out_specs=[slab(t, 0), slab(tc, 0)],
        scratch_shapes=[
            pltpu.VMEM((3, 3 * SUBLANES + tc + t, LANES), F32),
            pltpu.VMEM((tc + t, LANES), F32),
            pltpu.VMEM((hg, 2, nch, DN_DK + c, DN_DV), BF16),
            pltpu.VMEM((hg, 2, nch, DN_DK, DN_DV), F32),
            pltpu.VMEM((hg, 2, nch, SUBLANES, LANES), F32),
            pltpu.VMEM((hg, 2, tc + t, DN_DV), F32),
            pltpu.VMEM((hg, 2, DN_DK, DN_DV), F32),
        ],
        compiler_params=_params("parallel", "parallel"),
        name="deltanet",
    )(dn, dn, dn, z, ab, dnc, dnc, dnc, zc, abc, conv_w, conv_w, conv_w, alog, dtb, norm_w)


def _swa_kernel(sink_ref, q_ref, k_ref, v_ref, kc_ref, vc_ref, o_ref, *, n_blocks):
    blk = SWA_BLOCK
    n_kv_pairs = k_ref.shape[1] // LANES
    slabs = q_ref.shape[1] // LANES // n_kv_pairs
    rows = slabs * blk
    low = lax.broadcasted_iota(jnp.int32, (rows, LANES), 1) < SWA_HD
    low_blk = lax.broadcasted_iota(jnp.int32, (blk, LANES), 1) < SWA_HD
    low1 = lax.broadcasted_iota(jnp.int32, (1, LANES), 1) < SWA_HD
    low16 = jnp.where(low1, 1.0, 0.0).astype(BF16)
    high16 = jnp.where(low1, 0.0, 1.0).astype(BF16)
    qi = lax.broadcasted_iota(jnp.int32, (rows, blk), 0) % blk
    kj = lax.broadcasted_iota(jnp.int32, (rows, blk), 1)

    def block_body(n, carry):
        lo = jnp.maximum(n - 1, 0)
        hi = jnp.minimum(n + 1, n_blocks - 1)
        ok_lo = (kj >= qi) & (n > 0)
        ok_hi = (kj <= qi) & (n < n_blocks - 1)

        def tok(i):
            return pl.ds(pl.multiple_of(i * blk, blk), blk)

        for p in range(n_kv_pairs):
            kvl = slice(p * LANES, (p + 1) * LANES)
            k_all = jnp.concatenate([k_ref[tok(lo), kvl], k_ref[tok(n), kvl], k_ref[tok(hi), kvl], kc_ref[:, kvl]], axis=0)
            v_all = jnp.concatenate([v_ref[tok(lo), kvl], v_ref[tok(n), kvl], v_ref[tok(hi), kvl], vc_ref[:, kvl]], axis=0)
            qs = jnp.concatenate(
                [q_ref[tok(n), (p * slabs + j) * LANES:(p * slabs + j + 1) * LANES] for j in range(slabs)], axis=0)
            res = []
            for half in range(2):
                keep16, drop16 = (high16, low16) if half else (low16, high16)
                s = _dot_nt(qs * keep16, k_all)
                cols = [s[:, j * blk:(j + 1) * blk] for j in range(s.shape[1] // blk)]
                cols[0] = jnp.where(ok_lo, cols[0], NEG_INF)
                cols[2] = jnp.where(ok_hi, cols[2], NEG_INF)
                probs, e_sink = [], []
                for j in range(slabs):
                    cj = [c[j * blk:(j + 1) * blk] for c in cols]
                    sink = sink_ref[(2 * p + half) * slabs + j] * LOG2E
                    m = jnp.maximum(jnp.max(functools.reduce(jnp.maximum, cj), axis=-1, keepdims=True), sink)
                    probs.append(jnp.concatenate([jnp.exp2(c - m).astype(BF16) for c in cj], axis=1))
                    e_sink.append(jnp.exp2(sink - m))
                r = _dot(jnp.concatenate(probs, axis=0), v_all * keep16 + drop16)
                den_lanes = low_blk if half else ~low_blk
                res.append(jnp.concatenate(
                    [r[j * blk:(j + 1) * blk] + jnp.where(den_lanes, e_sink[j], 0.0) for j in range(slabs)], axis=0))
            num = jnp.where(low, res[0], res[1])
            den = pltpu.roll(jnp.where(low, res[1], res[0]), SWA_HD, 1)
            o = (num / den).astype(BF16)
            for j in range(slabs):
                o_ref[tok(n), (p * slabs + j) * LANES:(p * slabs + j + 1) * LANES] = o[j * blk:(j + 1) * blk]
        return carry

    lax.fori_loop(0, n_blocks, block_body, 0)


def _swa_call(sink, q, k, v, kc, vc):
    bsz, t, wq = q.shape
    wk = k.shape[2]
    tc = kc.shape[1]
    return pl.pallas_call(
        functools.partial(_swa_kernel, n_blocks=t // SWA_BLOCK),
        out_shape=jax.ShapeDtypeStruct((bsz, t, wq), BF16),
        grid=(bsz,),
        in_specs=[
            pl.BlockSpec(memory_space=pltpu.SMEM),
            pl.BlockSpec((None, t, wq), lambda b: (b, 0, 0)),
            pl.BlockSpec((None, t, wk), lambda b: (b, 0, 0)),
            pl.BlockSpec((None, t, wk), lambda b: (b, 0, 0)),
            pl.BlockSpec((None, tc, wk), lambda b: (b, 0, 0)),
            pl.BlockSpec((None, tc, wk), lambda b: (b, 0, 0)),
        ],
        out_specs=pl.BlockSpec((None, t, wq), lambda b: (b, 0, 0)),
        compiler_params=_params("parallel"),
        name="swa_attn",
    )(sink, q, k, v, kc, vc)


def _swa_slab_order(w, axis):
    rep = SWA_HEADS // SWA_KV_HEADS
    shape = w.shape
    split = shape[:axis] + (SWA_KV_HEADS // 2, 2, rep, SWA_HD) + shape[axis + 1:]
    return jnp.swapaxes(w.reshape(split), axis + 1, axis + 2).reshape(shape)


def _rope_tables(t):
    pos = jnp.arange(t)
    half = SWA_HD // 2
    inv = jnp.power(ROPE_BASE, -jnp.arange(0, half, 2, dtype=F32) / half)
    ang_r = (pos // GRID_W).astype(F32)[:, None] * inv
    ang_c = (pos % GRID_W).astype(F32)[:, None] * inv
    cos = jnp.concatenate([jnp.cos(ang_r)] * 2 + [jnp.cos(ang_c)] * 2, axis=-1)
    sin = jnp.concatenate([-jnp.sin(ang_r), jnp.sin(ang_r), -jnp.sin(ang_c), jnp.sin(ang_c)], axis=-1)
    return jnp.tile(cos, (1, LANES // SWA_HD)), jnp.tile(sin, (1, LANES // SWA_HD))


def kernel(x, c, ctx, c_ctx, ada_w, ada_b, norm_g, ffn_w_up, ffn_w_down, even_w_in, even_w_out, na_rpb,
           dn_conv_w, dn_a_log, dn_dt_bias, dn_norm_w, odd_w_in, odd_w_out, swa_sink, final_norm_g):
    bsz, t, d = x.shape
    tc = ctx.shape[1]
    depth = ada_w.shape[0]
    ctx_row = bsz

    c16 = jnp.concatenate([c, c_ctx[None, :], jnp.zeros((MOD_ROWS - bsz - 1, d), F32)], axis=0)
    mods = _adaln_call(c16, ada_w, ada_b).reshape(depth, N_MOD, MOD_ROWS, 1, d)
    norm_g3 = norm_g.reshape(depth * 3, 1, d)

    w_up = ffn_w_up.astype(BF16)
    f = ffn_w_down.shape[2]
    w_down = ffn_w_down.astype(BF16).reshape(depth, 2, f // FF_CHUNK, FF_CHUNK, d)

    h = x
    hc = ctx.reshape(1, bsz * tc, d)
    for i in range(depth):
        need_ctx = i < depth - 1
        j = i // 2
        h = _ffn_call(h, mods, norm_g3, i, 0, None, w_up, w_down)
        hc = _ffn_call(hc, mods, norm_g3, i, 0, ctx_row, w_up, w_down)
        mix_c = None
        if i % 2 == 0:
            w_in = even_w_in[j]
            w_in = jnp.pad(w_in, ((0, 0), (0, -w_in.shape[1] % LANES))).astype(BF16)
            na, dn, z, ab = _proj_even_call(h, mods, norm_g3, i, None, w_in)
            nac, dnc, zc, abc = [a.reshape(bsz, tc, a.shape[-1])
                                 for a in _proj_even_call(hc, mods, norm_g3, i, ctx_row, w_in)]
            y_na, y_na_c = _na_call(na, nac, _na_bias_table(na_rpb[j]))
            lane_pad = (0, LANES - 2 * DN_HEADS)
            alog = jnp.pad(dn_a_log[j].reshape(-1), lane_pad)[None, :]
            dtb = jnp.pad(dn_dt_bias[j].reshape(-1), lane_pad)[None, :]
            y_dn, y_dn_c = _dn_call(dn, z, ab, dnc, zc, abc, dn_conv_w[j], alog, dtb, dn_norm_w[j][None, :])
            w_out = even_w_out[j].astype(BF16)
            ws = [w_out[:NA_HEADS * NA_HD], w_out[NA_HEADS * NA_HD:]]
            mix = ([y_na, y_dn], ws)
            if need_ctx:
                mix_c = ([y.reshape(1, bsz * tc, y.shape[-1]) for y in (y_na_c, y_dn_c)], ws)
        else:
            if need_ctx:
                raise NotImplementedError("context queries of a windowed layer are only needed before a later layer")
            w_in = odd_w_in[j]
            qw = SWA_HEADS * SWA_HD
            kw = SWA_KV_HEADS * SWA_HD
            wq = _swa_slab_order(w_in[:, :qw].astype(BF16), 1)
            wk = w_in[:, qw:qw + kw].astype(BF16)
            wv = w_in[:, qw + kw:].astype(BF16)
            cos, sin = _rope_tables(t)
            q, k, v = _proj_odd_call(h, mods, norm_g3, i, None, wq, wk, wv, cos, sin)
            kc, vc = [a.reshape(bsz, tc, kw) for a in _proj_odd_call(hc, mods, norm_g3, i, ctx_row, None, wk, wv, None, None)]
            mix = ([_swa_call(swa_sink[j], q, k, v, kc, vc)], [_swa_slab_order(odd_w_out[j].astype(BF16), 0)])
        last = i == depth - 1
        h = _ffn_call(h, mods, norm_g3, i, 1, None, w_up, w_down, mix=mix,
                      final_g=final_norm_g[None, :] if last else None)
        if need_ctx:
            hc = _ffn_call(hc, mods, norm_g3, i, 1, ctx_row, w_up, w_down, mix=mix_c)
    return h
```

```python
import functools
import math

import jax
import jax.numpy as jnp
from jax import lax
from jax.experimental import pallas as pl
from jax.experimental.pallas import tpu as pltpu

F32 = jnp.float32
BF16 = jnp.bfloat16

GRID_W = 64
N_MOD = 9
NORM_EPS = 1e-6
NEG_INF = -1e30
LOG2E = math.log2(math.e)
ROPE_BASE = 10000.0
NA_HEADS = 8
NA_HD = 64
NA_WIN_H = 8
NA_WIN_W = 16
NA_PROJ = 3 * NA_HEADS * NA_HD
DN_HEADS = 4
DN_DK = 128
DN_DV = 128
DN_CONV = 5
DN_CHUNK = 64
DN_QKV = DN_HEADS * (2 * DN_DK + DN_DV)
SWA_HEADS = 16
SWA_KV_HEADS = 4
SWA_HD = 64
SWA_BLOCK = 128

LANES = 128
SUBLANES = 8
VMEM_LIMIT = 56 * 1024 * 1024

TOK_TILE = 1024
FF_CHUNK = 256
MOD_ROWS = 16
DN_HEAD_GROUP = 2
DN_PREP_CHUNKS = 9


def _params(*sem):
    return pltpu.CompilerParams(dimension_semantics=sem, vmem_limit_bytes=VMEM_LIMIT)


def _dot(a, b):
    return jnp.dot(a, b, preferred_element_type=F32)


def _dot_nt(a, b):
    return lax.dot_general(a, b, (((1,), (1,)), ((), ())), preferred_element_type=F32)


def _dot_tn(a, b):
    return lax.dot_general(a, b, (((0,), (0,)), ((), ())), preferred_element_type=F32)


def _sigmoid(x):
    return 0.5 + 0.5 * jnp.tanh(0.5 * x)


def _silu(x):
    h = 0.5 * x
    return h + h * jnp.tanh(h)


def _softplus(x):
    return jnp.maximum(x, 0.0) + jnp.log1p(jnp.exp(-jnp.abs(x)))


def _modulate(x, g, shift, scale):
    y = x * lax.rsqrt(jnp.mean(x * x, axis=-1, keepdims=True) + NORM_EPS) * g
    return y * (1.0 + scale) + shift


def _split3(x):
    h1 = x.astype(BF16)
    r1 = x - h1.astype(F32)
    h2 = r1.astype(BF16)
    h3 = (r1 - h2.astype(F32)).astype(BF16)
    return h1, h2, h3


def _split2(x):
    hi = x.astype(BF16)
    return hi, (x - hi.astype(F32)).astype(BF16)


def _adaln_kernel(c_ref, w_ref, b_ref, o_ref):
    s = _silu(c_ref[...]).astype(BF16)
    o_ref[...] = _dot(s, w_ref[...].astype(BF16)) + b_ref[...]


def _adaln_call(c16, ada_w, ada_b):
    depth, d, _ = ada_w.shape
    b4 = ada_b.reshape(depth * N_MOD, 1, d)
    return pl.pallas_call(
        _adaln_kernel,
        out_shape=jax.ShapeDtypeStruct((depth, N_MOD, MOD_ROWS, d), F32),
        grid=(depth, N_MOD),
        in_specs=[
            pl.BlockSpec((MOD_ROWS, d), lambda i, k: (0, 0)),
            pl.BlockSpec((None, d, d), lambda i, k: (i, 0, k)),
            pl.BlockSpec((None, 1, d), lambda i, k: (i * N_MOD + k, 0, 0)),
        ],
        out_specs=pl.BlockSpec((None, None, MOD_ROWS, d), lambda i, k: (i, k, 0, 0)),
        compiler_params=_params("arbitrary", "arbitrary"),
        name="adaln",
    )(c16, ada_w, b4)


def _mod_spec(layer, k, ctx_row, d):
    if ctx_row is None:
        return pl.BlockSpec((None, None, None, 1, d), lambda b, t: (layer, k, b, 0, 0))
    return pl.BlockSpec((None, None, None, 1, d), lambda b, t: (layer, k, ctx_row, 0, 0))


def _const_spec(shape):
    nd = len(shape)
    return pl.BlockSpec(shape, lambda b, t: (0,) * nd, pipeline_mode=pl.Buffered(1))


def _ffn_kernel(x_ref, g_ref, sh_ref, sc_ref, gt_ref, wup_ref, wd_ref, *rest, n_mix, final):
    if n_mix:
        mg_ref, rest = rest[0], rest[1:]
        y_refs, w_refs, rest = rest[:n_mix], rest[n_mix:2 * n_mix], rest[2 * n_mix:]
    if final:
        fg_ref, rest = rest[0], rest[1:]
    o_ref, u_ref, acc_ref = rest
    n_chunks, fc, _ = wd_ref.shape
    f = n_chunks * fc

    x = x_ref[...]
    if n_mix:
        mix = _dot(y_refs[0][...], w_refs[0][...])
        for y_ref, w_ref in zip(y_refs[1:], w_refs[1:]):
            mix += _dot(y_ref[...], w_ref[...])
        x = x + mg_ref[...] * mix
        o_ref[...] = x
    u_ref[...] = _modulate(x, g_ref[...], sh_ref[...], sc_ref[...]).astype(BF16)
    acc_ref[...] = jnp.zeros_like(acc_ref)

    def body(c, carry):
        u = u_ref[...]
        off = pl.multiple_of(c * fc, fc)
        gate = _dot(u, wup_ref[:, pl.ds(off, fc)])
        up = _dot(u, wup_ref[:, pl.ds(f + off, fc)])
        a = (_silu(gate) * up).astype(BF16)
        acc_ref[...] += _dot(a, wd_ref[c])
        return carry

    lax.fori_loop(0, n_chunks, body, 0)
    y = (o_ref[...] if n_mix else x_ref[...]) + 0.5 * gt_ref[...] * acc_ref[...]
    if final:
        y = y * lax.rsqrt(jnp.mean(y * y, axis=-1, keepdims=True) + NORM_EPS) * fg_ref[...]
    o_ref[...] = y


def _ffn_call(h, mods, norm_g3, layer, which, ctx_row, w_up, w_down, mix=None, final_g=None):
    bsz, t, d = h.shape
    tm = min(TOK_TILE, t)
    k0 = 6 if which else 0
    in_specs = [
        pl.BlockSpec((None, tm, d), lambda b, i: (b, i, 0)),
        pl.BlockSpec((None, 1, d), lambda b, i: (layer * 3 + (2 if which else 0), 0, 0)),
        _mod_spec(layer, k0, ctx_row, d),
        _mod_spec(layer, k0 + 1, ctx_row, d),
        _mod_spec(layer, k0 + 2, ctx_row, d),
        pl.BlockSpec((None, None) + w_up.shape[2:], lambda b, i: (layer, which, 0, 0), pipeline_mode=pl.Buffered(1)),
        pl.BlockSpec((None, None) + w_down.shape[2:], lambda b, i: (layer, which, 0, 0, 0),
                     pipeline_mode=pl.Buffered(1)),
    ]
    args = [h, norm_g3, mods, mods, mods, w_up, w_down]
    n_mix = 0
    if mix is not None:
        ys, ws = mix
        n_mix = len(ys)
        in_specs.append(_mod_spec(layer, 5, ctx_row, d))
        in_specs += [pl.BlockSpec((None, tm, y.shape[2]), lambda b, i: (b, i, 0)) for y in ys]
        in_specs += [_const_spec(w.shape) for w in ws]
        args += [mods, *ys, *ws]
    if final_g is not None:
        in_specs.append(pl.BlockSpec((1, d), lambda b, i: (0, 0)))
        args.append(final_g)
    return pl.pallas_call(
        functools.partial(_ffn_kernel, n_mix=n_mix, final=final_g is not None),
        out_shape=jax.ShapeDtypeStruct(h.shape, F32),
        grid=(bsz, t // tm),
        in_specs=in_specs,
        out_specs=pl.BlockSpec((None, tm, d), lambda b, i: (b, i, 0)),
        scratch_shapes=[pltpu.VMEM((tm, d), BF16), pltpu.VMEM((tm, d), F32)],
        compiler_params=_params("parallel", "parallel"),
        name="ffn",
    )(*args)


def _proj_even_kernel(x_ref, g_ref, sh_ref, sc_ref, w_ref, na_ref, dn_ref, z_ref, ab_ref):
    u = _modulate(x_ref[...], g_ref[...], sh_ref[...], sc_ref[...]).astype(BF16)
    c0 = 0
    for o_ref in (na_ref, dn_ref, z_ref, ab_ref):
        c1 = c0 + o_ref.shape[1]
        y = _dot(u, w_ref[:, c0:c1])
        if o_ref is na_ref:
            qw = NA_HEADS * NA_HD
            o_ref[:, :qw] = (y[:, :qw] * (NA_HD ** -0.5 * LOG2E)).astype(BF16)
            o_ref[:, qw:] = y[:, qw:].astype(BF16)
        else:
            o_ref[...] = y
        c0 = c1


def _proj_even_call(h, mods, norm_g3, layer, ctx_row, w_in):
    bsz, t, d = h.shape
    tm = min(TOK_TILE, t)
    widths = (NA_PROJ, DN_QKV, DN_HEADS * DN_DV, LANES)
    dtypes = (BF16, F32, F32, F32)
    return pl.pallas_call(
        _proj_even_kernel,
        out_shape=[jax.ShapeDtypeStruct((bsz, t, w), dt) for w, dt in zip(widths, dtypes)],
        grid=(bsz, t // tm),
        in_specs=[
            pl.BlockSpec((None, tm, d), lambda b, i: (b, i, 0)),
            pl.BlockSpec((None, 1, d), lambda b, i: (layer * 3 + 1, 0, 0)),
            _mod_spec(layer, 3, ctx_row, d),
            _mod_spec(layer, 4, ctx_row, d),
            _const_spec(w_in.shape),
        ],
        out_specs=[pl.BlockSpec((None, tm, w), lambda b, i: (b, i, 0)) for w in widths],
        compiler_params=_params("parallel", "parallel"),
        name="proj_even",
    )(h, norm_g3, mods, mods, w_in)


def _rope_slab(x, cos, sin, first):
    swapped = jnp.where(first, pltpu.roll(x, LANES - 16, 1), pltpu.roll(x, 16, 1))
    return x * cos + swapped * sin


def _proj_odd_kernel(x_ref, g_ref, sh_ref, sc_ref, *rest, with_q):
    if with_q:
        wq_ref, wk_ref, wv_ref, cos_ref, sin_ref, q_ref, k_ref, v_ref = rest
    else:
        wk_ref, wv_ref, k_ref, v_ref = rest
    u = _modulate(x_ref[...], g_ref[...], sh_ref[...], sc_ref[...]).astype(BF16)
    v_ref[...] = _dot(u, wv_ref[...]).astype(BF16)
    k = _dot(u, wk_ref[...])
    if not with_q:
        k_ref[...] = k.astype(BF16)
        return
    cos = cos_ref[...]
    sin = sin_ref[...]
    lane = lax.broadcasted_iota(jnp.int32, cos.shape, 1)
    first = (lane % 32) < 16
    for j in range(k.shape[1] // LANES):
        sl = slice(j * LANES, (j + 1) * LANES)
        k_ref[:, sl] = _rope_slab(k[:, sl], cos, sin, first).astype(BF16)
    q = _dot(u, wq_ref[...])
    for j in range(q.shape[1] // LANES):
        sl = slice(j * LANES, (j + 1) * LANES)
        q_ref[:, sl] = (_rope_slab(q[:, sl], cos, sin, first) * (SWA_HD ** -0.5 * LOG2E)).astype(BF16)


def _proj_odd_call(h, mods, norm_g3, layer, ctx_row, wq, wk, wv, cos, sin):
    bsz, t, d = h.shape
    tm = min(TOK_TILE, t)
    with_q = wq is not None
    in_specs = [
        pl.BlockSpec((None, tm, d), lambda b, i: (b, i, 0)),
        pl.BlockSpec((None, 1, d), lambda b, i: (layer * 3 + 1, 0, 0)),
        _mod_spec(layer, 3, ctx_row, d),
        _mod_spec(layer, 4, ctx_row, d),
    ]
    args = [h, norm_g3, mods, mods]
    widths = []
    if with_q:
        in_specs.append(_const_spec(wq.shape))
        args.append(wq)
        widths.append(wq.shape[1])
    in_specs += [_const_spec(wk.shape), _const_spec(wv.shape)]
    args += [wk, wv]
    widths += [wk.shape[1], wv.shape[1]]
    if with_q:
        in_specs += [pl.BlockSpec((tm, LANES), lambda b, i: (i, 0))] * 2
        args += [cos, sin]
    return pl.pallas_call(
        functools.partial(_proj_odd_kernel, with_q=with_q),
        out_shape=[jax.ShapeDtypeStruct((bsz, t, w), BF16) for w in widths],
        grid=(bsz, t // tm),
        in_specs=in_specs,
        out_specs=[pl.BlockSpec((None, tm, w), lambda b, i: (b, i, 0)) for w in widths],
        compiler_params=_params("parallel", "parallel"),
        name="proj_odd",
    )(*args)


def _softmax2_pv(cols, v):
    m = jnp.max(functools.reduce(jnp.maximum, cols), axis=-1, keepdims=True)
    p = [jnp.exp2(c - m) for c in cols]
    den = jnp.sum(functools.reduce(jnp.add, p), axis=-1, keepdims=True)
    return _dot(jnp.concatenate([x.astype(BF16) for x in p], axis=1), v) / den


def _na_kernel(q_ref, k_ref, v_ref, qc_ref, kc_ref, vc_ref, bias_ref, o_ref, oc_ref, *, rows):
    n_pairs = q_ref.shape[1] // LANES
    tc = qc_ref.shape[0]
    win = NA_WIN_H * GRID_W
    n_bias = win // LANES
    low1 = lax.broadcasted_iota(jnp.int32, (1, LANES), 1) < NA_HD
    low16 = jnp.where(low1, 1.0, 0.0).astype(BF16)
    high16 = jnp.where(low1, 0.0, 1.0).astype(BF16)
    low_q = lax.broadcasted_iota(jnp.int32, (GRID_W, LANES), 1) < NA_HD
    low_c = lax.broadcasted_iota(jnp.int32, (tc, LANES), 1) < NA_HD
    pairs = [slice(hp * LANES, (hp + 1) * LANES) for hp in range(n_pairs)]

    def row_body(r, carry):
        r0 = jnp.clip(r - NA_WIN_H // 2, 0, rows - NA_WIN_H)
        doff = r0 - r + NA_WIN_H - 1
        qs = pl.ds(pl.multiple_of(r * GRID_W, GRID_W), GRID_W)
        ks = pl.ds(pl.multiple_of(r0 * GRID_W, GRID_W), win)
        q2 = [q_ref[qs, sl] for sl in pairs]
        qm = [jnp.concatenate([x * low16, x * high16], axis=0) for x in q2]
        s = [_dot_nt(x, jnp.concatenate([k_ref[ks, sl], kc_ref[:, sl]], axis=0)) for x, sl in zip(qm, pairs)]
        cols = [[x[:, j * LANES:(j + 1) * LANES] + bias_ref[hp, doff + 2 * j] for j in range(n_bias)]
                + [x[:, j * LANES:(j + 1) * LANES] for j in range(n_bias, x.shape[1] // LANES)]
                for hp, x in enumerate(s)]
        o = [_softmax2_pv(c, jnp.concatenate([v_ref[ks, sl], vc_ref[:, sl]], axis=0)) for c, sl in zip(cols, pairs)]
        for x, sl in zip(o, pairs):
            o_ref[qs, sl] = jnp.where(low_q, x[:GRID_W], x[GRID_W:]).astype(BF16)
        return carry

    lax.fori_loop(0, rows, row_body, 0)

    for sl in pairs:
        halves = []
        for keep16 in (low16, high16):
            s = _dot_nt(qc_ref[:, sl] * keep16, kc_ref[:, sl])
            halves.append(_softmax2_pv([s[:, j * LANES:(j + 1) * LANES] for j in range(tc // LANES)], vc_ref[:, sl]))
        oc_ref[:, sl] = jnp.where(low_c, halves[0], halves[1]).astype(BF16)


def _na_call(qkv, qkvc, bias):
    bsz, t, w3 = qkv.shape
    tc = qkvc.shape[1]
    w = w3 // 3

    def col(j):
        return lambda b: (b, 0, j)

    return pl.pallas_call(
        functools.partial(_na_kernel, rows=t // GRID_W),
        out_shape=[jax.ShapeDtypeStruct((bsz, t, w), BF16), jax.ShapeDtypeStruct((bsz, tc, w), BF16)],
        grid=(bsz,),
        in_specs=[pl.BlockSpec((None, t, w), col(j)) for j in range(3)]
        + [pl.BlockSpec((None, tc, w), col(j)) for j in range(3)]
        + [pl.BlockSpec(bias.shape, lambda b: (0, 0, 0, 0), pipeline_mode=pl.Buffered(1))],
        out_specs=[pl.BlockSpec((None, t, w), lambda b: (b, 0, 0)), pl.BlockSpec((None, tc, w), lambda b: (b, 0, 0))],
        compiler_params=_params("parallel"),
        name="na_attn",
    )(qkv, qkv, qkv, qkvc, qkvc, qkvc, bias)


def _na_bias_table(rpb):
    c = jnp.arange(GRID_W)[:, None]
    kc = jnp.arange(GRID_W)[None, :]
    cstart = jnp.clip(c - NA_WIN_W // 2, 0, GRID_W - NA_WIN_W)
    ok = (kc >= cstart) & (kc < cstart + NA_WIN_W)
    dc = jnp.clip(kc - c + NA_WIN_W - 1, 0, 2 * NA_WIN_W - 2)
    onehot = (dc[None] == jnp.arange(2 * NA_WIN_W - 1)[:, None, None]).astype(F32)
    h = rpb.shape[0]
    nd = 2 * NA_WIN_H - 2
    rp = (rpb.astype(F32) * LOG2E).reshape(h // 2, 2, 2 * NA_WIN_H - 1, 2 * NA_WIN_W - 1)
    t = jnp.einsum('phdx,xck->pdhck', rp, onehot, precision=lax.Precision.HIGHEST)
    t = jnp.where(ok, t, NEG_INF).reshape(h // 2, 2 * NA_WIN_H - 1, 2 * GRID_W, GRID_W)
    return jnp.concatenate([t[:, :nd], t[:, 1:nd + 1]], axis=-1)


def _dn_kernel(qr_ref, kr_ref, vr_ref, z_ref, ab_ref, qcr_ref, kcr_ref, vcr_ref, zc_ref, abc_ref,
               cwq_ref, cwk_ref, cwv_ref, alog_ref, dtb_ref, nw_ref,
               y_ref, yc_ref,
               pad_ref, abs_ref, aq_ref, b_ref, gl_ref, o_ref, s_ref, *, hg):
    c = DN_CHUNK
    t = qr_ref.shape[0]
    tc = qcr_ref.shape[0]
    ncc, ncl = tc // c, t // c
    lat0 = 2 * SUBLANES + tc
    half = DN_CONV // 2
    n_sq = int(math.log2(c)) - 1

    zeros8 = jnp.zeros((3, SUBLANES, LANES), F32)
    pad_ref[:, 0:SUBLANES, :] = zeros8
    pad_ref[:, SUBLANES + tc:lat0, :] = zeros8
    pad_ref[:, lat0 + t:lat0 + t + SUBLANES, :] = zeros8
    abs_ref[0:tc, :] = abc_ref[...]
    abs_ref[tc:tc + t, :] = ab_ref[...]
    s_ref[...] = jnp.zeros_like(s_ref)

    lane = lax.broadcasted_iota(jnp.int32, (c, LANES), 1)
    row = lax.broadcasted_iota(jnp.int32, (c, LANES), 0)
    col = jnp.bitwise_and(lane, c - 1)
    fwd = lane < c
    incl = (fwd & (row >= col)) | (~fwd & (row <= col))
    strict = (fwd & (row > col)) | (~fwd & (row < col))
    diag = row == col
    eye2 = jnp.where(diag, 1.0, 0.0).astype(F32)
    fwd16 = jnp.where(fwd, 1.0, 0.0).astype(BF16)
    bwd16 = jnp.where(fwd, 0.0, 1.0).astype(BF16)
    ri = lax.broadcasted_iota(jnp.int32, (2 * c, c), 0)
    ci_ = lax.broadcasted_iota(jnp.int32, (2 * c, c), 1)
    tri2 = jnp.where(((ri < c) & (ci_ <= ri)) | ((ri >= c) & (ci_ >= ri - c)), 1.0, 0.0).astype(BF16)
    tri6 = jnp.concatenate([tri2, tri2, tri2], axis=1)
    neg_a = -jnp.exp(alog_ref[...])
    dtb = dtb_ref[...]

    def l2n(x):
        return x * lax.rsqrt(jnp.sum(x * x, axis=-1, keepdims=True) + NORM_EPS)

    def bd(m16):
        return jnp.concatenate([m16 * fwd16, m16 * bwd16], axis=0)

    def mm3(lhs_parts, bh, bl):
        bdh = bd(bh)
        rhs = jnp.concatenate([bdh, bdh, bd(bl)], axis=0)
        lhs = jnp.concatenate([jnp.concatenate([ah, al, ah], axis=1) for ah, al in lhs_parts], axis=0)
        return _dot(lhs, rhs)

    def prep_head(hh):
        hid = pl.program_id(1) * hg + hh
        hs = slice(hh * LANES, (hh + 1) * LANES)
        for s, (cr, lr) in enumerate(((qcr_ref, qr_ref), (kcr_ref, kr_ref), (vcr_ref, vr_ref))):
            pad_ref[s, SUBLANES:SUBLANES + tc, :] = cr[:, hs]
            pad_ref[s, lat0:lat0 + t, :] = lr[:, hs]

        def conv(s, cw_ref, base):
            acc = pad_ref[s, pl.ds(base - half, c), :] * cw_ref[0:1, hs]
            for j in range(1, DN_CONV):
                acc += pad_ref[s, pl.ds(base - half + j, c), :] * cw_ref[j:j + 1, hs]
            return _silu(acc)

        def pick(x, idx):
            return jnp.broadcast_to(jnp.sum(jnp.where(lane == idx, x, 0.0), axis=-1, keepdims=True), (c, LANES))

        def stage_inputs(ch):
            base = ch * c + jnp.where(ch < ncc, SUBLANES, 2 * SUBLANES)
            q = l2n(conv(0, cwq_ref, base))
            k = l2n(conv(1, cwk_ref, base))
            v = conv(2, cwv_ref, base)
            abt = abs_ref[pl.ds(pl.multiple_of(ch * c, c), c), :]
            g_all = neg_a * _softplus(abt + dtb)
            b_all = _sigmoid(abt)
            return q * DN_DK ** -0.5, k, v, g_all, b_all

        def stage_decay(g_all):
            parts = []
            for gp in _split3(g_all):
                gp = gp.astype(F32)
                pf = pick(gp, hid)
                pb = pick(gp, DN_HEADS + hid)
                parts.append(jnp.concatenate([jnp.where(strict, jnp.where(fwd, pf, pb), 0.0), pf, pb], axis=1).astype(BF16))
            res = _dot(tri6, jnp.concatenate(parts, axis=0))
            return jnp.where(fwd, res[:c, 0:LANES], res[c:, 0:LANES]), res[:c, LANES:2 * LANES], res[c:, 2 * LANES:]

        def stage_rhs(qs, k, v, gc_f, gc_b, bt_f, bt_b):
            e_f = jnp.exp(gc_f)
            e_b = jnp.exp(gc_b)
            rhs = jnp.concatenate([
                jnp.concatenate([v * bt_f, k * (bt_f * e_f)], axis=1),
                jnp.concatenate([v * bt_b, k * (bt_b * e_b)], axis=1)], axis=0).astype(BF16)
            gl_f = gc_f[c - 1:c, :]
            gl_b = gc_b[0:1, :]
            kd = jnp.concatenate([k * jnp.exp(gl_f - gc_f), k * jnp.exp(gl_b - gc_b)], axis=0).astype(BF16)
            return rhs, kd, (qs * e_f, qs * e_b), (jnp.exp(gl_f), jnp.exp(gl_b))

        def stage_solve(p, rhs):
            sol16 = _dot(bd(p.astype(BF16)), rhs).astype(BF16)
            z16 = jnp.zeros((c, 2 * DN_DV), BF16)
            return jnp.concatenate([jnp.concatenate([sol16[:c], z16], axis=1),
                                    jnp.concatenate([z16, sol16[c:]], axis=1)], axis=0)

        def stage_store(ch, qsol, ksol, q_dec, egl):
            rows = pl.ds(pl.multiple_of(ch * c, c), c)
            for d in range(2):
                lo = 2 * DN_DV * d
                aq_ref[hh, d, ch, 0:DN_DK, :] = ksol[:, lo + DN_DV:lo + 2 * DN_DV].astype(BF16)
                aq_ref[hh, d, ch, DN_DK:DN_DK + c, :] = (q_dec[d] - qsol[:, lo + DN_DV:lo + 2 * DN_DV]).astype(BF16)
                b_ref[hh, d, ch] = ksol[:, lo:lo + DN_DV]
                o_ref[hh, d, rows, :] = qsol[:, lo:lo + DN_DV]
                gl_ref[hh, d, ch] = jnp.broadcast_to(egl[d], (SUBLANES, LANES))

        def body(i, carry):
            chs = [i * DN_PREP_CHUNKS + j for j in range(DN_PREP_CHUNKS)]
            ins = [stage_inputs(ch) for ch in chs]
            dec = [stage_decay(x[3]) for x in ins]
            bts = [(pick(x[4], 2 * DN_HEADS + hid), pick(x[4], 3 * DN_HEADS + hid)) for x in ins]
            decay = [jnp.where(incl, jnp.exp(jnp.where(incl, d[0], 0.0)), 0.0) for d in dec]
            k16 = [x[1].astype(BF16) for x in ins]
            kk = [jnp.concatenate([a, a], axis=0) for a in k16]
            qk16 = [(_dot_nt(x[0].astype(BF16), b) * dc).astype(BF16) for x, b, dc in zip(ins, kk, decay)]
            lmat = [jnp.where(strict, jnp.where(fwd, bt[0], bt[1]) * _dot_nt(a, b) * dc, 0.0)
                    for a, b, bt, dc in zip(k16, kk, bts, decay)]
            lsp = [_split2(x) for x in lmat]
            m = [mm3([s], s[0], s[1]) for s in lsp]
            p = [eye2 - x for x in lmat]
            for s in range(n_sq):
                msp = [_split2(x) for x in m]
                psp = [_split2(x) for x in p]
                if s < n_sq - 1:
                    res = [mm3([a, b], b[0], b[1]) for a, b in zip(psp, msp)]
                    p = [a + r[:c] for a, r in zip(p, res)]
                    m = [r[c:] for r in res]
                else:
                    p = [a + mm3([b], e[0], e[1]) for a, b, e in zip(p, psp, msp)]
            fin = [stage_rhs(x[0], x[1], x[2], d[1], d[2], *bt) for x, d, bt in zip(ins, dec, bts)]
            both = [stage_solve(a, f[0]) for a, f in zip(p, fin)]
            qsol = [_dot(a, b) for a, b in zip(qk16, both)]
            ksol = [_dot_tn(f[1], b) for f, b in zip(fin, both)]
            for ch, a, b, f in zip(chs, qsol, ksol, fin):
                stage_store(ch, a, b, f[2], f[3])
            return carry

        lax.fori_loop(0, (ncc + ncl) // DN_PREP_CHUNKS, body, 0)

    for hh in range(hg):
        prep_head(hh)

    def scan(first, n):
        def body(i, carry):
            for hh in range(hg):
                for d in range(2):
                    ch = first + i if d == 0 else first + n - 1 - i
                    s_mat = s_ref[hh, d]
                    res = _dot(aq_ref[hh, d, ch], s_mat.astype(BF16))
                    s_ref[hh, d] = s_mat * gl_ref[hh, d, ch][0:1, :] + (b_ref[hh, d, ch] - res[:DN_DK])
                    rows = pl.ds(pl.multiple_of(ch * c, c), c)
                    o_ref[hh, d, rows, :] = o_ref[hh, d, rows, :] + res[DN_DK:]
            return carry
        return body

    lax.fori_loop(0, ncc, scan(0, ncc), 0)
    lax.fori_loop(0, ncl, scan(ncc, ncl), 0)

    def gated_norm(o, z):
        return (o * lax.rsqrt(jnp.mean(o * o, axis=-1, keepdims=True) + NORM_EPS) * nw_ref[...] * _silu(z))

    piece = 4 * c
    for hh in range(hg):
        hs = slice(hh * LANES, (hh + 1) * LANES)
        yc_ref[:, hs] = gated_norm(o_ref[hh, 0, 0:tc, :] + o_ref[hh, 1, 0:tc, :], zc_ref[:, hs]).astype(BF16)
        for j in range(t // piece):
            rows = slice(j * piece, (j + 1) * piece)
            orow = slice(tc + j * piece, tc + (j + 1) * piece)
            y_ref[rows, hs] = gated_norm(o_ref[hh, 0, orow, :] + o_ref[hh, 1, orow, :], z_ref[rows, hs]).astype(BF16)


def _dn_call(dn, z, ab, dnc, zc, abc, conv_w, alog, dtb, norm_w):
    bsz, t, _ = dn.shape
    tc = dnc.shape[1]
    h = DN_HEADS
    hg = DN_HEAD_GROUP
    nch = (t + tc) // DN_CHUNK
    c = DN_CHUNK
    wide = hg * LANES

    def slab(rows, j0):
        return pl.BlockSpec((None, rows, wide), lambda b, g: (b, 0, j0 // hg + g))

    def whole(rows):
        return pl.BlockSpec((None, rows, LANES), lambda b, g: (b, 0, 0))

    def cw(j0):
        return pl.BlockSpec((DN_CONV, wide), lambda b, g: (0, j0 // hg + g))

    vec = pl.BlockSpec((1, LANES), lambda b, g: (0, 0))
    return pl.pallas_call(
        functools.partial(_dn_kernel, hg=hg),
        out_shape=[jax.ShapeDtypeStruct((bsz, t, h * DN_DV), BF16), jax.ShapeDtypeStruct((bsz, tc, h * DN_DV), BF16)],
        grid=(bsz, h // hg),
        in_specs=[slab(t, 0), slab(t, h), slab(t, 2 * h), slab(t, 0), whole(t),
                  slab(tc, 0), slab(tc, h), slab(tc, 2 * h), slab(tc, 0), whole(tc),
                  cw(0), cw(h), cw(2 * h), vec, vec, vec],
        out_specs=[slab(t, 0), slab(tc, 0)],
        scratch_shapes=[
            pltpu.VMEM((3, 3 * SUBLANES + tc + t, LANES), F32),
            pltpu.VMEM((tc + t, LANES), F32),
            pltpu.VMEM((hg, 2, nch, DN_DK + c, DN_DV), BF16),
            pltpu.VMEM((hg, 2, nch, DN_DK, DN_DV), F32),
            pltpu.VMEM((hg, 2, nch, SUBLANES, LANES), F32),
            pltpu.VMEM((hg, 2, tc + t, DN_DV), F32),
            pltpu.VMEM((hg, 2, DN_DK, DN_DV), F32),
        ],
        compiler_params=_params("parallel", "parallel"),
        name="deltanet",
    )(dn, dn, dn, z, ab, dnc, dnc, dnc, zc, abc, conv_w, conv_w, conv_w, alog, dtb, norm_w)


def _swa_kernel(sink_ref, q_ref, k_ref, v_ref, kc_ref, vc_ref, o_ref, *, n_blocks):
    blk = SWA_BLOCK
    n_kv_pairs = k_ref.shape[1] // LANES
    slabs = q_ref.shape[1] // LANES // n_kv_pairs
    rows = slabs * blk
    low = lax.broadcasted_iota(jnp.int32, (rows, LANES), 1) < SWA_HD
    low_blk = lax.broadcasted_iota(jnp.int32, (blk, LANES), 1) < SWA_HD
    low1 = lax.broadcasted_iota(jnp.int32, (1, LANES), 1) < SWA_HD
    low16 = jnp.where(low1, 1.0, 0.0).astype(BF16)
    high16 = jnp.where(low1, 0.0, 1.0).astype(BF16)
    qi = lax.broadcasted_iota(jnp.int32, (rows, blk), 0) % blk
    kj = lax.broadcasted_iota(jnp.int32, (rows, blk), 1)

    def block_body(n, carry):
        lo = jnp.maximum(n - 1, 0)
        hi = jnp.minimum(n + 1, n_blocks - 1)
        ok_lo = (kj >= qi) & (n > 0)
        ok_hi = (kj <= qi) & (n < n_blocks - 1)

        def tok(i):
            return pl.ds(pl.multiple_of(i * blk, blk), blk)

        for p in range(n_kv_pairs):
            kvl = slice(p * LANES, (p + 1) * LANES)
            k_all = jnp.concatenate([k_ref[tok(lo), kvl], k_ref[tok(n), kvl], k_ref[tok(hi), kvl], kc_ref[:, kvl]], axis=0)
            v_all = jnp.concatenate([v_ref[tok(lo), kvl], v_ref[tok(n), kvl], v_ref[tok(hi), kvl], vc_ref[:, kvl]], axis=0)
            qs = jnp.concatenate(
                [q_ref[tok(n), (p * slabs + j) * LANES:(p * slabs + j + 1) * LANES] for j in range(slabs)], axis=0)
            res = []
            for half in range(2):
                keep16, drop16 = (high16, low16) if half else (low16, high16)
                s = _dot_nt(qs * keep16, k_all)
                cols = [s[:, j * blk:(j + 1) * blk] for j in range(s.shape[1] // blk)]
                cols[0] = jnp.where(ok_lo, cols[0], NEG_INF)
                cols[2] = jnp.where(ok_hi, cols[2], NEG_INF)
                probs, e_sink = [], []
                for j in range(slabs):
                    cj = [c[j * blk:(j + 1) * blk] for c in cols]
                    sink = sink_ref[(2 * p + half) * slabs + j] * LOG2E
                    m = jnp.maximum(jnp.max(functools.reduce(jnp.maximum, cj), axis=-1, keepdims=True), sink)
                    probs.append(jnp.concatenate([jnp.exp2(c - m).astype(BF16) for c in cj], axis=1))
                    e_sink.append(jnp.exp2(sink - m))
                r = _dot(jnp.concatenate(probs, axis=0), v_all * keep16 + drop16)
                den_lanes = low_blk if half else ~low_blk
                res.append(jnp.concatenate(
                    [r[j * blk:(j + 1) * blk] + jnp.where(den_lanes, e_sink[j], 0.0) for j in range(slabs)], axis=0))
            num = jnp.where(low, res[0], res[1])
            den = pltpu.roll(jnp.where(low, res[1], res[0]), SWA_HD, 1)
            o = (num / den).astype(BF16)
            for j in range(slabs):
                o_ref[tok(n), (p * slabs + j) * LANES:(p * slabs + j + 1) * LANES] = o[j * blk:(j + 1) * blk]
        return carry

    lax.fori_loop(0, n_blocks, block_body, 0)


def _swa_call(sink, q, k, v, kc, vc):
    bsz, t, wq = q.shape
    wk = k.shape[2]
    tc = kc.shape[1]
    return pl.pallas_call(
        functools.partial(_swa_kernel, n_blocks=t // SWA_BLOCK),
        out_shape=jax.ShapeDtypeStruct((bsz, t, wq), BF16),
        grid=(bsz,),
        in_specs=[
            pl.BlockSpec(memory_space=pltpu.SMEM),
            pl.BlockSpec((None, t, wq), lambda b: (b, 0, 0)),
            pl.BlockSpec((None, t, wk), lambda b: (b, 0, 0)),
            pl.BlockSpec((None, t, wk), lambda b: (b, 0, 0)),
            pl.BlockSpec((None, tc, wk), lambda b: (b, 0, 0)),
            pl.BlockSpec((None, tc, wk), lambda b: (b, 0, 0)),
        ],
        out_specs=pl.BlockSpec((None, t, wq), lambda b: (b, 0, 0)),
        compiler_params=_params("parallel"),
        name="swa_attn",
    )(sink, q, k, v, kc, vc)


def _swa_slab_order(w, axis):
    rep = SWA_HEADS // SWA_KV_HEADS
    shape = w.shape
    split = shape[:axis] + (SWA_KV_HEADS // 2, 2, rep, SWA_HD) + shape[axis + 1:]
    return jnp.swapaxes(w.reshape(split), axis + 1, axis + 2).reshape(shape)


def _rope_tables(t):
    pos = jnp.arange(t)
    half = SWA_HD // 2
    inv = jnp.power(ROPE_BASE, -jnp.arange(0, half, 2, dtype=F32) / half)
    ang_r = (pos // GRID_W).astype(F32)[:, None] * inv
    ang_c = (pos % GRID_W).astype(F32)[:, None] * inv
    cos = jnp.concatenate([jnp.cos(ang_r)] * 2 + [jnp.cos(ang_c)] * 2, axis=-1)
    sin = jnp.concatenate([-jnp.sin(ang_r), jnp.sin(ang_r), -jnp.sin(ang_c), jnp.sin(ang_c)], axis=-1)
    return jnp.tile(cos, (1, LANES // SWA_HD)), jnp.tile(sin, (1, LANES // SWA_HD))


def kernel(x, c, ctx, c_ctx, ada_w, ada_b, norm_g, ffn_w_up, ffn_w_down, even_w_in, even_w_out, na_rpb,
           dn_conv_w, dn_a_log, dn_dt_bias, dn_norm_w, odd_w_in, odd_w_out, swa_sink, final_norm_g):
    bsz, t, d = x.shape
    tc = ctx.shape[1]
    depth = ada_w.shape[0]
    ctx_row = bsz

    c16 = jnp.concatenate([c, c_ctx[None, :], jnp.zeros((MOD_ROWS - bsz - 1, d), F32)], axis=0)
    mods = _adaln_call(c16, ada_w, ada_b).reshape(depth, N_MOD, MOD_ROWS, 1, d)
    norm_g3 = norm_g.reshape(depth * 3, 1, d)

    w_up = ffn_w_up.astype(BF16)
    f = ffn_w_down.shape[2]
    w_down = ffn_w_down.astype(BF16).reshape(depth, 2, f // FF_CHUNK, FF_CHUNK, d)

    h = x
    hc = ctx.reshape(1, bsz * tc, d)
    for i in range(depth):
        need_ctx = i < depth - 1
        j = i // 2
        h = _ffn_call(h, mods, norm_g3, i, 0, None, w_up, w_down)
        hc = _ffn_call(hc, mods, norm_g3, i, 0, ctx_row, w_up, w_down)
        mix_c = None
        if i % 2 == 0:
            w_in = even_w_in[j]
            w_in = jnp.pad(w_in, ((0, 0), (0, -w_in.shape[1] % LANES))).astype(BF16)
            na, dn, z, ab = _proj_even_call(h, mods, norm_g3, i, None, w_in)
            nac, dnc, zc, abc = [a.reshape(bsz, tc, a.shape[-1])
                                 for a in _proj_even_call(hc, mods, norm_g3, i, ctx_row, w_in)]
            y_na, y_na_c = _na_call(na, nac, _na_bias_table(na_rpb[j]))
            lane_pad = (0, LANES - 2 * DN_HEADS)
            alog = jnp.pad(dn_a_log[j].reshape(-1), lane_pad)[None, :]
            dtb = jnp.pad(dn_dt_bias[j].reshape(-1), lane_pad)[None, :]
            y_dn, y_dn_c = _dn_call(dn, z, ab, dnc, zc, abc, dn_conv_w[j], alog, dtb, dn_norm_w[j][None, :])
            w_out = even_w_out[j].astype(BF16)
            ws = [w_out[:NA_HEADS * NA_HD], w_out[NA_HEADS * NA_HD:]]
            mix = ([y_na, y_dn], ws)
            if need_ctx:
                mix_c = ([y.reshape(1, bsz * tc, y.shape[-1]) for y in (y_na_c, y_dn_c)], ws)
        else:
            if need_ctx:
                raise NotImplementedError("context queries of a windowed layer are only needed before a later layer")
            w_in = odd_w_in[j]
            qw = SWA_HEADS * SWA_HD
            kw = SWA_KV_HEADS * SWA_HD
            wq = _swa_slab_order(w_in[:, :qw].astype(BF16), 1)
            wk = w_in[:, qw:qw + kw].astype(BF16)
            wv = w_in[:, qw + kw:].astype(BF16)
            cos, sin = _rope_tables(t)
            q, k, v = _proj_odd_call(h, mods, norm_g3, i, None, wq, wk, wv, cos, sin)
            kc, vc = [a.reshape(bsz, tc, kw) for a in _proj_odd_call(hc, mods, norm_g3, i, ctx_row, None, wk, wv, None, None)]
            mix = ([_swa_call(swa_sink[j], q, k, v, kc, vc)], [_swa_slab_order(odd_w_out[j].astype(BF16), 0)])
        last = i == depth - 1
        h = _ffn_call(h, mods, norm_g3, i, 1, None, w_up, w_down, mix=mix,
                      final_g=final_norm_g[None, :] if last else None)
        if need_ctx:
            hc = _ffn_call(hc, mods, norm_g3, i, 1, ctx_row, w_up, w_down, mix=mix_c)
    return h
```

```python
import functools
import math

import jax
import jax.numpy as jnp
from jax import lax
from jax.experimental import pallas as pl
from jax.experimental.pallas import tpu as pltpu

F32 = jnp.float32
BF16 = jnp.bfloat16

GRID_W = 64
N_MOD = 9
NORM_EPS = 1e-6
NEG_INF = -1e30
LOG2E = math.log2(math.e)
ROPE_BASE = 10000.0
NA_HEADS = 8
NA_HD = 64
NA_WIN_H = 8
NA_WIN_W = 16
NA_PROJ = 3 * NA_HEADS * NA_HD
DN_HEADS = 4
DN_DK = 128
DN_DV = 128
DN_CONV = 5
DN_CHUNK = 64
DN_QKV = DN_HEADS * (2 * DN_DK + DN_DV)
SWA_HEADS = 16
SWA_KV_HEADS = 4
SWA_HD = 64
SWA_BLOCK = 128

LANES = 128
SUBLANES = 8
VMEM_LIMIT = 56 * 1024 * 1024

TOK_TILE = 1024
FF_CHUNK = 256
MOD_ROWS = 16
DN_HEAD_GROUP = 2
NA_ROWS_PER_STEP = 2
DN_PREP_CHUNKS = 9


def _params(*sem):
    return pltpu.CompilerParams(dimension_semantics=sem, vmem_limit_bytes=VMEM_LIMIT)


def _dot(a, b):
    return jnp.dot(a, b, preferred_element_type=F32)


def _dot_nt(a, b):
    return lax.dot_general(a, b, (((1,), (1,)), ((), ())), preferred_element_type=F32)


def _dot_tn(a, b):
    return lax.dot_general(a, b, (((0,), (0,)), ((), ())), preferred_element_type=F32)


def _sigmoid(x):
    return 0.5 + 0.5 * jnp.tanh(0.5 * x)


def _silu(x):
    h = 0.5 * x
    return h + h * jnp.tanh(h)


def _softplus(x):
    return jnp.maximum(x, 0.0) + jnp.log1p(jnp.exp(-jnp.abs(x)))


def _modulate(x, g, shift, scale):
    y = x * lax.rsqrt(jnp.mean(x * x, axis=-1, keepdims=True) + NORM_EPS) * g
    return y * (1.0 + scale) + shift


def _split3(x):
    h1 = x.astype(BF16)
    r1 = x - h1.astype(F32)
    h2 = r1.astype(BF16)
    h3 = (r1 - h2.astype(F32)).astype(BF16)
    return h1, h2, h3


def _split2(x):
    hi = x.astype(BF16)
    return hi, (x - hi.astype(F32)).astype(BF16)


def _adaln_kernel(c_ref, w_ref, b_ref, o_ref):
    s = _silu(c_ref[...]).astype(BF16)
    o_ref[...] = _dot(s, w_ref[...].astype(BF16)) + b_ref[...]


def _adaln_call(c16, ada_w, ada_b):
    depth, d, _ = ada_w.shape
    b4 = ada_b.reshape(depth * N_MOD, 1, d)
    return pl.pallas_call(
        _adaln_kernel,
        out_shape=jax.ShapeDtypeStruct((depth, N_MOD, MOD_ROWS, d), F32),
        grid=(depth, N_MOD),
        in_specs=[
            pl.BlockSpec((MOD_ROWS, d), lambda i, k: (0, 0)),
            pl.BlockSpec((None, d, d), lambda i, k: (i, 0, k)),
            pl.BlockSpec((None, 1, d), lambda i, k: (i * N_MOD + k, 0, 0)),
        ],
        out_specs=pl.BlockSpec((None, None, MOD_ROWS, d), lambda i, k: (i, k, 0, 0)),
        compiler_params=_params("arbitrary", "arbitrary"),
        name="adaln",
    )(c16, ada_w, b4)


def _mod_spec(layer, k, ctx_row, d):
    if ctx_row is None:
        return pl.BlockSpec((None, None, None, 1, d), lambda b, t: (layer, k, b, 0, 0))
    return pl.BlockSpec((None, None, None, 1, d), lambda b, t: (layer, k, ctx_row, 0, 0))


def _const_spec(shape):
    nd = len(shape)
    return pl.BlockSpec(shape, lambda b, t: (0,) * nd, pipeline_mode=pl.Buffered(1))


def _ffn_kernel(x_ref, g_ref, sh_ref, sc_ref, gt_ref, wup_ref, wd_ref, *rest, n_mix, final):
    if n_mix:
        mg_ref, rest = rest[0], rest[1:]
        y_refs, w_refs, rest = rest[:n_mix], rest[n_mix:2 * n_mix], rest[2 * n_mix:]
    if final:
        fg_ref, rest = rest[0], rest[1:]
    o_ref, u_ref, acc_ref = rest
    n_chunks, fc, _ = wd_ref.shape
    f = n_chunks * fc

    x = x_ref[...]
    if n_mix:
        mix = _dot(y_refs[0][...], w_refs[0][...])
        for y_ref, w_ref in zip(y_refs[1:], w_refs[1:]):
            mix += _dot(y_ref[...], w_ref[...])
        x = x + mg_ref[...] * mix
        o_ref[...] = x
    u_ref[...] = _modulate(x, g_ref[...], sh_ref[...], sc_ref[...]).astype(BF16)
    acc_ref[...] = jnp.zeros_like(acc_ref)

    def body(c, carry):
        u = u_ref[...]
        off = pl.multiple_of(c * fc, fc)
        gate = _dot(u, wup_ref[:, pl.ds(off, fc)])
        up = _dot(u, wup_ref[:, pl.ds(f + off, fc)])
        a = (_silu(gate) * up).astype(BF16)
        acc_ref[...] += _dot(a, wd_ref[c])
        return carry

    lax.fori_loop(0, n_chunks, body, 0)
    y = (o_ref[...] if n_mix else x_ref[...]) + 0.5 * gt_ref[...] * acc_ref[...]
    if final:
        y = y * lax.rsqrt(jnp.mean(y * y, axis=-1, keepdims=True) + NORM_EPS) * fg_ref[...]
    o_ref[...] = y


def _ffn_call(h, mods, norm_g3, layer, which, ctx_row, w_up, w_down, mix=None, final_g=None):
    bsz, t, d = h.shape
    tm = min(TOK_TILE, t)
    k0 = 6 if which else 0
    in_specs = [
        pl.BlockSpec((None, tm, d), lambda b, i: (b, i, 0)),
        pl.BlockSpec((None, 1, d), lambda b, i: (layer * 3 + (2 if which else 0), 0, 0)),
        _mod_spec(layer, k0, ctx_row, d),
        _mod_spec(layer, k0 + 1, ctx_row, d),
        _mod_spec(layer, k0 + 2, ctx_row, d),
        pl.BlockSpec((None, None) + w_up.shape[2:], lambda b, i: (layer, which, 0, 0), pipeline_mode=pl.Buffered(1)),
        pl.BlockSpec((None, None) + w_down.shape[2:], lambda b, i: (layer, which, 0, 0, 0),
                     pipeline_mode=pl.Buffered(1)),
    ]
    args = [h, norm_g3, mods, mods, mods, w_up, w_down]
    n_mix = 0
    if mix is not None:
        ys, ws = mix
        n_mix = len(ys)
        in_specs.append(_mod_spec(layer, 5, ctx_row, d))
        in_specs += [pl.BlockSpec((None, tm, y.shape[2]), lambda b, i: (b, i, 0)) for y in ys]
        in_specs += [_const_spec(w.shape) for w in ws]
        args += [mods, *ys, *ws]
    if final_g is not None:
        in_specs.append(pl.BlockSpec((1, d), lambda b, i: (0, 0)))
        args.append(final_g)
    return pl.pallas_call(
        functools.partial(_ffn_kernel, n_mix=n_mix, final=final_g is not None),
        out_shape=jax.ShapeDtypeStruct(h.shape, F32),
        grid=(bsz, t // tm),
        in_specs=in_specs,
        out_specs=pl.BlockSpec((None, tm, d), lambda b, i: (b, i, 0)),
        scratch_shapes=[pltpu.VMEM((tm, d), BF16), pltpu.VMEM((tm, d), F32)],
        compiler_params=_params("parallel", "parallel"),
        name="ffn",
    )(*args)


def _proj_even_kernel(x_ref, g_ref, sh_ref, sc_ref, w_ref, na_ref, dn_ref, z_ref, ab_ref):
    u = _modulate(x_ref[...], g_ref[...], sh_ref[...], sc_ref[...]).astype(BF16)
    c0 = 0
    for o_ref in (na_ref, dn_ref, z_ref, ab_ref):
        c1 = c0 + o_ref.shape[1]
        y = _dot(u, w_ref[:, c0:c1])
        if o_ref is na_ref:
            qw = NA_HEADS * NA_HD
            o_ref[:, :qw] = (y[:, :qw] * (NA_HD ** -0.5 * LOG2E)).astype(BF16)
            o_ref[:, qw:] = y[:, qw:].astype(BF16)
        else:
            o_ref[...] = y
        c0 = c1


def _proj_even_call(h, mods, norm_g3, layer, ctx_row, w_in):
    bsz, t, d = h.shape
    tm = min(TOK_TILE, t)
    widths = (NA_PROJ, DN_QKV, DN_HEADS * DN_DV, LANES)
    dtypes = (BF16, F32, F32, F32)
    return pl.pallas_call(
        _proj_even_kernel,
        out_shape=[jax.ShapeDtypeStruct((bsz, t, w), dt) for w, dt in zip(widths, dtypes)],
        grid=(bsz, t // tm),
        in_specs=[
            pl.BlockSpec((None, tm, d), lambda b, i: (b, i, 0)),
            pl.BlockSpec((None, 1, d), lambda b, i: (layer * 3 + 1, 0, 0)),
            _mod_spec(layer, 3, ctx_row, d),
            _mod_spec(layer, 4, ctx_row, d),
            _const_spec(w_in.shape),
        ],
        out_specs=[pl.BlockSpec((None, tm, w), lambda b, i: (b, i, 0)) for w in widths],
        compiler_params=_params("parallel", "parallel"),
        name="proj_even",
    )(h, norm_g3, mods, mods, w_in)


def _rope_slab(x, cos, sin, first):
    swapped = jnp.where(first, pltpu.roll(x, LANES - 16, 1), pltpu.roll(x, 16, 1))
    return x * cos + swapped * sin


def _proj_odd_kernel(x_ref, g_ref, sh_ref, sc_ref, *rest, with_q):
    if with_q:
        wq_ref, wk_ref, wv_ref, cos_ref, sin_ref, q_ref, k_ref, v_ref = rest
    else:
        wk_ref, wv_ref, k_ref, v_ref = rest
    u = _modulate(x_ref[...], g_ref[...], sh_ref[...], sc_ref[...]).astype(BF16)
    v_ref[...] = _dot(u, wv_ref[...]).astype(BF16)
    k = _dot(u, wk_ref[...])
    if not with_q:
        k_ref[...] = k.astype(BF16)
        return
    cos = cos_ref[...]
    sin = sin_ref[...]
    lane = lax.broadcasted_iota(jnp.int32, cos.shape, 1)
    first = (lane % 32) < 16
    for j in range(k.shape[1] // LANES):
        sl = slice(j * LANES, (j + 1) * LANES)
        k_ref[:, sl] = _rope_slab(k[:, sl], cos, sin, first).astype(BF16)
    q = _dot(u, wq_ref[...])
    for j in range(q.shape[1] // LANES):
        sl = slice(j * LANES, (j + 1) * LANES)
        q_ref[:, sl] = (_rope_slab(q[:, sl], cos, sin, first) * (SWA_HD ** -0.5 * LOG2E)).astype(BF16)


def _proj_odd_call(h, mods, norm_g3, layer, ctx_row, wq, wk, wv, cos, sin):
    bsz, t, d = h.shape
    tm = min(TOK_TILE, t)
    with_q = wq is not None
    in_specs = [
        pl.BlockSpec((None, tm, d), lambda b, i: (b, i, 0)),
        pl.BlockSpec((None, 1, d), lambda b, i: (layer * 3 + 1, 0, 0)),
        _mod_spec(layer, 3, ctx_row, d),
        _mod_spec(layer, 4, ctx_row, d),
    ]
    args = [h, norm_g3, mods, mods]
    widths = []
    if with_q:
        in_specs.append(_const_spec(wq.shape))
        args.append(wq)
        widths.append(wq.shape[1])
    in_specs += [_const_spec(wk.shape), _const_spec(wv.shape)]
    args += [wk, wv]
    widths += [wk.shape[1], wv.shape[1]]
    if with_q:
        in_specs += [pl.BlockSpec((tm, LANES), lambda b, i: (i, 0))] * 2
        args += [cos, sin]
    return pl.pallas_call(
        functools.partial(_proj_odd_kernel, with_q=with_q),
        out_shape=[jax.ShapeDtypeStruct((bsz, t, w), BF16) for w in widths],
        grid=(bsz, t // tm),
        in_specs=in_specs,
        out_specs=[pl.BlockSpec((None, tm, w), lambda b, i: (b, i, 0)) for w in widths],
        compiler_params=_params("parallel", "parallel"),
        name="proj_odd",
    )(*args)


def _softmax2(cols):
    m = jnp.max(functools.reduce(jnp.maximum, cols), axis=-1, keepdims=True)
    p = [jnp.exp2(c - m) for c in cols]
    den = jnp.sum(functools.reduce(jnp.add, p), axis=-1, keepdims=True)
    return jnp.concatenate([x.astype(BF16) for x in p], axis=1), den


def _na_kernel(q_ref, k_ref, v_ref, qc_ref, kc_ref, vc_ref, bias_ref, o_ref, oc_ref, *, rows):
    n_pairs = q_ref.shape[1] // LANES
    tc = qc_ref.shape[0]
    win = NA_WIN_H * GRID_W
    n_bias = win // LANES
    low1 = lax.broadcasted_iota(jnp.int32, (1, LANES), 1) < NA_HD
    low16 = jnp.where(low1, 1.0, 0.0).astype(BF16)
    high16 = jnp.where(low1, 0.0, 1.0).astype(BF16)
    low_q = lax.broadcasted_iota(jnp.int32, (GRID_W, LANES), 1) < NA_HD
    low_c = lax.broadcasted_iota(jnp.int32, (tc, LANES), 1) < NA_HD
    pairs = [slice(hp * LANES, (hp + 1) * LANES) for hp in range(n_pairs)]

    def row_body(i, carry):
        units = []
        for rr in range(NA_ROWS_PER_STEP):
            r = i * NA_ROWS_PER_STEP + rr
            r0 = jnp.clip(r - NA_WIN_H // 2, 0, rows - NA_WIN_H)
            doff = r0 - r + NA_WIN_H - 1
            qs = pl.ds(pl.multiple_of(r * GRID_W, GRID_W), GRID_W)
            ks = pl.ds(pl.multiple_of(r0 * GRID_W, GRID_W), win)
            units += [(hp, sl, doff, qs, ks) for hp, sl in enumerate(pairs)]
        qm = [jnp.concatenate([q_ref[qs, sl] * low16, q_ref[qs, sl] * high16], axis=0) for _, sl, _, qs, _ in units]
        s = [_dot_nt(x, jnp.concatenate([k_ref[ks, sl], kc_ref[:, sl]], axis=0))
             for x, (_, sl, _, _, ks) in zip(qm, units)]
        cols = [[x[:, j * LANES:(j + 1) * LANES] + bias_ref[hp, doff + 2 * j] for j in range(n_bias)]
                + [x[:, j * LANES:(j + 1) * LANES] for j in range(n_bias, x.shape[1] // LANES)]
                for x, (hp, _, doff, _, _) in zip(s, units)]
        pd = [_softmax2(c) for c in cols]
        o = [_dot(p, jnp.concatenate([v_ref[ks, sl], vc_ref[:, sl]], axis=0)) / den
             for (p, den), (_, sl, _, _, ks) in zip(pd, units)]
        for x, (_, sl, _, qs, _) in zip(o, units):
            o_ref[qs, sl] = jnp.where(low_q, x[:GRID_W], x[GRID_W:]).astype(BF16)
        return carry

    lax.fori_loop(0, rows // NA_ROWS_PER_STEP, row_body, 0)

    for sl in pairs:
        halves = []
        for keep16 in (low16, high16):
            s = _dot_nt(qc_ref[:, sl] * keep16, kc_ref[:, sl])
            p, den = _softmax2([s[:, j * LANES:(j + 1) * LANES] for j in range(tc // LANES)])
            halves.append(_dot(p, vc_ref[:, sl]) / den)
        oc_ref[:, sl] = jnp.where(low_c, halves[0], halves[1]).astype(BF16)


def _na_call(qkv, qkvc, bias):
    bsz, t, w3 = qkv.shape
    tc = qkvc.shape[1]
    w = w3 // 3

    def col(j):
        return lambda b: (b, 0, j)

    return pl.pallas_call(
        functools.partial(_na_kernel, rows=t // GRID_W),
        out_shape=[jax.ShapeDtypeStruct((bsz, t, w), BF16), jax.ShapeDtypeStruct((bsz, tc, w), BF16)],
        grid=(bsz,),
        in_specs=[pl.BlockSpec((None, t, w), col(j)) for j in range(3)]
        + [pl.BlockSpec((None, tc, w), col(j)) for j in range(3)]
        + [pl.BlockSpec(bias.shape, lambda b: (0, 0, 0, 0), pipeline_mode=pl.Buffered(1))],
        out_specs=[pl.BlockSpec((None, t, w), lambda b: (b, 0, 0)), pl.BlockSpec((None, tc, w), lambda b: (b, 0, 0))],
        compiler_params=_params("parallel"),
        name="na_attn",
    )(qkv, qkv, qkv, qkvc, qkvc, qkvc, bias)


def _na_bias_table(rpb):
    c = jnp.arange(GRID_W)[:, None]
    kc = jnp.arange(GRID_W)[None, :]
    cstart = jnp.clip(c - NA_WIN_W // 2, 0, GRID_W - NA_WIN_W)
    ok = (kc >= cstart) & (kc < cstart + NA_WIN_W)
    dc = jnp.clip(kc - c + NA_WIN_W - 1, 0, 2 * NA_WIN_W - 2)
    onehot = (dc[None] == jnp.arange(2 * NA_WIN_W - 1)[:, None, None]).astype(F32)
    h = rpb.shape[0]
    nd = 2 * NA_WIN_H - 2
    rp = (rpb.astype(F32) * LOG2E).reshape(h // 2, 2, 2 * NA_WIN_H - 1, 2 * NA_WIN_W - 1)
    t = jnp.einsum('phdx,xck->pdhck', rp, onehot, precision=lax.Precision.HIGHEST)
    t = jnp.where(ok, t, NEG_INF).reshape(h // 2, 2 * NA_WIN_H - 1, 2 * GRID_W, GRID_W)
    return jnp.concatenate([t[:, :nd], t[:, 1:nd + 1]], axis=-1)


def _dn_kernel(qr_ref, kr_ref, vr_ref, z_ref, ab_ref, qcr_ref, kcr_ref, vcr_ref, zc_ref, abc_ref,
               cwq_ref, cwk_ref, cwv_ref, alog_ref, dtb_ref, nw_ref,
               y_ref, yc_ref,
               pad_ref, abs_ref, aq_ref, b_ref, gl_ref, o_ref, s_ref, *, hg):
    c = DN_CHUNK
    t = qr_ref.shape[0]
    tc = qcr_ref.shape[0]
    ncc, ncl = tc // c, t // c
    lat0 = 2 * SUBLANES + tc
    half = DN_CONV // 2
    n_sq = int(math.log2(c)) - 1

    zeros8 = jnp.zeros((3, SUBLANES, LANES), F32)
    pad_ref[:, 0:SUBLANES, :] = zeros8
    pad_ref[:, SUBLANES + tc:lat0, :] = zeros8
    pad_ref[:, lat0 + t:lat0 + t + SUBLANES, :] = zeros8
    abs_ref[0:tc, :] = abc_ref[...]
    abs_ref[tc:tc + t, :] = ab_ref[...]
    s_ref[...] = jnp.zeros_like(s_ref)

    lane = lax.broadcasted_iota(jnp.int32, (c, LANES), 1)
    row = lax.broadcasted_iota(jnp.int32, (c, LANES), 0)
    col = jnp.bitwise_and(lane, c - 1)
    fwd = lane < c
    incl = (fwd & (row >= col)) | (~fwd & (row <= col))
    strict = (fwd & (row > col)) | (~fwd & (row < col))
    diag = row == col
    eye2 = jnp.where(diag, 1.0, 0.0).astype(F32)
    fwd16 = jnp.where(fwd, 1.0, 0.0).astype(BF16)
    bwd16 = jnp.where(fwd, 0.0, 1.0).astype(BF16)
    ri = lax.broadcasted_iota(jnp.int32, (2 * c, c), 0)
    ci_ = lax.broadcasted_iota(jnp.int32, (2 * c, c), 1)
    tri2 = jnp.where(((ri < c) & (ci_ <= ri)) | ((ri >= c) & (ci_ >= ri - c)), 1.0, 0.0).astype(BF16)
    tri6 = jnp.concatenate([tri2, tri2, tri2], axis=1)
    neg_a = -jnp.exp(alog_ref[...])
    dtb = dtb_ref[...]

    def l2n(x):
        return x * lax.rsqrt(jnp.sum(x * x, axis=-1, keepdims=True) + NORM_EPS)

    def bd(m16):
        return jnp.concatenate([m16 * fwd16, m16 * bwd16], axis=0)

    def mm3(lhs_parts, bh, bl):
        bdh = bd(bh)
        rhs = jnp.concatenate([bdh, bdh, bd(bl)], axis=0)
        lhs = jnp.concatenate([jnp.concatenate([ah, al, ah], axis=1) for ah, al in lhs_parts], axis=0)
        return _dot(lhs, rhs)

    def prep_head(hh):
        hid = pl.program_id(1) * hg + hh
        hs = slice(hh * LANES, (hh + 1) * LANES)
        for s, (cr, lr) in enumerate(((qcr_ref, qr_ref), (kcr_ref, kr_ref), (vcr_ref, vr_ref))):
            pad_ref[s, SUBLANES:SUBLANES + tc, :] = cr[:, hs]
            pad_ref[s, lat0:lat0 + t, :] = lr[:, hs]

        def conv(s, cw_ref, base):
            acc = pad_ref[s, pl.ds(base - half, c), :] * cw_ref[0:1, hs]
            for j in range(1, DN_CONV):
                acc += pad_ref[s, pl.ds(base - half + j, c), :] * cw_ref[j:j + 1, hs]
            return _silu(acc)

        def pick(x, idx):
            return jnp.broadcast_to(jnp.sum(jnp.where(lane == idx, x, 0.0), axis=-1, keepdims=True), (c, LANES))

        def stage_inputs(ch):
            base = ch * c + jnp.where(ch < ncc, SUBLANES, 2 * SUBLANES)
            q = l2n(conv(0, cwq_ref, base))
            k = l2n(conv(1, cwk_ref, base))
            v = conv(2, cwv_ref, base)
            abt = abs_ref[pl.ds(pl.multiple_of(ch * c, c), c), :]
            g_all = neg_a * _softplus(abt + dtb)
            b_all = _sigmoid(abt)
            return q * DN_DK ** -0.5, k, v, g_all, b_all

        def stage_decay(g_all):
            parts = []
            for gp in _split3(g_all):
                gp = gp.astype(F32)
                pf = pick(gp, hid)
                pb = pick(gp, DN_HEADS + hid)
                parts.append(jnp.concatenate([jnp.where(strict, jnp.where(fwd, pf, pb), 0.0), pf, pb], axis=1).astype(BF16))
            res = _dot(tri6, jnp.concatenate(parts, axis=0))
            return jnp.where(fwd, res[:c, 0:LANES], res[c:, 0:LANES]), res[:c, LANES:2 * LANES], res[c:, 2 * LANES:]

        def stage_rhs(qs, k, v, gc_f, gc_b, bt_f, bt_b):
            e_f = jnp.exp(gc_f)
            e_b = jnp.exp(gc_b)
            rhs = jnp.concatenate([
                jnp.concatenate([v * bt_f, k * (bt_f * e_f)], axis=1),
                jnp.concatenate([v * bt_b, k * (bt_b * e_b)], axis=1)], axis=0).astype(BF16)
            gl_f = gc_f[c - 1:c, :]
            gl_b = gc_b[0:1, :]
            kd = jnp.concatenate([k * jnp.exp(gl_f - gc_f), k * jnp.exp(gl_b - gc_b)], axis=0).astype(BF16)
            return rhs, kd, (qs * e_f, qs * e_b), (jnp.exp(gl_f), jnp.exp(gl_b))

        def stage_solve(p, rhs):
            sol16 = _dot(bd(p.astype(BF16)), rhs).astype(BF16)
            z16 = jnp.zeros((c, 2 * DN_DV), BF16)
            return jnp.concatenate([jnp.concatenate([sol16[:c], z16], axis=1),
                                    jnp.concatenate([z16, sol16[c:]], axis=1)], axis=0)

        def stage_store(ch, qsol, ksol, q_dec, egl):
            rows = pl.ds(pl.multiple_of(ch * c, c), c)
            for d in range(2):
                lo = 2 * DN_DV * d
                aq_ref[hh, d, ch, 0:DN_DK, :] = ksol[:, lo + DN_DV:lo + 2 * DN_DV].astype(BF16)
                aq_ref[hh, d, ch, DN_DK:DN_DK + c, :] = (q_dec[d] - qsol[:, lo + DN_DV:lo + 2 * DN_DV]).astype(BF16)
                b_ref[hh, d, ch] = ksol[:, lo:lo + DN_DV]
                o_ref[hh, d, rows, :] = qsol[:, lo:lo + DN_DV]
                gl_ref[hh, d, ch] = jnp.broadcast_to(egl[d], (SUBLANES, LANES))

        def body(i, carry):
            chs = [i * DN_PREP_CHUNKS + j for j in range(DN_PREP_CHUNKS)]
            ins = [stage_inputs(ch) for ch in chs]
            dec = [stage_decay(x[3]) for x in ins]
            bts = [(pick(x[4], 2 * DN_HEADS + hid), pick(x[4], 3 * DN_HEADS + hid)) for x in ins]
            decay = [jnp.where(incl, jnp.exp(jnp.where(incl, d[0], 0.0)), 0.0) for d in dec]
            k16 = [x[1].astype(BF16) for x in ins]
            kk = [jnp.concatenate([a, a], axis=0) for a in k16]
            qk16 = [(_dot_nt(x[0].astype(BF16), b) * dc).astype(BF16) for x, b, dc in zip(ins, kk, decay)]
            lmat = [jnp.where(strict, jnp.where(fwd, bt[0], bt[1]) * _dot_nt(a, b) * dc, 0.0)
                    for a, b, bt, dc in zip(k16, kk, bts, decay)]
            lsp = [_split2(x) for x in lmat]
            m = [mm3([s], s[0], s[1]) for s in lsp]
            p = [eye2 - x for x in lmat]
            for s in range(n_sq):
                msp = [_split2(x) for x in m]
                psp = [_split2(x) for x in p]
                if s < n_sq - 1:
                    res = [mm3([a, b], b[0], b[1]) for a, b in zip(psp, msp)]
                    p = [a + r[:c] for a, r in zip(p, res)]
                    m = [r[c:] for r in res]
                else:
                    p = [a + mm3([b], e[0], e[1]) for a, b, e in zip(p, psp, msp)]
            fin = [stage_rhs(x[0], x[1], x[2], d[1], d[2], *bt) for x, d, bt in zip(ins, dec, bts)]
            both = [stage_solve(a, f[0]) for a, f in zip(p, fin)]
            qsol = [_dot(a, b) for a, b in zip(qk16, both)]
            ksol = [_dot_tn(f[1], b) for f, b in zip(fin, both)]
            for ch, a, b, f in zip(chs, qsol, ksol, fin):
                stage_store(ch, a, b, f[2], f[3])
            return carry

        lax.fori_loop(0, (ncc + ncl) // DN_PREP_CHUNKS, body, 0)

    for hh in range(hg):
        prep_head(hh)

    def scan(first, n):
        def body(i, carry):
            chains = [(hh, d, first + i if d == 0 else first + n - 1 - i) for hh in range(hg) for d in range(2)]
            s_mats = [s_ref[hh, d] for hh, d, _ in chains]
            res = [_dot(aq_ref[hh, d, ch], s.astype(BF16)) for s, (hh, d, ch) in zip(s_mats, chains)]
            for s, r, (hh, d, ch) in zip(s_mats, res, chains):
                s_ref[hh, d] = s * gl_ref[hh, d, ch][0:1, :] + (b_ref[hh, d, ch] - r[:DN_DK])
                rows = pl.ds(pl.multiple_of(ch * c, c), c)
                o_ref[hh, d, rows, :] = o_ref[hh, d, rows, :] + r[DN_DK:]
            return carry
        return body

    lax.fori_loop(0, ncc, scan(0, ncc), 0)
    lax.fori_loop(0, ncl, scan(ncc, ncl), 0)

    def gated_norm(o, z):
        return (o * lax.rsqrt(jnp.mean(o * o, axis=-1, keepdims=True) + NORM_EPS) * nw_ref[...] * _silu(z))

    piece = 4 * c
    for hh in range(hg):
        hs = slice(hh * LANES, (hh + 1) * LANES)
        yc_ref[:, hs] = gated_norm(o_ref[hh, 0, 0:tc, :] + o_ref[hh, 1, 0:tc, :], zc_ref[:, hs]).astype(BF16)
        for j in range(t // piece):
            rows = slice(j * piece, (j + 1) * piece)
            orow = slice(tc + j * piece, tc + (j + 1) * piece)
            y_ref[rows, hs] = gated_norm(o_ref[hh, 0, orow, :] + o_ref[hh, 1, orow, :], z_ref[rows, hs]).astype(BF16)


def _dn_call(dn, z, ab, dnc, zc, abc, conv_w, alog, dtb, norm_w):
    bsz, t, _ = dn.shape
    tc = dnc.shape[1]
    h = DN_HEADS
    hg = DN_HEAD_GROUP
    nch = (t + tc) // DN_CHUNK
    c = DN_CHUNK
    wide = hg * LANES

    def slab(rows, j0):
        return pl.BlockSpec((None, rows, wide), lambda b, g: (b, 0, j0 // hg + g))

    def whole(rows):
        return pl.BlockSpec((None, rows, LANES), lambda b, g: (b, 0, 0))

    def cw(j0):
        return pl.BlockSpec((DN_CONV, wide), lambda b, g: (0, j0 // hg + g))

    vec = pl.BlockSpec((1, LANES), lambda b, g: (0, 0))
    return pl.pallas_call(
        functools.partial(_dn_kernel, hg=hg),
        out_shape=[jax.ShapeDtypeStruct((bsz, t, h * DN_DV), BF16), jax.ShapeDtypeStruct((bsz, tc, h * DN_DV), BF16)],
        grid=(bsz, h // hg),
        in_specs=[slab(t, 0), slab(t, h), slab(t, 2 * h), slab(t, 0), whole(t),
                  slab(tc, 0), slab(tc, h), slab(tc, 2 * h), slab(tc, 0), whole(tc),
                  cw(0), cw(h), cw(2 * h), vec, vec, vec],
        out_specs=[slab(t, 0), slab(tc, 0)],
        scratch_shapes=[
            pltpu.VMEM((3, 3 * SUBLANES + tc + t, LANES), F32),
            pltpu.VMEM((tc + t, LANES), F32),
            pltpu.VMEM((hg, 2, nch, DN_DK + c, DN_DV), BF16),
            pltpu.VMEM((hg, 2, nch, DN_DK, DN_DV), F32),
            pltpu.VMEM((hg, 2, nch, SUBLANES, LANES), F32),
            pltpu.VMEM((hg, 2, tc + t, DN_DV), F32),
            pltpu.VMEM((hg, 2, DN_DK, DN_DV), F32),
        ],
        compiler_params=_params("parallel", "parallel"),
        name="deltanet",
    )(dn, dn, dn, z, ab, dnc, dnc, dnc, zc, abc, conv_w, conv_w, conv_w, alog, dtb, norm_w)


def _swa_kernel(sink_ref, q_ref, k_ref, v_ref, kc_ref, vc_ref, o_ref, *, n_blocks):
    blk = SWA_BLOCK
    n_kv_pairs = k_ref.shape[1] // LANES
    slabs = q_ref.shape[1] // LANES // n_kv_pairs
    rows = slabs * blk
    low = lax.broadcasted_iota(jnp.int32, (rows, LANES), 1) < SWA_HD
    low_blk = lax.broadcasted_iota(jnp.int32, (blk, LANES), 1) < SWA_HD
    low1 = lax.broadcasted_iota(jnp.int32, (1, LANES), 1) < SWA_HD
    low16 = jnp.where(low1, 1.0, 0.0).astype(BF16)
    high16 = jnp.where(low1, 0.0, 1.0).astype(BF16)
    qi = lax.broadcasted_iota(jnp.int32, (rows, blk), 0) % blk
    kj = lax.broadcasted_iota(jnp.int32, (rows, blk), 1)

    def block_body(n, carry):
        lo = jnp.maximum(n - 1, 0)
        hi = jnp.minimum(n + 1, n_blocks - 1)
        ok_lo = (kj >= qi) & (n > 0)
        ok_hi = (kj <= qi) & (n < n_blocks - 1)

        def tok(i):
            return pl.ds(pl.multiple_of(i * blk, blk), blk)

        def kv_rows(ref, ctx_ref, kvl):
            return jnp.concatenate([ref[tok(lo), kvl], ref[tok(n), kvl], ref[tok(hi), kvl], ctx_ref[:, kvl]], axis=0)

        def softmax(s, p, half):
            cols = [s[:, j * blk:(j + 1) * blk] for j in range(s.shape[1] // blk)]
            cols[0] = jnp.where(ok_lo, cols[0], NEG_INF)
            cols[2] = jnp.where(ok_hi, cols[2], NEG_INF)
            probs, e_sink = [], []
            for j in range(slabs):
                cj = [c[j * blk:(j + 1) * blk] for c in cols]
                sink = sink_ref[(2 * p + half) * slabs + j] * LOG2E
                m = jnp.maximum(jnp.max(functools.reduce(jnp.maximum, cj), axis=-1, keepdims=True), sink)
                probs.append(jnp.concatenate([jnp.exp2(c - m).astype(BF16) for c in cj], axis=1))
                e_sink.append(jnp.exp2(sink - m))
            return jnp.concatenate(probs, axis=0), e_sink

        units = [(p, half) for p in range(n_kv_pairs) for half in range(2)]
        kvls = [slice(p * LANES, (p + 1) * LANES) for p in range(n_kv_pairs)]
        k_all = [kv_rows(k_ref, kc_ref, kvl) for kvl in kvls]
        v_all = [kv_rows(v_ref, vc_ref, kvl) for kvl in kvls]
        qs = [jnp.concatenate([q_ref[tok(n), (p * slabs + j) * LANES:(p * slabs + j + 1) * LANES]
                               for j in range(slabs)], axis=0) for p in range(n_kv_pairs)]
        s = [_dot_nt(qs[p] * (high16 if half else low16), k_all[p]) for p, half in units]
        pe = [softmax(x, p, half) for x, (p, half) in zip(s, units)]
        r = [_dot(x[0], v_all[p] * (high16 if half else low16) + (low16 if half else high16))
             for x, (p, half) in zip(pe, units)]
        res = [jnp.concatenate(
            [y[j * blk:(j + 1) * blk] + jnp.where(low_blk if half else ~low_blk, x[1][j], 0.0) for j in range(slabs)],
            axis=0) for y, x, (p, half) in zip(r, pe, units)]
        for p in range(n_kv_pairs):
            num = jnp.where(low, res[2 * p], res[2 * p + 1])
            den = pltpu.roll(jnp.where(low, res[2 * p + 1], res[2 * p]), SWA_HD, 1)
            o = (num / den).astype(BF16)
            for j in range(slabs):
                o_ref[tok(n), (p * slabs + j) * LANES:(p * slabs + j + 1) * LANES] = o[j * blk:(j + 1) * blk]
        return carry

    lax.fori_loop(0, n_blocks, block_body, 0)


def _swa_call(sink, q, k, v, kc, vc):
    bsz, t, wq = q.shape
    wk = k.shape[2]
    tc = kc.shape[1]
    return pl.pallas_call(
        functools.partial(_swa_kernel, n_blocks=t // SWA_BLOCK),
        out_shape=jax.ShapeDtypeStruct((bsz, t, wq), BF16),
        grid=(bsz,),
        in_specs=[
            pl.BlockSpec(memory_space=pltpu.SMEM),
            pl.BlockSpec((None, t, wq), lambda b: (b, 0, 0)),
            pl.BlockSpec((None, t, wk), lambda b: (b, 0, 0)),
            pl.BlockSpec((None, t, wk), lambda b: (b, 0, 0)),
            pl.BlockSpec((None, tc, wk), lambda b: (b, 0, 0)),
            pl.BlockSpec((None, tc, wk), lambda b: (b, 0, 0)),
        ],
        out_specs=pl.BlockSpec((None, t, wq), lambda b: (b, 0, 0)),
        compiler_params=_params("parallel"),
        name="swa_attn",
    )(sink, q, k, v, kc, vc)


def _swa_slab_order(w, axis):
    rep = SWA_HEADS // SWA_KV_HEADS
    shape = w.shape
    split = shape[:axis] + (SWA_KV_HEADS // 2, 2, rep, SWA_HD) + shape[axis + 1:]
    return jnp.swapaxes(w.reshape(split), axis + 1, axis + 2).reshape(shape)


def _rope_tables(t):
    pos = jnp.arange(t)
    half = SWA_HD // 2
    inv = jnp.power(ROPE_BASE, -jnp.arange(0, half, 2, dtype=F32) / half)
    ang_r = (pos // GRID_W).astype(F32)[:, None] * inv
    ang_c = (pos % GRID_W).astype(F32)[:, None] * inv
    cos = jnp.concatenate([jnp.cos(ang_r)] * 2 + [jnp.cos(ang_c)] * 2, axis=-1)
    sin = jnp.concatenate([-jnp.sin(ang_r), jnp.sin(ang_r), -jnp.sin(ang_c), jnp.sin(ang_c)], axis=-1)
    return jnp.tile(cos, (1, LANES // SWA_HD)), jnp.tile(sin, (1, LANES // SWA_HD))


def kernel(x, c, ctx, c_ctx, ada_w, ada_b, norm_g, ffn_w_up, ffn_w_down, even_w_in, even_w_out, na_rpb,
           dn_conv_w, dn_a_log, dn_dt_bias, dn_norm_w, odd_w_in, odd_w_out, swa_sink, final_norm_g):
    bsz, t, d = x.shape
    tc = ctx.shape[1]
    depth = ada_w.shape[0]
    ctx_row = bsz

    c16 = jnp.concatenate([c, c_ctx[None, :], jnp.zeros((MOD_ROWS - bsz - 1, d), F32)], axis=0)
    mods = _adaln_call(c16, ada_w, ada_b).reshape(depth, N_MOD, MOD_ROWS, 1, d)
    norm_g3 = norm_g.reshape(depth * 3, 1, d)

    w_up = ffn_w_up.astype(BF16)
    f = ffn_w_down.shape[2]
    w_down = ffn_w_down.astype(BF16).reshape(depth, 2, f // FF_CHUNK, FF_CHUNK, d)

    h = x
    hc = ctx.reshape(1, bsz * tc, d)
    for i in range(depth):
        need_ctx = i < depth - 1
        j = i // 2
        h = _ffn_call(h, mods, norm_g3, i, 0, None, w_up, w_down)
        hc = _ffn_call(hc, mods, norm_g3, i, 0, ctx_row, w_up, w_down)
        mix_c = None
        if i % 2 == 0:
            w_in = even_w_in[j]
            w_in = jnp.pad(w_in, ((0, 0), (0, -w_in.shape[1] % LANES))).astype(BF16)
            na, dn, z, ab = _proj_even_call(h, mods, norm_g3, i, None, w_in)
            nac, dnc, zc, abc = [a.reshape(bsz, tc, a.shape[-1])
                                 for a in _proj_even_call(hc, mods, norm_g3, i, ctx_row, w_in)]
            y_na, y_na_c = _na_call(na, nac, _na_bias_table(na_rpb[j]))
            lane_pad = (0, LANES - 2 * DN_HEADS)
            alog = jnp.pad(dn_a_log[j].reshape(-1), lane_pad)[None, :]
            dtb = jnp.pad(dn_dt_bias[j].reshape(-1), lane_pad)[None, :]
            y_dn, y_dn_c = _dn_call(dn, z, ab, dnc, zc, abc, dn_conv_w[j], alog, dtb, dn_norm_w[j][None, :])
            w_out = even_w_out[j].astype(BF16)
            ws = [w_out[:NA_HEADS * NA_HD], w_out[NA_HEADS * NA_HD:]]
            mix = ([y_na, y_dn], ws)
            if need_ctx:
                mix_c = ([y.reshape(1, bsz * tc, y.shape[-1]) for y in (y_na_c, y_dn_c)], ws)
        else:
            if need_ctx:
                raise NotImplementedError("context queries of a windowed layer are only needed before a later layer")
            w_in = odd_w_in[j]
            qw = SWA_HEADS * SWA_HD
            kw = SWA_KV_HEADS * SWA_HD
            wq = _swa_slab_order(w_in[:, :qw].astype(BF16), 1)
            wk = w_in[:, qw:qw + kw].astype(BF16)
            wv = w_in[:, qw + kw:].astype(BF16)
            cos, sin = _rope_tables(t)
            q, k, v = _proj_odd_call(h, mods, norm_g3, i, None, wq, wk, wv, cos, sin)
            kc, vc = [a.reshape(bsz, tc, kw) for a in _proj_odd_call(hc, mods, norm_g3, i, ctx_row, None, wk, wv, None, None)]
            mix = ([_swa_call(swa_sink[j], q, k, v, kc, vc)], [_swa_slab_order(odd_w_out[j].astype(BF16), 0)])
        last = i == depth - 1
        h = _ffn_call(h, mods, norm_g3, i, 1, None, w_up, w_down, mix=mix,
                      final_g=final_norm_g[None, :] if last else None)
        if need_ctx:
            hc = _ffn_call(hc, mods, norm_g3, i, 1, ctx_row, w_up, w_down, mix=mix_c)
    return h
```

```python
import functools
import math

import jax
import jax.numpy as jnp
from jax import lax
from jax.experimental import pallas as pl
from jax.experimental.pallas import tpu as pltpu

F32 = jnp.float32
BF16 = jnp.bfloat16

GRID_W = 64
N_MOD = 9
NORM_EPS = 1e-6
NEG_INF = -1e30
LOG2E = math.log2(math.e)
ROPE_BASE = 10000.0
NA_HEADS = 8
NA_HD = 64
NA_WIN_H = 8
NA_WIN_W = 16
NA_PROJ = 3 * NA_HEADS * NA_HD
DN_HEADS = 4
DN_DK = 128
DN_DV = 128
DN_CONV = 5
DN_CHUNK = 64
DN_QKV = DN_HEADS * (2 * DN_DK + DN_DV)
SWA_HEADS = 16
SWA_KV_HEADS = 4
SWA_HD = 64
SWA_BLOCK = 128

LANES = 128
SUBLANES = 8
VMEM_LIMIT = 56 * 1024 * 1024

TOK_TILE = 1024
FF_CHUNK = 256
MOD_ROWS = 16
DN_HEAD_GROUP = 2
NA_ROWS_PER_STEP = 2
DN_PREP_CHUNKS = 9


def _params(*sem):
    return pltpu.CompilerParams(dimension_semantics=sem, vmem_limit_bytes=VMEM_LIMIT)


def _dot(a, b):
    return jnp.dot(a, b, preferred_element_type=F32)


def _dot_nt(a, b):
    return lax.dot_general(a, b, (((1,), (1,)), ((), ())), preferred_element_type=F32)


def _dot_tn(a, b):
    return lax.dot_general(a, b, (((0,), (0,)), ((), ())), preferred_element_type=F32)


def _sigmoid(x):
    return 0.5 + 0.5 * jnp.tanh(0.5 * x)


def _silu(x):
    h = 0.5 * x
    return h + h * jnp.tanh(h)


def _softplus(x):
    return jnp.maximum(x, 0.0) + jnp.log1p(jnp.exp(-jnp.abs(x)))


def _modulate(x, g, shift, scale):
    y = x * lax.rsqrt(jnp.mean(x * x, axis=-1, keepdims=True) + NORM_EPS) * g
    return y * (1.0 + scale) + shift


def _split3(x):
    h1 = x.astype(BF16)
    r1 = x - h1.astype(F32)
    h2 = r1.astype(BF16)
    h3 = (r1 - h2.astype(F32)).astype(BF16)
    return h1, h2, h3


def _split2(x):
    hi = x.astype(BF16)
    return hi, (x - hi.astype(F32)).astype(BF16)


def _adaln_kernel(c_ref, w_ref, b_ref, o_ref):
    s = _silu(c_ref[...]).astype(BF16)
    o_ref[...] = _dot(s, w_ref[...].astype(BF16)) + b_ref[...]


def _adaln_call(c16, ada_w, ada_b):
    depth, d, _ = ada_w.shape
    b4 = ada_b.reshape(depth * N_MOD, 1, d)
    return pl.pallas_call(
        _adaln_kernel,
        out_shape=jax.ShapeDtypeStruct((depth, N_MOD, MOD_ROWS, d), F32),
        grid=(depth, N_MOD),
        in_specs=[
            pl.BlockSpec((MOD_ROWS, d), lambda i, k: (0, 0)),
            pl.BlockSpec((None, d, d), lambda i, k: (i, 0, k)),
            pl.BlockSpec((None, 1, d), lambda i, k: (i * N_MOD + k, 0, 0)),
        ],
        out_specs=pl.BlockSpec((None, None, MOD_ROWS, d), lambda i, k: (i, k, 0, 0)),
        compiler_params=_params("arbitrary", "arbitrary"),
        name="adaln",
    )(c16, ada_w, b4)


def _mod_spec(layer, k, ctx_row, d):
    if ctx_row is None:
        return pl.BlockSpec((None, None, None, 1, d), lambda b, t: (layer, k, b, 0, 0))
    return pl.BlockSpec((None, None, None, 1, d), lambda b, t: (layer, k, ctx_row, 0, 0))


def _const_spec(shape):
    nd = len(shape)
    return pl.BlockSpec(shape, lambda b, t: (0,) * nd, pipeline_mode=pl.Buffered(1))


def _ffn_kernel(x_ref, g_ref, sh_ref, sc_ref, gt_ref, wup_ref, wd_ref, *rest, n_mix, final):
    if n_mix:
        mg_ref, rest = rest[0], rest[1:]
        y_refs, w_refs, rest = rest[:n_mix], rest[n_mix:2 * n_mix], rest[2 * n_mix:]
    if final:
        fg_ref, rest = rest[0], rest[1:]
    o_ref, u_ref, acc_ref = rest
    n_chunks, fc, _ = wd_ref.shape
    f = n_chunks * fc

    x = x_ref[...]
    if n_mix:
        mix = _dot(y_refs[0][...], w_refs[0][...])
        for y_ref, w_ref in zip(y_refs[1:], w_refs[1:]):
            mix += _dot(y_ref[...], w_ref[...])
        x = x + mg_ref[...] * mix
        o_ref[...] = x
    u_ref[...] = _modulate(x, g_ref[...], sh_ref[...], sc_ref[...]).astype(BF16)
    acc_ref[...] = jnp.zeros_like(acc_ref)

    def body(c, carry):
        u = u_ref[...]
        off = pl.multiple_of(c * fc, fc)
        gate = _dot(u, wup_ref[:, pl.ds(off, fc)])
        up = _dot(u, wup_ref[:, pl.ds(f + off, fc)])
        a = (_silu(gate) * up).astype(BF16)
        acc_ref[...] += _dot(a, wd_ref[c])
        return carry

    lax.fori_loop(0, n_chunks, body, 0, unroll=True)
    y = (o_ref[...] if n_mix else x_ref[...]) + 0.5 * gt_ref[...] * acc_ref[...]
    if final:
        y = y * lax.rsqrt(jnp.mean(y * y, axis=-1, keepdims=True) + NORM_EPS) * fg_ref[...]
    o_ref[...] = y


def _ffn_call(h, mods, norm_g3, layer, which, ctx_row, w_up, w_down, mix=None, final_g=None):
    bsz, t, d = h.shape
    tm = min(TOK_TILE, t)
    k0 = 6 if which else 0
    in_specs = [
        pl.BlockSpec((None, tm, d), lambda b, i: (b, i, 0)),
        pl.BlockSpec((None, 1, d), lambda b, i: (layer * 3 + (2 if which else 0), 0, 0)),
        _mod_spec(layer, k0, ctx_row, d),
        _mod_spec(layer, k0 + 1, ctx_row, d),
        _mod_spec(layer, k0 + 2, ctx_row, d),
        pl.BlockSpec((None, None) + w_up.shape[2:], lambda b, i: (layer, which, 0, 0), pipeline_mode=pl.Buffered(1)),
        pl.BlockSpec((None, None) + w_down.shape[2:], lambda b, i: (layer, which, 0, 0, 0),
                     pipeline_mode=pl.Buffered(1)),
    ]
    args = [h, norm_g3, mods, mods, mods, w_up, w_down]
    n_mix = 0
    if mix is not None:
        ys, ws = mix
        n_mix = len(ys)
        in_specs.append(_mod_spec(layer, 5, ctx_row, d))
        in_specs += [pl.BlockSpec((None, tm, y.shape[2]), lambda b, i: (b, i, 0)) for y in ys]
        in_specs += [_const_spec(w.shape) for w in ws]
        args += [mods, *ys, *ws]
    if final_g is not None:
        in_specs.append(pl.BlockSpec((1, d), lambda b, i: (0, 0)))
        args.append(final_g)
    return pl.pallas_call(
        functools.partial(_ffn_kernel, n_mix=n_mix, final=final_g is not None),
        out_shape=jax.ShapeDtypeStruct(h.shape, F32),
        grid=(bsz, t // tm),
        in_specs=in_specs,
        out_specs=pl.BlockSpec((None, tm, d), lambda b, i: (b, i, 0)),
        scratch_shapes=[pltpu.VMEM((tm, d), BF16), pltpu.VMEM((tm, d), F32)],
        compiler_params=_params("parallel", "parallel"),
        name="ffn",
    )(*args)


def _proj_even_kernel(x_ref, g_ref, sh_ref, sc_ref, w_ref, na_ref, dn_ref, z_ref, ab_ref):
    u = _modulate(x_ref[...], g_ref[...], sh_ref[...], sc_ref[...]).astype(BF16)
    c0 = 0
    for o_ref in (na_ref, dn_ref, z_ref, ab_ref):
        c1 = c0 + o_ref.shape[1]
        y = _dot(u, w_ref[:, c0:c1])
        if o_ref is na_ref:
            qw = NA_HEADS * NA_HD
            o_ref[:, :qw] = (y[:, :qw] * (NA_HD ** -0.5 * LOG2E)).astype(BF16)
            o_ref[:, qw:] = y[:, qw:].astype(BF16)
        else:
            o_ref[...] = y
        c0 = c1


def _proj_even_call(h, mods, norm_g3, layer, ctx_row, w_in):
    bsz, t, d = h.shape
    tm = min(TOK_TILE, t)
    widths = (NA_PROJ, DN_QKV, DN_HEADS * DN_DV, LANES)
    dtypes = (BF16, F32, F32, F32)
    return pl.pallas_call(
        _proj_even_kernel,
        out_shape=[jax.ShapeDtypeStruct((bsz, t, w), dt) for w, dt in zip(widths, dtypes)],
        grid=(bsz, t // tm),
        in_specs=[
            pl.BlockSpec((None, tm, d), lambda b, i: (b, i, 0)),
            pl.BlockSpec((None, 1, d), lambda b, i: (layer * 3 + 1, 0, 0)),
            _mod_spec(layer, 3, ctx_row, d),
            _mod_spec(layer, 4, ctx_row, d),
            _const_spec(w_in.shape),
        ],
        out_specs=[pl.BlockSpec((None, tm, w), lambda b, i: (b, i, 0)) for w in widths],
        compiler_params=_params("parallel", "parallel"),
        name="proj_even",
    )(h, norm_g3, mods, mods, w_in)


def _rope_slab(x, cos, sin, first):
    swapped = jnp.where(first, pltpu.roll(x, LANES - 16, 1), pltpu.roll(x, 16, 1))
    return x * cos + swapped * sin


def _proj_odd_kernel(x_ref, g_ref, sh_ref, sc_ref, *rest, with_q):
    if with_q:
        wq_ref, wk_ref, wv_ref, cos_ref, sin_ref, q_ref, k_ref, v_ref = rest
    else:
        wk_ref, wv_ref, k_ref, v_ref = rest
    u = _modulate(x_ref[...], g_ref[...], sh_ref[...], sc_ref[...]).astype(BF16)
    v_ref[...] = _dot(u, wv_ref[...]).astype(BF16)
    k = _dot(u, wk_ref[...])
    if not with_q:
        k_ref[...] = k.astype(BF16)
        return
    cos = cos_ref[...]
    sin = sin_ref[...]
    lane = lax.broadcasted_iota(jnp.int32, cos.shape, 1)
    first = (lane % 32) < 16
    for j in range(k.shape[1] // LANES):
        sl = slice(j * LANES, (j + 1) * LANES)
        k_ref[:, sl] = _rope_slab(k[:, sl], cos, sin, first).astype(BF16)
    q = _dot(u, wq_ref[...])
    for j in range(q.shape[1] // LANES):
        sl = slice(j * LANES, (j + 1) * LANES)
        q_ref[:, sl] = (_rope_slab(q[:, sl], cos, sin, first) * (SWA_HD ** -0.5 * LOG2E)).astype(BF16)


def _proj_odd_call(h, mods, norm_g3, layer, ctx_row, wq, wk, wv, cos, sin):
    bsz, t, d = h.shape
    tm = min(TOK_TILE, t)
    with_q = wq is not None
    in_specs = [
        pl.BlockSpec((None, tm, d), lambda b, i: (b, i, 0)),
        pl.BlockSpec((None, 1, d), lambda b, i: (layer * 3 + 1, 0, 0)),
        _mod_spec(layer, 3, ctx_row, d),
        _mod_spec(layer, 4, ctx_row, d),
    ]
    args = [h, norm_g3, mods, mods]
    widths = []
    if with_q:
        in_specs.append(_const_spec(wq.shape))
        args.append(wq)
        widths.append(wq.shape[1])
    in_specs += [_const_spec(wk.shape), _const_spec(wv.shape)]
    args += [wk, wv]
    widths += [wk.shape[1], wv.shape[1]]
    if with_q:
        in_specs += [pl.BlockSpec((tm, LANES), lambda b, i: (i, 0))] * 2
        args += [cos, sin]
    return pl.pallas_call(
        functools.partial(_proj_odd_kernel, with_q=with_q),
        out_shape=[jax.ShapeDtypeStruct((bsz, t, w), BF16) for w in widths],
        grid=(bsz, t // tm),
        in_specs=in_specs,
        out_specs=[pl.BlockSpec((None, tm, w), lambda b, i: (b, i, 0)) for w in widths],
        compiler_params=_params("parallel", "parallel"),
        name="proj_odd",
    )(*args)


def _softmax2(cols):
    m = jnp.max(functools.reduce(jnp.maximum, cols), axis=-1, keepdims=True)
    p = [jnp.exp2(c - m) for c in cols]
    den = jnp.sum(functools.reduce(jnp.add, p), axis=-1, keepdims=True)
    return jnp.concatenate([x.astype(BF16) for x in p], axis=1), den


def _na_kernel(q_ref, k_ref, v_ref, qc_ref, kc_ref, vc_ref, bias_ref, o_ref, oc_ref, *, rows):
    n_pairs = q_ref.shape[1] // LANES
    tc = qc_ref.shape[0]
    win = NA_WIN_H * GRID_W
    n_bias = win // LANES
    low1 = lax.broadcasted_iota(jnp.int32, (1, LANES), 1) < NA_HD
    low16 = jnp.where(low1, 1.0, 0.0).astype(BF16)
    high16 = jnp.where(low1, 0.0, 1.0).astype(BF16)
    low_q = lax.broadcasted_iota(jnp.int32, (GRID_W, LANES), 1) < NA_HD
    low_c = lax.broadcasted_iota(jnp.int32, (tc, LANES), 1) < NA_HD
    pairs = [slice(hp * LANES, (hp + 1) * LANES) for hp in range(n_pairs)]

    def row_body(i, carry):
        units = []
        for rr in range(NA_ROWS_PER_STEP):
            r = i * NA_ROWS_PER_STEP + rr
            r0 = jnp.clip(r - NA_WIN_H // 2, 0, rows - NA_WIN_H)
            doff = r0 - r + NA_WIN_H - 1
            qs = pl.ds(pl.multiple_of(r * GRID_W, GRID_W), GRID_W)
            ks = pl.ds(pl.multiple_of(r0 * GRID_W, GRID_W), win)
            units += [(hp, sl, doff, qs, ks) for hp, sl in enumerate(pairs)]
        qm = [jnp.concatenate([q_ref[qs, sl] * low16, q_ref[qs, sl] * high16], axis=0) for _, sl, _, qs, _ in units]
        s = [_dot_nt(x, jnp.concatenate([k_ref[ks, sl], kc_ref[:, sl]], axis=0))
             for x, (_, sl, _, _, ks) in zip(qm, units)]
        cols = [[x[:, j * LANES:(j + 1) * LANES] + bias_ref[hp, doff + 2 * j] for j in range(n_bias)]
                + [x[:, j * LANES:(j + 1) * LANES] for j in range(n_bias, x.shape[1] // LANES)]
                for x, (hp, _, doff, _, _) in zip(s, units)]
        pd = [_softmax2(c) for c in cols]
        o = [_dot(p, jnp.concatenate([v_ref[ks, sl], vc_ref[:, sl]], axis=0)) / den
             for (p, den), (_, sl, _, _, ks) in zip(pd, units)]
        for x, (_, sl, _, qs, _) in zip(o, units):
            o_ref[qs, sl] = jnp.where(low_q, x[:GRID_W], x[GRID_W:]).astype(BF16)
        return carry

    lax.fori_loop(0, rows // NA_ROWS_PER_STEP, row_body, 0)

    for sl in pairs:
        halves = []
        for keep16 in (low16, high16):
            s = _dot_nt(qc_ref[:, sl] * keep16, kc_ref[:, sl])
            p, den = _softmax2([s[:, j * LANES:(j + 1) * LANES] for j in range(tc // LANES)])
            halves.append(_dot(p, vc_ref[:, sl]) / den)
        oc_ref[:, sl] = jnp.where(low_c, halves[0], halves[1]).astype(BF16)


def _na_call(qkv, qkvc, bias):
    bsz, t, w3 = qkv.shape
    tc = qkvc.shape[1]
    w = w3 // 3

    def col(j):
        return lambda b: (b, 0, j)

    return pl.pallas_call(
        functools.partial(_na_kernel, rows=t // GRID_W),
        out_shape=[jax.ShapeDtypeStruct((bsz, t, w), BF16), jax.ShapeDtypeStruct((bsz, tc, w), BF16)],
        grid=(bsz,),
        in_specs=[pl.BlockSpec((None, t, w), col(j)) for j in range(3)]
        + [pl.BlockSpec((None, tc, w), col(j)) for j in range(3)]
        + [pl.BlockSpec(bias.shape, lambda b: (0, 0, 0, 0), pipeline_mode=pl.Buffered(1))],
        out_specs=[pl.BlockSpec((None, t, w), lambda b: (b, 0, 0)), pl.BlockSpec((None, tc, w), lambda b: (b, 0, 0))],
        compiler_params=_params("parallel"),
        name="na_attn",
    )(qkv, qkv, qkv, qkvc, qkvc, qkvc, bias)


def _na_bias_table(rpb):
    c = jnp.arange(GRID_W)[:, None]
    kc = jnp.arange(GRID_W)[None, :]
    cstart = jnp.clip(c - NA_WIN_W // 2, 0, GRID_W - NA_WIN_W)
    ok = (kc >= cstart) & (kc < cstart + NA_WIN_W)
    dc = jnp.clip(kc - c + NA_WIN_W - 1, 0, 2 * NA_WIN_W - 2)
    onehot = (dc[None] == jnp.arange(2 * NA_WIN_W - 1)[:, None, None]).astype(F32)
    h = rpb.shape[0]
    nd = 2 * NA_WIN_H - 2
    rp = (rpb.astype(F32) * LOG2E).reshape(h // 2, 2, 2 * NA_WIN_H - 1, 2 * NA_WIN_W - 1)
    t = jnp.einsum('phdx,xck->pdhck', rp, onehot, precision=lax.Precision.HIGHEST)
    t = jnp.where(ok, t, NEG_INF).reshape(h // 2, 2 * NA_WIN_H - 1, 2 * GRID_W, GRID_W)
    return jnp.concatenate([t[:, :nd], t[:, 1:nd + 1]], axis=-1)


def _dn_kernel(qr_ref, kr_ref, vr_ref, z_ref, ab_ref, qcr_ref, kcr_ref, vcr_ref, zc_ref, abc_ref,
               cwq_ref, cwk_ref, cwv_ref, alog_ref, dtb_ref, nw_ref,
               y_ref, yc_ref,
               pad_ref, abs_ref, aq_ref, b_ref, gl_ref, o_ref, s_ref, *, hg):
    c = DN_CHUNK
    t = qr_ref.shape[0]
    tc = qcr_ref.shape[0]
    ncc, ncl = tc // c, t // c
    lat0 = 2 * SUBLANES + tc
    half = DN_CONV // 2
    n_sq = int(math.log2(c)) - 1

    zeros8 = jnp.zeros((3, SUBLANES, LANES), F32)
    pad_ref[:, 0:SUBLANES, :] = zeros8
    pad_ref[:, SUBLANES + tc:lat0, :] = zeros8
    pad_ref[:, lat0 + t:lat0 + t + SUBLANES, :] = zeros8
    abs_ref[0:tc, :] = abc_ref[...]
    abs_ref[tc:tc + t, :] = ab_ref[...]
    s_ref[...] = jnp.zeros_like(s_ref)

    lane = lax.broadcasted_iota(jnp.int32, (c, LANES), 1)
    row = lax.broadcasted_iota(jnp.int32, (c, LANES), 0)
    col = jnp.bitwise_and(lane, c - 1)
    fwd = lane < c
    incl = (fwd & (row >= col)) | (~fwd & (row <= col))
    strict = (fwd & (row > col)) | (~fwd & (row < col))
    diag = row == col
    eye2 = jnp.where(diag, 1.0, 0.0).astype(F32)
    fwd16 = jnp.where(fwd, 1.0, 0.0).astype(BF16)
    bwd16 = jnp.where(fwd, 0.0, 1.0).astype(BF16)
    ri = lax.broadcasted_iota(jnp.int32, (2 * c, c), 0)
    ci_ = lax.broadcasted_iota(jnp.int32, (2 * c, c), 1)
    tri2 = jnp.where(((ri < c) & (ci_ <= ri)) | ((ri >= c) & (ci_ >= ri - c)), 1.0, 0.0).astype(BF16)
    tri6 = jnp.concatenate([tri2, tri2, tri2], axis=1)
    neg_a = -jnp.exp(alog_ref[...])
    dtb = dtb_ref[...]

    def l2n(x):
        return x * lax.rsqrt(jnp.sum(x * x, axis=-1, keepdims=True) + NORM_EPS)

    def bd(m16):
        return jnp.concatenate([m16 * fwd16, m16 * bwd16], axis=0)

    def mm3(lhs_parts, bh, bl):
        bdh = bd(bh)
        rhs = jnp.concatenate([bdh, bdh, bd(bl)], axis=0)
        lhs = jnp.concatenate([jnp.concatenate([ah, al, ah], axis=1) for ah, al in lhs_parts], axis=0)
        return _dot(lhs, rhs)

    def prep_head(hh):
        hid = pl.program_id(1) * hg + hh
        hs = slice(hh * LANES, (hh + 1) * LANES)
        for s, (cr, lr) in enumerate(((qcr_ref, qr_ref), (kcr_ref, kr_ref), (vcr_ref, vr_ref))):
            pad_ref[s, SUBLANES:SUBLANES + tc, :] = cr[:, hs]
            pad_ref[s, lat0:lat0 + t, :] = lr[:, hs]

        def conv(s, cw_ref, base):
            acc = pad_ref[s, pl.ds(base - half, c), :] * cw_ref[0:1, hs]
            for j in range(1, DN_CONV):
                acc += pad_ref[s, pl.ds(base - half + j, c), :] * cw_ref[j:j + 1, hs]
            return _silu(acc)

        def pick(x, idx):
            return jnp.broadcast_to(jnp.sum(jnp.where(lane == idx, x, 0.0), axis=-1, keepdims=True), (c, LANES))

        def stage_inputs(ch):
            base = ch * c + jnp.where(ch < ncc, SUBLANES, 2 * SUBLANES)
            q = l2n(conv(0, cwq_ref, base))
            k = l2n(conv(1, cwk_ref, base))
            v = conv(2, cwv_ref, base)
            abt = abs_ref[pl.ds(pl.multiple_of(ch * c, c), c), :]
            g_all = neg_a * _softplus(abt + dtb)
            b_all = _sigmoid(abt)
            return q * DN_DK ** -0.5, k, v, g_all, b_all

        def stage_decay(g_all):
            parts = []
            for gp in _split3(g_all):
                gp = gp.astype(F32)
                pf = pick(gp, hid)
                pb = pick(gp, DN_HEADS + hid)
                parts.append(jnp.concatenate([jnp.where(strict, jnp.where(fwd, pf, pb), 0.0), pf, pb], axis=1).astype(BF16))
            res = _dot(tri6, jnp.concatenate(parts, axis=0))
            return jnp.where(fwd, res[:c, 0:LANES], res[c:, 0:LANES]), res[:c, LANES:2 * LANES], res[c:, 2 * LANES:]

        def stage_rhs(qs, k, v, gc_f, gc_b, bt_f, bt_b):
            e_f = jnp.exp(gc_f)
            e_b = jnp.exp(gc_b)
            rhs = jnp.concatenate([
                jnp.concatenate([v * bt_f, k * (bt_f * e_f)], axis=1),
                jnp.concatenate([v * bt_b, k * (bt_b * e_b)], axis=1)], axis=0).astype(BF16)
            gl_f = gc_f[c - 1:c, :]
            gl_b = gc_b[0:1, :]
            kd = jnp.concatenate([k * jnp.exp(gl_f - gc_f), k * jnp.exp(gl_b - gc_b)], axis=0).astype(BF16)
            return rhs, kd, (qs * e_f, qs * e_b), (jnp.exp(gl_f), jnp.exp(gl_b))

        def stage_solve(p, rhs):
            sol16 = _dot(bd(p.astype(BF16)), rhs).astype(BF16)
            z16 = jnp.zeros((c, 2 * DN_DV), BF16)
            return jnp.concatenate([jnp.concatenate([sol16[:c], z16], axis=1),
                                    jnp.concatenate([z16, sol16[c:]], axis=1)], axis=0)

        def stage_store(ch, qsol, ksol, q_dec, egl):
            rows = pl.ds(pl.multiple_of(ch * c, c), c)
            for d in range(2):
                lo = 2 * DN_DV * d
                aq_ref[hh, d, ch, 0:DN_DK, :] = ksol[:, lo + DN_DV:lo + 2 * DN_DV].astype(BF16)
                aq_ref[hh, d, ch, DN_DK:DN_DK + c, :] = (q_dec[d] - qsol[:, lo + DN_DV:lo + 2 * DN_DV]).astype(BF16)
                b_ref[hh, d, ch] = ksol[:, lo:lo + DN_DV]
                o_ref[hh, d, rows, :] = qsol[:, lo:lo + DN_DV]
                gl_ref[hh, d, ch] = jnp.broadcast_to(egl[d], (SUBLANES, LANES))

        def body(i, carry):
            chs = [i * DN_PREP_CHUNKS + j for j in range(DN_PREP_CHUNKS)]
            ins = [stage_inputs(ch) for ch in chs]
            dec = [stage_decay(x[3]) for x in ins]
            bts = [(pick(x[4], 2 * DN_HEADS + hid), pick(x[4], 3 * DN_HEADS + hid)) for x in ins]
            decay = [jnp.where(incl, jnp.exp(jnp.where(incl, d[0], 0.0)), 0.0) for d in dec]
            k16 = [x[1].astype(BF16) for x in ins]
            kk = [jnp.concatenate([a, a], axis=0) for a in k16]
            qk16 = [(_dot_nt(x[0].astype(BF16), b) * dc).astype(BF16) for x, b, dc in zip(ins, kk, decay)]
            lmat = [jnp.where(strict, jnp.where(fwd, bt[0], bt[1]) * _dot_nt(a, b) * dc, 0.0)
                    for a, b, bt, dc in zip(k16, kk, bts, decay)]
            lsp = [_split2(x) for x in lmat]
            m = [mm3([s], s[0], s[1]) for s in lsp]
            p = [eye2 - x for x in lmat]
            for s in range(n_sq):
                msp = [_split2(x) for x in m]
                psp = [_split2(x) for x in p]
                if s < n_sq - 1:
                    res = [mm3([a, b], b[0], b[1]) for a, b in zip(psp, msp)]
                    p = [a + r[:c] for a, r in zip(p, res)]
                    m = [r[c:] for r in res]
                else:
                    p = [a + mm3([b], e[0], e[1]) for a, b, e in zip(p, psp, msp)]
            fin = [stage_rhs(x[0], x[1], x[2], d[1], d[2], *bt) for x, d, bt in zip(ins, dec, bts)]
            both = [stage_solve(a, f[0]) for a, f in zip(p, fin)]
            qsol = [_dot(a, b) for a, b in zip(qk16, both)]
            ksol = [_dot_tn(f[1], b) for f, b in zip(fin, both)]
            for ch, a, b, f in zip(chs, qsol, ksol, fin):
                stage_store(ch, a, b, f[2], f[3])
            return carry

        lax.fori_loop(0, (ncc + ncl) // DN_PREP_CHUNKS, body, 0)

    for hh in range(hg):
        prep_head(hh)

    def scan(first, n):
        def body(i, carry):
            chains = [(hh, d, first + i if d == 0 else first + n - 1 - i) for hh in range(hg) for d in range(2)]
            s_mats = [s_ref[hh, d] for hh, d, _ in chains]
            res = [_dot(aq_ref[hh, d, ch], s.astype(BF16)) for s, (hh, d, ch) in zip(s_mats, chains)]
            for s, r, (hh, d, ch) in zip(s_mats, res, chains):
                s_ref[hh, d] = s * gl_ref[hh, d, ch][0:1, :] + (b_ref[hh, d, ch] - r[:DN_DK])
                rows = pl.ds(pl.multiple_of(ch * c, c), c)
                o_ref[hh, d, rows, :] = o_ref[hh, d, rows, :] + r[DN_DK:]
            return carry
        return body

    lax.fori_loop(0, ncc, scan(0, ncc), 0)
    lax.fori_loop(0, ncl, scan(ncc, ncl), 0)

    def gated_norm(o, z):
        return (o * lax.rsqrt(jnp.mean(o * o, axis=-1, keepdims=True) + NORM_EPS) * nw_ref[...] * _silu(z))

    piece = 4 * c
    for hh in range(hg):
        hs = slice(hh * LANES, (hh + 1) * LANES)
        yc_ref[:, hs] = gated_norm(o_ref[hh, 0, 0:tc, :] + o_ref[hh, 1, 0:tc, :], zc_ref[:, hs]).astype(BF16)
        for j in range(t // piece):
            rows = slice(j * piece, (j + 1) * piece)
            orow = slice(tc + j * piece, tc + (j + 1) * piece)
            y_ref[rows, hs] = gated_norm(o_ref[hh, 0, orow, :] + o_ref[hh, 1, orow, :], z_ref[rows, hs]).astype(BF16)


def _dn_call(dn, z, ab, dnc, zc, abc, conv_w, alog, dtb, norm_w):
    bsz, t, _ = dn.shape
    tc = dnc.shape[1]
    h = DN_HEADS
    hg = DN_HEAD_GROUP
    nch = (t + tc) // DN_CHUNK
    c = DN_CHUNK
    wide = hg * LANES

    def slab(rows, j0):
        return pl.BlockSpec((None, rows, wide), lambda b, g: (b, 0, j0 // hg + g))

    def whole(rows):
        return pl.BlockSpec((None, rows, LANES), lambda b, g: (b, 0, 0))

    def cw(j0):
        return pl.BlockSpec((DN_CONV, wide), lambda b, g: (0, j0 // hg + g))

    vec = pl.BlockSpec((1, LANES), lambda b, g: (0, 0))
    return pl.pallas_call(
        functools.partial(_dn_kernel, hg=hg),
        out_shape=[jax.ShapeDtypeStruct((bsz, t, h * DN_DV), BF16), jax.ShapeDtypeStruct((bsz, tc, h * DN_DV), BF16)],
        grid=(bsz, h // hg),
        in_specs=[slab(t, 0), slab(t, h), slab(t, 2 * h), slab(t, 0), whole(t),
                  slab(tc, 0), slab(tc, h), slab(tc, 2 * h), slab(tc, 0), whole(tc),
                  cw(0), cw(h), cw(2 * h), vec, vec, vec],
        out_specs=[slab(t, 0), slab(tc, 0)],
        scratch_shapes=[
            pltpu.VMEM((3, 3 * SUBLANES + tc + t, LANES), F32),
            pltpu.VMEM((tc + t, LANES), F32),
            pltpu.VMEM((hg, 2, nch, DN_DK + c, DN_DV), BF16),
            pltpu.VMEM((hg, 2, nch, DN_DK, DN_DV), F32),
            pltpu.VMEM((hg, 2, nch, SUBLANES, LANES), F32),
            pltpu.VMEM((hg, 2, tc + t, DN_DV), F32),
            pltpu.VMEM((hg, 2, DN_DK, DN_DV), F32),
        ],
        compiler_params=_params("parallel", "parallel"),
        name="deltanet",
    )(dn, dn, dn, z, ab, dnc, dnc, dnc, zc, abc, conv_w, conv_w, conv_w, alog, dtb, norm_w)


def _swa_kernel(sink_ref, q_ref, k_ref, v_ref, kc_ref, vc_ref, o_ref, *, n_blocks):
    blk = SWA_BLOCK
    n_kv_pairs = k_ref.shape[1] // LANES
    slabs = q_ref.shape[1] // LANES // n_kv_pairs
    rows = slabs * blk
    low = lax.broadcasted_iota(jnp.int32, (rows, LANES), 1) < SWA_HD
    low_blk = lax.broadcasted_iota(jnp.int32, (blk, LANES), 1) < SWA_HD
    low1 = lax.broadcasted_iota(jnp.int32, (1, LANES), 1) < SWA_HD
    low16 = jnp.where(low1, 1.0, 0.0).astype(BF16)
    high16 = jnp.where(low1, 0.0, 1.0).astype(BF16)
    qi = lax.broadcasted_iota(jnp.int32, (rows, blk), 0) % blk
    kj = lax.broadcasted_iota(jnp.int32, (rows, blk), 1)

    def block_body(n, carry):
        lo = jnp.maximum(n - 1, 0)
        hi = jnp.minimum(n + 1, n_blocks - 1)
        ok_lo = (kj >= qi) & (n > 0)
        ok_hi = (kj <= qi) & (n < n_blocks - 1)

        def tok(i):
            return pl.ds(pl.multiple_of(i * blk, blk), blk)

        def kv_rows(ref, ctx_ref, kvl):
            return jnp.concatenate([ref[tok(lo), kvl], ref[tok(n), kvl], ref[tok(hi), kvl], ctx_ref[:, kvl]], axis=0)

        def softmax(s, p, half):
            cols = [s[:, j * blk:(j + 1) * blk] for j in range(s.shape[1] // blk)]
            cols[0] = jnp.where(ok_lo, cols[0], NEG_INF)
            cols[2] = jnp.where(ok_hi, cols[2], NEG_INF)
            probs, e_sink = [], []
            for j in range(slabs):
                cj = [c[j * blk:(j + 1) * blk] for c in cols]
                sink = sink_ref[(2 * p + half) * slabs + j] * LOG2E
                m = jnp.maximum(jnp.max(functools.reduce(jnp.maximum, cj), axis=-1, keepdims=True), sink)
                probs.append(jnp.concatenate([jnp.exp2(c - m).astype(BF16) for c in cj], axis=1))
                e_sink.append(jnp.exp2(sink - m))
            return jnp.concatenate(probs, axis=0), e_sink

        units = [(p, half) for p in range(n_kv_pairs) for half in range(2)]
        kvls = [slice(p * LANES, (p + 1) * LANES) for p in range(n_kv_pairs)]
        k_all = [kv_rows(k_ref, kc_ref, kvl) for kvl in kvls]
        v_all = [kv_rows(v_ref, vc_ref, kvl) for kvl in kvls]
        qs = [jnp.concatenate([q_ref[tok(n), (p * slabs + j) * LANES:(p * slabs + j + 1) * LANES]
                               for j in range(slabs)], axis=0) for p in range(n_kv_pairs)]
        s = [_dot_nt(qs[p] * (high16 if half else low16), k_all[p]) for p, half in units]
        pe = [softmax(x, p, half) for x, (p, half) in zip(s, units)]
        r = [_dot(x[0], v_all[p] * (high16 if half else low16) + (low16 if half else high16))
             for x, (p, half) in zip(pe, units)]
        res = [jnp.concatenate(
            [y[j * blk:(j + 1) * blk] + jnp.where(low_blk if half else ~low_blk, x[1][j], 0.0) for j in range(slabs)],
            axis=0) for y, x, (p, half) in zip(r, pe, units)]
        for p in range(n_kv_pairs):
            num = jnp.where(low, res[2 * p], res[2 * p + 1])
            den = pltpu.roll(jnp.where(low, res[2 * p + 1], res[2 * p]), SWA_HD, 1)
            o = (num / den).astype(BF16)
            for j in range(slabs):
                o_ref[tok(n), (p * slabs + j) * LANES:(p * slabs + j + 1) * LANES] = o[j * blk:(j + 1) * blk]
        return carry

    lax.fori_loop(0, n_blocks, block_body, 0)


def _swa_call(sink, q, k, v, kc, vc):
    bsz, t, wq = q.shape
    wk = k.shape[2]
    tc = kc.shape[1]
    return pl.pallas_call(
        functools.partial(_swa_kernel, n_blocks=t // SWA_BLOCK),
        out_shape=jax.ShapeDtypeStruct((bsz, t, wq), BF16),
        grid=(bsz,),
        in_specs=[
            pl.BlockSpec(memory_space=pltpu.SMEM),
            pl.BlockSpec((None, t, wq), lambda b: (b, 0, 0)),
            pl.BlockSpec((None, t, wk), lambda b: (b, 0, 0)),
            pl.BlockSpec((None, t, wk), lambda b: (b, 0, 0)),
            pl.BlockSpec((None, tc, wk), lambda b: (b, 0, 0)),
            pl.BlockSpec((None, tc, wk), lambda b: (b, 0, 0)),
        ],
        out_specs=pl.BlockSpec((None, t, wq), lambda b: (b, 0, 0)),
        compiler_params=_params("parallel"),
        name="swa_attn",
    )(sink, q, k, v, kc, vc)


def _swa_slab_order(w, axis):
    rep = SWA_HEADS // SWA_KV_HEADS
    shape = w.shape
    split = shape[:axis] + (SWA_KV_HEADS // 2, 2, rep, SWA_HD) + shape[axis + 1:]
    return jnp.swapaxes(w.reshape(split), axis + 1, axis + 2).reshape(shape)


def _rope_tables(t):
    pos = jnp.arange(t)
    half = SWA_HD // 2
    inv = jnp.power(ROPE_BASE, -jnp.arange(0, half, 2, dtype=F32) / half)
    ang_r = (pos // GRID_W).astype(F32)[:, None] * inv
    ang_c = (pos % GRID_W).astype(F32)[:, None] * inv
    cos = jnp.concatenate([jnp.cos(ang_r)] * 2 + [jnp.cos(ang_c)] * 2, axis=-1)
    sin = jnp.concatenate([-jnp.sin(ang_r), jnp.sin(ang_r), -jnp.sin(ang_c), jnp.sin(ang_c)], axis=-1)
    return jnp.tile(cos, (1, LANES // SWA_HD)), jnp.tile(sin, (1, LANES // SWA_HD))


def kernel(x, c, ctx, c_ctx, ada_w, ada_b, norm_g, ffn_w_up, ffn_w_down, even_w_in, even_w_out, na_rpb,
           dn_conv_w, dn_a_log, dn_dt_bias, dn_norm_w, odd_w_in, odd_w_out, swa_sink, final_norm_g):
    bsz, t, d = x.shape
    tc = ctx.shape[1]
    depth = ada_w.shape[0]
    ctx_row = bsz

    c16 = jnp.concatenate([c, c_ctx[None, :], jnp.zeros((MOD_ROWS - bsz - 1, d), F32)], axis=0)
    mods = _adaln_call(c16, ada_w, ada_b).reshape(depth, N_MOD, MOD_ROWS, 1, d)
    norm_g3 = norm_g.reshape(depth * 3, 1, d)

    w_up = ffn_w_up.astype(BF16)
    f = ffn_w_down.shape[2]
    w_down = ffn_w_down.astype(BF16).reshape(depth, 2, f // FF_CHUNK, FF_CHUNK, d)

    h = x
    hc = ctx.reshape(1, bsz * tc, d)
    for i in range(depth):
        need_ctx = i < depth - 1
        j = i // 2
        h = _ffn_call(h, mods, norm_g3, i, 0, None, w_up, w_down)
        hc = _ffn_call(hc, mods, norm_g3, i, 0, ctx_row, w_up, w_down)
        mix_c = None
        if i % 2 == 0:
            w_in = even_w_in[j]
            w_in = jnp.pad(w_in, ((0, 0), (0, -w_in.shape[1] % LANES))).astype(BF16)
            na, dn, z, ab = _proj_even_call(h, mods, norm_g3, i, None, w_in)
            nac, dnc, zc, abc = [a.reshape(bsz, tc, a.shape[-1])
                                 for a in _proj_even_call(hc, mods, norm_g3, i, ctx_row, w_in)]
            y_na, y_na_c = _na_call(na, nac, _na_bias_table(na_rpb[j]))
            lane_pad = (0, LANES - 2 * DN_HEADS)
            alog = jnp.pad(dn_a_log[j].reshape(-1), lane_pad)[None, :]
            dtb = jnp.pad(dn_dt_bias[j].reshape(-1), lane_pad)[None, :]
            y_dn, y_dn_c = _dn_call(dn, z, ab, dnc, zc, abc, dn_conv_w[j], alog, dtb, dn_norm_w[j][None, :])
            w_out = even_w_out[j].astype(BF16)
            ws = [w_out[:NA_HEADS * NA_HD], w_out[NA_HEADS * NA_HD:]]
            mix = ([y_na, y_dn], ws)
            if need_ctx:
                mix_c = ([y.reshape(1, bsz * tc, y.shape[-1]) for y in (y_na_c, y_dn_c)], ws)
        else:
            if need_ctx:
                raise NotImplementedError("context queries of a windowed layer are only needed before a later layer")
            w_in = odd_w_in[j]
            qw = SWA_HEADS * SWA_HD
            kw = SWA_KV_HEADS * SWA_HD
            wq = _swa_slab_order(w_in[:, :qw].astype(BF16), 1)
            wk = w_in[:, qw:qw + kw].astype(BF16)
            wv = w_in[:, qw + kw:].astype(BF16)
            cos, sin = _rope_tables(t)
            q, k, v = _proj_odd_call(h, mods, norm_g3, i, None, wq, wk, wv, cos, sin)
            kc, vc = [a.reshape(bsz, tc, kw) for a in _proj_odd_call(hc, mods, norm_g3, i, ctx_row, None, wk, wv, None, None)]
            mix = ([_swa_call(swa_sink[j], q, k, v, kc, vc)], [_swa_slab_order(odd_w_out[j].astype(BF16), 0)])
        last = i == depth - 1
        h = _ffn_call(h, mods, norm_g3, i, 1, None, w_up, w_down, mix=mix,
                      final_g=final_norm_g[None, :] if last else None)
        if need_ctx:
            hc = _ffn_call(hc, mods, norm_g3, i, 1, ctx_row, w_up, w_down, mix=mix_c)
    return h
```

```python
import functools
import math

import jax
import jax.numpy as jnp
from jax import lax
from jax.experimental import pallas as pl
from jax.experimental.pallas import tpu as pltpu

F32 = jnp.float32
BF16 = jnp.bfloat16

GRID_W = 64
N_MOD = 9
NORM_EPS = 1e-6
NEG_INF = -1e30
LOG2E = math.log2(math.e)
ROPE_BASE = 10000.0
NA_HEADS = 8
NA_HD = 64
NA_WIN_H = 8
NA_WIN_W = 16
NA_PROJ = 3 * NA_HEADS * NA_HD
DN_HEADS = 4
DN_DK = 128
DN_DV = 128
DN_CONV = 5
DN_CHUNK = 64
DN_QKV = DN_HEADS * (2 * DN_DK + DN_DV)
SWA_HEADS = 16
SWA_KV_HEADS = 4
SWA_HD = 64
SWA_BLOCK = 128

LANES = 128
SUBLANES = 8
VMEM_LIMIT = 56 * 1024 * 1024

TOK_TILE = 1024
FF_CHUNK = 256
MOD_ROWS = 16
DN_HEAD_GROUP = 2
NA_ROWS_PER_STEP = 2
DN_PREP_CHUNKS = 9


def _params(*sem):
    return pltpu.CompilerParams(dimension_semantics=sem, vmem_limit_bytes=VMEM_LIMIT)


def _dot(a, b):
    return jnp.dot(a, b, preferred_element_type=F32)


def _dot_nt(a, b):
    return lax.dot_general(a, b, (((1,), (1,)), ((), ())), preferred_element_type=F32)


def _dot_tn(a, b):
    return lax.dot_general(a, b, (((0,), (0,)), ((), ())), preferred_element_type=F32)


def _sigmoid(x):
    return 0.5 + 0.5 * jnp.tanh(0.5 * x)


def _silu(x):
    h = 0.5 * x
    return h + h * jnp.tanh(h)


def _softplus(x):
    return jnp.maximum(x, 0.0) + jnp.log1p(jnp.exp(-jnp.abs(x)))


def _modulate(x, g, shift, scale):
    y = x * lax.rsqrt(jnp.mean(x * x, axis=-1, keepdims=True) + NORM_EPS) * g
    return y * (1.0 + scale) + shift


def _split3(x):
    h1 = x.astype(BF16)
    r1 = x - h1.astype(F32)
    h2 = r1.astype(BF16)
    h3 = (r1 - h2.astype(F32)).astype(BF16)
    return h1, h2, h3


def _split2(x):
    hi = x.astype(BF16)
    return hi, (x - hi.astype(F32)).astype(BF16)


def _adaln_kernel(c_ref, w_ref, b_ref, o_ref):
    s = _silu(c_ref[...]).astype(BF16)
    o_ref[...] = _dot(s, w_ref[...].astype(BF16)) + b_ref[...]


def _adaln_call(c16, ada_w, ada_b):
    depth, d, _ = ada_w.shape
    b4 = ada_b.reshape(depth * N_MOD, 1, d)
    return pl.pallas_call(
        _adaln_kernel,
        out_shape=jax.ShapeDtypeStruct((depth, N_MOD, MOD_ROWS, d), F32),
        grid=(depth, N_MOD),
        in_specs=[
            pl.BlockSpec((MOD_ROWS, d), lambda i, k: (0, 0)),
            pl.BlockSpec((None, d, d), lambda i, k: (i, 0, k)),
            pl.BlockSpec((None, 1, d), lambda i, k: (i * N_MOD + k, 0, 0)),
        ],
        out_specs=pl.BlockSpec((None, None, MOD_ROWS, d), lambda i, k: (i, k, 0, 0)),
        compiler_params=_params("arbitrary", "arbitrary"),
        name="adaln",
    )(c16, ada_w, b4)


def _mod_spec(layer, k, ctx_row, d):
    if ctx_row is None:
        return pl.BlockSpec((None, None, None, 1, d), lambda b, t: (layer, k, b, 0, 0))
    return pl.BlockSpec((None, None, None, 1, d), lambda b, t: (layer, k, ctx_row, 0, 0))


def _const_spec(shape):
    nd = len(shape)
    return pl.BlockSpec(shape, lambda b, t: (0,) * nd, pipeline_mode=pl.Buffered(1))


def _ffn_kernel(x_ref, g_ref, sh_ref, sc_ref, gt_ref, wup_ref, wd_ref, *rest, n_mix, final):
    if n_mix:
        mg_ref, rest = rest[0], rest[1:]
        y_refs, w_refs, rest = rest[:n_mix], rest[n_mix:2 * n_mix], rest[2 * n_mix:]
    if final:
        fg_ref, rest = rest[0], rest[1:]
    o_ref, u_ref, acc_ref = rest
    n_chunks, fc, _ = wd_ref.shape
    f = n_chunks * fc

    x = x_ref[...]
    if n_mix:
        mix = _dot(y_refs[0][...], w_refs[0][...])
        for y_ref, w_ref in zip(y_refs[1:], w_refs[1:]):
            mix += _dot(y_ref[...], w_ref[...])
        x = x + mg_ref[...] * mix
        o_ref[...] = x
    u_ref[...] = _modulate(x, g_ref[...], sh_ref[...], sc_ref[...]).astype(BF16)
    acc_ref[...] = jnp.zeros_like(acc_ref)

    def body(c, carry):
        u = u_ref[...]
        off = pl.multiple_of(c * fc, fc)
        gate = _dot(u, wup_ref[:, pl.ds(off, fc)])
        up = _dot(u, wup_ref[:, pl.ds(f + off, fc)])
        a = (_silu(gate) * up).astype(BF16)
        acc_ref[...] += _dot(a, wd_ref[c])
        return carry

    lax.fori_loop(0, n_chunks, body, 0, unroll=True)
    y = (o_ref[...] if n_mix else x_ref[...]) + 0.5 * gt_ref[...] * acc_ref[...]
    if final:
        y = y * lax.rsqrt(jnp.mean(y * y, axis=-1, keepdims=True) + NORM_EPS) * fg_ref[...]
    o_ref[...] = y


def _ffn_call(h, mods, norm_g3, layer, which, ctx_row, w_up, w_down, mix=None, final_g=None):
    bsz, t, d = h.shape
    tm = min(TOK_TILE, t)
    k0 = 6 if which else 0
    in_specs = [
        pl.BlockSpec((None, tm, d), lambda b, i: (b, i, 0)),
        pl.BlockSpec((None, 1, d), lambda b, i: (layer * 3 + (2 if which else 0), 0, 0)),
        _mod_spec(layer, k0, ctx_row, d),
        _mod_spec(layer, k0 + 1, ctx_row, d),
        _mod_spec(layer, k0 + 2, ctx_row, d),
        pl.BlockSpec((None, None) + w_up.shape[2:], lambda b, i: (layer, which, 0, 0), pipeline_mode=pl.Buffered(1)),
        pl.BlockSpec((None, None) + w_down.shape[2:], lambda b, i: (layer, which, 0, 0, 0),
                     pipeline_mode=pl.Buffered(1)),
    ]
    args = [h, norm_g3, mods, mods, mods, w_up, w_down]
    n_mix = 0
    if mix is not None:
        ys, ws = mix
        n_mix = len(ys)
        in_specs.append(_mod_spec(layer, 5, ctx_row, d))
        in_specs += [pl.BlockSpec((None, tm, y.shape[2]), lambda b, i: (b, i, 0)) for y in ys]
        in_specs += [_const_spec(w.shape) for w in ws]
        args += [mods, *ys, *ws]
    if final_g is not None:
        in_specs.append(pl.BlockSpec((1, d), lambda b, i: (0, 0)))
        args.append(final_g)
    return pl.pallas_call(
        functools.partial(_ffn_kernel, n_mix=n_mix, final=final_g is not None),
        out_shape=jax.ShapeDtypeStruct(h.shape, F32),
        grid=(bsz, t // tm),
        in_specs=in_specs,
        out_specs=pl.BlockSpec((None, tm, d), lambda b, i: (b, i, 0)),
        scratch_shapes=[pltpu.VMEM((tm, d), BF16), pltpu.VMEM((tm, d), F32)],
        compiler_params=_params("parallel", "parallel"),
        name="ffn",
    )(*args)


def _proj_even_kernel(x_ref, g_ref, sh_ref, sc_ref, w_ref, na_ref, dn_ref, z_ref, ab_ref):
    u = _modulate(x_ref[...], g_ref[...], sh_ref[...], sc_ref[...]).astype(BF16)
    c0 = 0
    for o_ref in (na_ref, dn_ref, z_ref, ab_ref):
        c1 = c0 + o_ref.shape[1]
        y = _dot(u, w_ref[:, c0:c1])
        if o_ref is na_ref:
            qw = NA_HEADS * NA_HD
            o_ref[:, :qw] = (y[:, :qw] * (NA_HD ** -0.5 * LOG2E)).astype(BF16)
            o_ref[:, qw:] = y[:, qw:].astype(BF16)
        else:
            o_ref[...] = y
        c0 = c1


def _proj_even_call(h, mods, norm_g3, layer, ctx_row, w_in):
    bsz, t, d = h.shape
    tm = min(TOK_TILE, t)
    widths = (NA_PROJ, DN_QKV, DN_HEADS * DN_DV, LANES)
    dtypes = (BF16, F32, F32, F32)
    return pl.pallas_call(
        _proj_even_kernel,
        out_shape=[jax.ShapeDtypeStruct((bsz, t, w), dt) for w, dt in zip(widths, dtypes)],
        grid=(bsz, t // tm),
        in_specs=[
            pl.BlockSpec((None, tm, d), lambda b, i: (b, i, 0)),
            pl.BlockSpec((None, 1, d), lambda b, i: (layer * 3 + 1, 0, 0)),
            _mod_spec(layer, 3, ctx_row, d),
            _mod_spec(layer, 4, ctx_row, d),
            _const_spec(w_in.shape),
        ],
        out_specs=[pl.BlockSpec((None, tm, w), lambda b, i: (b, i, 0)) for w in widths],
        compiler_params=_params("parallel", "parallel"),
        name="proj_even",
    )(h, norm_g3, mods, mods, w_in)


def _rope_slab(x, cos, sin, first):
    swapped = jnp.where(first, pltpu.roll(x, LANES - 16, 1), pltpu.roll(x, 16, 1))
    return x * cos + swapped * sin


def _proj_odd_kernel(x_ref, g_ref, sh_ref, sc_ref, *rest, with_q):
    if with_q:
        wq_ref, wk_ref, wv_ref, cos_ref, sin_ref, q_ref, k_ref, v_ref = rest
    else:
        wk_ref, wv_ref, k_ref, v_ref = rest
    u = _modulate(x_ref[...], g_ref[...], sh_ref[...], sc_ref[...]).astype(BF16)
    v_ref[...] = _dot(u, wv_ref[...]).astype(BF16)
    k = _dot(u, wk_ref[...])
    if not with_q:
        k_ref[...] = k.astype(BF16)
        return
    cos = cos_ref[...]
    sin = sin_ref[...]
    lane = lax.broadcasted_iota(jnp.int32, cos.shape, 1)
    first = (lane % 32) < 16
    for j in range(k.shape[1] // LANES):
        sl = slice(j * LANES, (j + 1) * LANES)
        k_ref[:, sl] = _rope_slab(k[:, sl], cos, sin, first).astype(BF16)
    q = _dot(u, wq_ref[...])
    for j in range(q.shape[1] // LANES):
        sl = slice(j * LANES, (j + 1) * LANES)
        q_ref[:, sl] = (_rope_slab(q[:, sl], cos, sin, first) * (SWA_HD ** -0.5 * LOG2E)).astype(BF16)


def _proj_odd_call(h, mods, norm_g3, layer, ctx_row, wq, wk, wv, cos, sin):
    bsz, t, d = h.shape
    tm = min(TOK_TILE, t)
    with_q = wq is not None
    in_specs = [
        pl.BlockSpec((None, tm, d), lambda b, i: (b, i, 0)),
        pl.BlockSpec((None, 1, d), lambda b, i: (layer * 3 + 1, 0, 0)),
        _mod_spec(layer, 3, ctx_row, d),
        _mod_spec(layer, 4, ctx_row, d),
    ]
    args = [h, norm_g3, mods, mods]
    widths = []
    if with_q:
        in_specs.append(_const_spec(wq.shape))
        args.append(wq)
        widths.append(wq.shape[1])
    in_specs += [_const_spec(wk.shape), _const_spec(wv.shape)]
    args += [wk, wv]
    widths += [wk.shape[1], wv.shape[1]]
    if with_q:
        in_specs += [pl.BlockSpec((tm, LANES), lambda b, i: (i, 0))] * 2
        args += [cos, sin]
    return pl.pallas_call(
        functools.partial(_proj_odd_kernel, with_q=with_q),
        out_shape=[jax.ShapeDtypeStruct((bsz, t, w), BF16) for w in widths],
        grid=(bsz, t // tm),
        in_specs=in_specs,
        out_specs=[pl.BlockSpec((None, tm, w), lambda b, i: (b, i, 0)) for w in widths],
        compiler_params=_params("parallel", "parallel"),
        name="proj_odd",
    )(*args)


def _softmax2(cols):
    m = jnp.max(functools.reduce(jnp.maximum, cols), axis=-1, keepdims=True)
    p = [jnp.exp2(c - m) for c in cols]
    den = jnp.sum(functools.reduce(jnp.add, p), axis=-1, keepdims=True)
    return jnp.concatenate([x.astype(BF16) for x in p], axis=1), den


def _na_kernel(q_ref, k_ref, v_ref, qc_ref, kc_ref, vc_ref, bias_ref, o_ref, oc_ref, *, rows):
    n_pairs = q_ref.shape[1] // LANES
    tc = qc_ref.shape[0]
    win = NA_WIN_H * GRID_W
    n_bias = win // LANES
    low1 = lax.broadcasted_iota(jnp.int32, (1, LANES), 1) < NA_HD
    low16 = jnp.where(low1, 1.0, 0.0).astype(BF16)
    high16 = jnp.where(low1, 0.0, 1.0).astype(BF16)
    low_q = lax.broadcasted_iota(jnp.int32, (GRID_W, LANES), 1) < NA_HD
    low_c = lax.broadcasted_iota(jnp.int32, (tc, LANES), 1) < NA_HD
    pairs = [slice(hp * LANES, (hp + 1) * LANES) for hp in range(n_pairs)]

    def row_body(i, carry):
        units = []
        for rr in range(NA_ROWS_PER_STEP):
            r = i * NA_ROWS_PER_STEP + rr
            r0 = jnp.clip(r - NA_WIN_H // 2, 0, rows - NA_WIN_H)
            doff = r0 - r + NA_WIN_H - 1
            qs = pl.ds(pl.multiple_of(r * GRID_W, GRID_W), GRID_W)
            ks = pl.ds(pl.multiple_of(r0 * GRID_W, GRID_W), win)
            units += [(hp, sl, doff, qs, ks) for hp, sl in enumerate(pairs)]
        qm = [jnp.concatenate([q_ref[qs, sl] * low16, q_ref[qs, sl] * high16], axis=0) for _, sl, _, qs, _ in units]
        s = [_dot_nt(x, jnp.concatenate([k_ref[ks, sl], kc_ref[:, sl]], axis=0))
             for x, (_, sl, _, _, ks) in zip(qm, units)]
        cols = [[x[:, j * LANES:(j + 1) * LANES] + bias_ref[hp, doff + 2 * j] for j in range(n_bias)]
                + [x[:, j * LANES:(j + 1) * LANES] for j in range(n_bias, x.shape[1] // LANES)]
                for x, (hp, _, doff, _, _) in zip(s, units)]
        pd = [_softmax2(c) for c in cols]
        o = [_dot(p, jnp.concatenate([v_ref[ks, sl], vc_ref[:, sl]], axis=0)) / den
             for (p, den), (_, sl, _, _, ks) in zip(pd, units)]
        for x, (_, sl, _, qs, _) in zip(o, units):
            o_ref[qs, sl] = jnp.where(low_q, x[:GRID_W], x[GRID_W:]).astype(BF16)
        return carry

    lax.fori_loop(0, rows // NA_ROWS_PER_STEP, row_body, 0, unroll=4)

    for sl in pairs:
        halves = []
        for keep16 in (low16, high16):
            s = _dot_nt(qc_ref[:, sl] * keep16, kc_ref[:, sl])
            p, den = _softmax2([s[:, j * LANES:(j + 1) * LANES] for j in range(tc // LANES)])
            halves.append(_dot(p, vc_ref[:, sl]) / den)
        oc_ref[:, sl] = jnp.where(low_c, halves[0], halves[1]).astype(BF16)


def _na_call(qkv, qkvc, bias):
    bsz, t, w3 = qkv.shape
    tc = qkvc.shape[1]
    w = w3 // 3

    def col(j):
        return lambda b: (b, 0, j)

    return pl.pallas_call(
        functools.partial(_na_kernel, rows=t // GRID_W),
        out_shape=[jax.ShapeDtypeStruct((bsz, t, w), BF16), jax.ShapeDtypeStruct((bsz, tc, w), BF16)],
        grid=(bsz,),
        in_specs=[pl.BlockSpec((None, t, w), col(j)) for j in range(3)]
        + [pl.BlockSpec((None, tc, w), col(j)) for j in range(3)]
        + [pl.BlockSpec(bias.shape, lambda b: (0, 0, 0, 0), pipeline_mode=pl.Buffered(1))],
        out_specs=[pl.BlockSpec((None, t, w), lambda b: (b, 0, 0)), pl.BlockSpec((None, tc, w), lambda b: (b, 0, 0))],
        compiler_params=_params("parallel"),
        name="na_attn",
    )(qkv, qkv, qkv, qkvc, qkvc, qkvc, bias)


def _na_bias_table(rpb):
    c = jnp.arange(GRID_W)[:, None]
    kc = jnp.arange(GRID_W)[None, :]
    cstart = jnp.clip(c - NA_WIN_W // 2, 0, GRID_W - NA_WIN_W)
    ok = (kc >= cstart) & (kc < cstart + NA_WIN_W)
    dc = jnp.clip(kc - c + NA_WIN_W - 1, 0, 2 * NA_WIN_W - 2)
    onehot = (dc[None] == jnp.arange(2 * NA_WIN_W - 1)[:, None, None]).astype(F32)
    h = rpb.shape[0]
    nd = 2 * NA_WIN_H - 2
    rp = (rpb.astype(F32) * LOG2E).reshape(h // 2, 2, 2 * NA_WIN_H - 1, 2 * NA_WIN_W - 1)
    t = jnp.einsum('phdx,xck->pdhck', rp, onehot, precision=lax.Precision.HIGHEST)
    t = jnp.where(ok, t, NEG_INF).reshape(h // 2, 2 * NA_WIN_H - 1, 2 * GRID_W, GRID_W)
    return jnp.concatenate([t[:, :nd], t[:, 1:nd + 1]], axis=-1)


def _dn_kernel(qr_ref, kr_ref, vr_ref, z_ref, ab_ref, qcr_ref, kcr_ref, vcr_ref, zc_ref, abc_ref,
               cwq_ref, cwk_ref, cwv_ref, alog_ref, dtb_ref, nw_ref,
               y_ref, yc_ref,
               pad_ref, abs_ref, aq_ref, b_ref, gl_ref, o_ref, s_ref, *, hg):
    c = DN_CHUNK
    t = qr_ref.shape[0]
    tc = qcr_ref.shape[0]
    ncc, ncl = tc // c, t // c
    lat0 = 2 * SUBLANES + tc
    half = DN_CONV // 2
    n_sq = int(math.log2(c)) - 1

    zeros8 = jnp.zeros((3, SUBLANES, LANES), F32)
    pad_ref[:, 0:SUBLANES, :] = zeros8
    pad_ref[:, SUBLANES + tc:lat0, :] = zeros8
    pad_ref[:, lat0 + t:lat0 + t + SUBLANES, :] = zeros8
    abs_ref[0:tc, :] = abc_ref[...]
    abs_ref[tc:tc + t, :] = ab_ref[...]
    s_ref[...] = jnp.zeros_like(s_ref)

    lane = lax.broadcasted_iota(jnp.int32, (c, LANES), 1)
    row = lax.broadcasted_iota(jnp.int32, (c, LANES), 0)
    col = jnp.bitwise_and(lane, c - 1)
    fwd = lane < c
    incl = (fwd & (row >= col)) | (~fwd & (row <= col))
    strict = (fwd & (row > col)) | (~fwd & (row < col))
    diag = row == col
    eye2 = jnp.where(diag, 1.0, 0.0).astype(F32)
    fwd16 = jnp.where(fwd, 1.0, 0.0).astype(BF16)
    bwd16 = jnp.where(fwd, 0.0, 1.0).astype(BF16)
    ri = lax.broadcasted_iota(jnp.int32, (2 * c, c), 0)
    ci_ = lax.broadcasted_iota(jnp.int32, (2 * c, c), 1)
    tri2 = jnp.where(((ri < c) & (ci_ <= ri)) | ((ri >= c) & (ci_ >= ri - c)), 1.0, 0.0).astype(BF16)
    tri6 = jnp.concatenate([tri2, tri2, tri2], axis=1)
    neg_a = -jnp.exp(alog_ref[...])
    dtb = dtb_ref[...]

    def l2n(x):
        return x * lax.rsqrt(jnp.sum(x * x, axis=-1, keepdims=True) + NORM_EPS)

    def bd(m16):
        return jnp.concatenate([m16 * fwd16, m16 * bwd16], axis=0)

    def mm3(lhs_parts, bh, bl):
        bdh = bd(bh)
        rhs = jnp.concatenate([bdh, bdh, bd(bl)], axis=0)
        lhs = jnp.concatenate([jnp.concatenate([ah, al, ah], axis=1) for ah, al in lhs_parts], axis=0)
        return _dot(lhs, rhs)

    def prep_head(hh):
        hid = pl.program_id(1) * hg + hh
        hs = slice(hh * LANES, (hh + 1) * LANES)
        for s, (cr, lr) in enumerate(((qcr_ref, qr_ref), (kcr_ref, kr_ref), (vcr_ref, vr_ref))):
            pad_ref[s, SUBLANES:SUBLANES + tc, :] = cr[:, hs]
            pad_ref[s, lat0:lat0 + t, :] = lr[:, hs]

        def conv(s, cw_ref, base):
            acc = pad_ref[s, pl.ds(base - half, c), :] * cw_ref[0:1, hs]
            for j in range(1, DN_CONV):
                acc += pad_ref[s, pl.ds(base - half + j, c), :] * cw_ref[j:j + 1, hs]
            return _silu(acc)

        def pick(x, idx):
            return jnp.broadcast_to(jnp.sum(jnp.where(lane == idx, x, 0.0), axis=-1, keepdims=True), (c, LANES))

        def stage_inputs(ch):
            base = ch * c + jnp.where(ch < ncc, SUBLANES, 2 * SUBLANES)
            q = l2n(conv(0, cwq_ref, base))
            k = l2n(conv(1, cwk_ref, base))
            v = conv(2, cwv_ref, base)
            abt = abs_ref[pl.ds(pl.multiple_of(ch * c, c), c), :]
            g_all = neg_a * _softplus(abt + dtb)
            b_all = _sigmoid(abt)
            return q * DN_DK ** -0.5, k, v, g_all, b_all

        def stage_decay(g_all):
            parts = []
            for gp in _split3(g_all):
                gp = gp.astype(F32)
                pf = pick(gp, hid)
                pb = pick(gp, DN_HEADS + hid)
                parts.append(jnp.concatenate([jnp.where(strict, jnp.where(fwd, pf, pb), 0.0), pf, pb], axis=1).astype(BF16))
            res = _dot(tri6, jnp.concatenate(parts, axis=0))
            return jnp.where(fwd, res[:c, 0:LANES], res[c:, 0:LANES]), res[:c, LANES:2 * LANES], res[c:, 2 * LANES:]

        def stage_rhs(qs, k, v, gc_f, gc_b, bt_f, bt_b):
            e_f = jnp.exp(gc_f)
            e_b = jnp.exp(gc_b)
            rhs = jnp.concatenate([
                jnp.concatenate([v * bt_f, k * (bt_f * e_f)], axis=1),
                jnp.concatenate([v * bt_b, k * (bt_b * e_b)], axis=1)], axis=0).astype(BF16)
            gl_f = gc_f[c - 1:c, :]
            gl_b = gc_b[0:1, :]
            kd = jnp.concatenate([k * jnp.exp(gl_f - gc_f), k * jnp.exp(gl_b - gc_b)], axis=0).astype(BF16)
            return rhs, kd, (qs * e_f, qs * e_b), (jnp.exp(gl_f), jnp.exp(gl_b))

        def stage_solve(p, rhs):
            sol16 = _dot(bd(p.astype(BF16)), rhs).astype(BF16)
            z16 = jnp.zeros((c, 2 * DN_DV), BF16)
            return jnp.concatenate([jnp.concatenate([sol16[:c], z16], axis=1),
                                    jnp.concatenate([z16, sol16[c:]], axis=1)], axis=0)

        def stage_store(ch, qsol, ksol, q_dec, egl):
            rows = pl.ds(pl.multiple_of(ch * c, c), c)
            for d in range(2):
                lo = 2 * DN_DV * d
                aq_ref[hh, d, ch, 0:DN_DK, :] = ksol[:, lo + DN_DV:lo + 2 * DN_DV].astype(BF16)
                aq_ref[hh, d, ch, DN_DK:DN_DK + c, :] = (q_dec[d] - qsol[:, lo + DN_DV:lo + 2 * DN_DV]).astype(BF16)
                b_ref[hh, d, ch] = ksol[:, lo:lo + DN_DV]
                o_ref[hh, d, rows, :] = qsol[:, lo:lo + DN_DV]
                gl_ref[hh, d, ch] = jnp.broadcast_to(egl[d], (SUBLANES, LANES))

        def body(i, carry):
            chs = [i * DN_PREP_CHUNKS + j for j in range(DN_PREP_CHUNKS)]
            ins = [stage_inputs(ch) for ch in chs]
            dec = [stage_decay(x[3]) for x in ins]
            bts = [(pick(x[4], 2 * DN_HEADS + hid), pick(x[4], 3 * DN_HEADS + hid)) for x in ins]
            decay = [jnp.where(incl, jnp.exp(jnp.where(incl, d[0], 0.0)), 0.0) for d in dec]
            k16 = [x[1].astype(BF16) for x in ins]
            kk = [jnp.concatenate([a, a], axis=0) for a in k16]
            qk16 = [(_dot_nt(x[0].astype(BF16), b) * dc).astype(BF16) for x, b, dc in zip(ins, kk, decay)]
            lmat = [jnp.where(strict, jnp.where(fwd, bt[0], bt[1]) * _dot_nt(a, b) * dc, 0.0)
                    for a, b, bt, dc in zip(k16, kk, bts, decay)]
            lsp = [_split2(x) for x in lmat]
            m = [mm3([s], s[0], s[1]) for s in lsp]
            p = [eye2 - x for x in lmat]
            for s in range(n_sq):
                msp = [_split2(x) for x in m]
                psp = [_split2(x) for x in p]
                if s < n_sq - 1:
                    res = [mm3([a, b], b[0], b[1]) for a, b in zip(psp, msp)]
                    p = [a + r[:c] for a, r in zip(p, res)]
                    m = [r[c:] for r in res]
                else:
                    p = [a + mm3([b], e[0], e[1]) for a, b, e in zip(p, psp, msp)]
            fin = [stage_rhs(x[0], x[1], x[2], d[1], d[2], *bt) for x, d, bt in zip(ins, dec, bts)]
            both = [stage_solve(a, f[0]) for a, f in zip(p, fin)]
            qsol = [_dot(a, b) for a, b in zip(qk16, both)]
            ksol = [_dot_tn(f[1], b) for f, b in zip(fin, both)]
            for ch, a, b, f in zip(chs, qsol, ksol, fin):
                stage_store(ch, a, b, f[2], f[3])
            return carry

        lax.fori_loop(0, (ncc + ncl) // DN_PREP_CHUNKS, body, 0, unroll=True)

    for hh in range(hg):
        prep_head(hh)

    def scan(first, n):
        def body(i, carry):
            chains = [(hh, d, first + i if d == 0 else first + n - 1 - i) for hh in range(hg) for d in range(2)]
            s_mats = [s_ref[hh, d] for hh, d, _ in chains]
            res = [_dot(aq_ref[hh, d, ch], s.astype(BF16)) for s, (hh, d, ch) in zip(s_mats, chains)]
            for s, r, (hh, d, ch) in zip(s_mats, res, chains):
                s_ref[hh, d] = s * gl_ref[hh, d, ch][0:1, :] + (b_ref[hh, d, ch] - r[:DN_DK])
                rows = pl.ds(pl.multiple_of(ch * c, c), c)
                o_ref[hh, d, rows, :] = o_ref[hh, d, rows, :] + r[DN_DK:]
            return carry
        return body

    lax.fori_loop(0, ncc, scan(0, ncc), 0)
    lax.fori_loop(0, ncl, scan(ncc, ncl), 0)

    def gated_norm(o, z):
        return (o * lax.rsqrt(jnp.mean(o * o, axis=-1, keepdims=True) + NORM_EPS) * nw_ref[...] * _silu(z))

    piece = 4 * c
    for hh in range(hg):
        hs = slice(hh * LANES, (hh + 1) * LANES)
        yc_ref[:, hs] = gated_norm(o_ref[hh, 0, 0:tc, :] + o_ref[hh, 1, 0:tc, :], zc_ref[:, hs]).astype(BF16)
        for j in range(t // piece):
            rows = slice(j * piece, (j + 1) * piece)
            orow = slice(tc + j * piece, tc + (j + 1) * piece)
            y_ref[rows, hs] = gated_norm(o_ref[hh, 0, orow, :] + o_ref[hh, 1, orow, :], z_ref[rows, hs]).astype(BF16)


def _dn_call(dn, z, ab, dnc, zc, abc, conv_w, alog, dtb, norm_w):
    bsz, t, _ = dn.shape
    tc = dnc.shape[1]
    h = DN_HEADS
    hg = DN_HEAD_GROUP
    nch = (t + tc) // DN_CHUNK
    c = DN_CHUNK
    wide = hg * LANES

    def slab(rows, j0):
        return pl.BlockSpec((None, rows, wide), lambda b, g: (b, 0, j0 // hg + g))

    def whole(rows):
        return pl.BlockSpec((None, rows, LANES), lambda b, g: (b, 0, 0))

    def cw(j0):
        return pl.BlockSpec((DN_CONV, wide), lambda b, g: (0, j0 // hg + g))

    vec = pl.BlockSpec((1, LANES), lambda b, g: (0, 0))
    return pl.pallas_call(
        functools.partial(_dn_kernel, hg=hg),
        out_shape=[jax.ShapeDtypeStruct((bsz, t, h * DN_DV), BF16), jax.ShapeDtypeStruct((bsz, tc, h * DN_DV), BF16)],
        grid=(bsz, h // hg),
        in_specs=[slab(t, 0), slab(t, h), slab(t, 2 * h), slab(t, 0), whole(t),
                  slab(tc, 0), slab(tc, h), slab(tc, 2 * h), slab(tc, 0), whole(tc),
                  cw(0), cw(h), cw(2 * h), vec, vec, vec],
        out_specs=[slab(t, 0), slab(tc, 0)],
        scratch_shapes=[
            pltpu.VMEM((3, 3 * SUBLANES + tc + t, LANES), F32),
            pltpu.VMEM((tc + t, LANES), F32),
            pltpu.VMEM((hg, 2, nch, DN_DK + c, DN_DV), BF16),
            pltpu.VMEM((hg, 2, nch, DN_DK, DN_DV), F32),
            pltpu.VMEM((hg, 2, nch, SUBLANES, LANES), F32),
            pltpu.VMEM((hg, 2, tc + t, DN_DV), F32),
            pltpu.VMEM((hg, 2, DN_DK, DN_DV), F32),
        ],
        compiler_params=_params("parallel", "parallel"),
        name="deltanet",
    )(dn, dn, dn, z, ab, dnc, dnc, dnc, zc, abc, conv_w, conv_w, conv_w, alog, dtb, norm_w)


def _swa_kernel(sink_ref, q_ref, k_ref, v_ref, kc_ref, vc_ref, o_ref, *, n_blocks):
    blk = SWA_BLOCK
    n_kv_pairs = k_ref.shape[1] // LANES
    slabs = q_ref.shape[1] // LANES // n_kv_pairs
    rows = slabs * blk
    low = lax.broadcasted_iota(jnp.int32, (rows, LANES), 1) < SWA_HD
    low_blk = lax.broadcasted_iota(jnp.int32, (blk, LANES), 1) < SWA_HD
    low1 = lax.broadcasted_iota(jnp.int32, (1, LANES), 1) < SWA_HD
    low16 = jnp.where(low1, 1.0, 0.0).astype(BF16)
    high16 = jnp.where(low1, 0.0, 1.0).astype(BF16)
    qi = lax.broadcasted_iota(jnp.int32, (rows, blk), 0) % blk
    kj = lax.broadcasted_iota(jnp.int32, (rows, blk), 1)

    def block_body(n, carry):
        lo = jnp.maximum(n - 1, 0)
        hi = jnp.minimum(n + 1, n_blocks - 1)
        ok_lo = (kj >= qi) & (n > 0)
        ok_hi = (kj <= qi) & (n < n_blocks - 1)

        def tok(i):
            return pl.ds(pl.multiple_of(i * blk, blk), blk)

        def kv_rows(ref, ctx_ref, kvl):
            return jnp.concatenate([ref[tok(lo), kvl], ref[tok(n), kvl], ref[tok(hi), kvl], ctx_ref[:, kvl]], axis=0)

        def softmax(s, p, half):
            cols = [s[:, j * blk:(j + 1) * blk] for j in range(s.shape[1] // blk)]
            cols[0] = jnp.where(ok_lo, cols[0], NEG_INF)
            cols[2] = jnp.where(ok_hi, cols[2], NEG_INF)
            probs, e_sink = [], []
            for j in range(slabs):
                cj = [c[j * blk:(j + 1) * blk] for c in cols]
                sink = sink_ref[(2 * p + half) * slabs + j] * LOG2E
                m = jnp.maximum(jnp.max(functools.reduce(jnp.maximum, cj), axis=-1, keepdims=True), sink)
                probs.append(jnp.concatenate([jnp.exp2(c - m).astype(BF16) for c in cj], axis=1))
                e_sink.append(jnp.exp2(sink - m))
            return jnp.concatenate(probs, axis=0), e_sink

        units = [(p, half) for p in range(n_kv_pairs) for half in range(2)]
        kvls = [slice(p * LANES, (p + 1) * LANES) for p in range(n_kv_pairs)]
        k_all = [kv_rows(k_ref, kc_ref, kvl) for kvl in kvls]
        v_all = [kv_rows(v_ref, vc_ref, kvl) for kvl in kvls]
        qs = [jnp.concatenate([q_ref[tok(n), (p * slabs + j) * LANES:(p * slabs + j + 1) * LANES]
                               for j in range(slabs)], axis=0) for p in range(n_kv_pairs)]
        s = [_dot_nt(qs[p] * (high16 if half else low16), k_all[p]) for p, half in units]
        pe = [softmax(x, p, half) for x, (p, half) in zip(s, units)]
        r = [_dot(x[0], v_all[p] * (high16 if half else low16) + (low16 if half else high16))
             for x, (p, half) in zip(pe, units)]
        res = [jnp.concatenate(
            [y[j * blk:(j + 1) * blk] + jnp.where(low_blk if half else ~low_blk, x[1][j], 0.0) for j in range(slabs)],
            axis=0) for y, x, (p, half) in zip(r, pe, units)]
        for p in range(n_kv_pairs):
            num = jnp.where(low, res[2 * p], res[2 * p + 1])
            den = pltpu.roll(jnp.where(low, res[2 * p + 1], res[2 * p]), SWA_HD, 1)
            o = (num / den).astype(BF16)
            for j in range(slabs):
                o_ref[tok(n), (p * slabs + j) * LANES:(p * slabs + j + 1) * LANES] = o[j * blk:(j + 1) * blk]
        return carry

    lax.fori_loop(0, n_blocks, block_body, 0, unroll=4)


def _swa_call(sink, q, k, v, kc, vc):
    bsz, t, wq = q.shape
    wk = k.shape[2]
    tc = kc.shape[1]
    return pl.pallas_call(
        functools.partial(_swa_kernel, n_blocks=t // SWA_BLOCK),
        out_shape=jax.ShapeDtypeStruct((bsz, t, wq), BF16),
        grid=(bsz,),
        in_specs=[
            pl.BlockSpec(memory_space=pltpu.SMEM),
            pl.BlockSpec((None, t, wq), lambda b: (b, 0, 0)),
            pl.BlockSpec((None, t, wk), lambda b: (b, 0, 0)),
            pl.BlockSpec((None, t, wk), lambda b: (b, 0, 0)),
            pl.BlockSpec((None, tc, wk), lambda b: (b, 0, 0)),
            pl.BlockSpec((None, tc, wk), lambda b: (b, 0, 0)),
        ],
        out_specs=pl.BlockSpec((None, t, wq), lambda b: (b, 0, 0)),
        compiler_params=_params("parallel"),
        name="swa_attn",
    )(sink, q, k, v, kc, vc)


def _swa_slab_order(w, axis):
    rep = SWA_HEADS // SWA_KV_HEADS
    shape = w.shape
    split = shape[:axis] + (SWA_KV_HEADS // 2, 2, rep, SWA_HD) + shape[axis + 1:]
    return jnp.swapaxes(w.reshape(split), axis + 1, axis + 2).reshape(shape)


def _rope_tables(t):
    pos = jnp.arange(t)
    half = SWA_HD // 2
    inv = jnp.power(ROPE_BASE, -jnp.arange(0, half, 2, dtype=F32) / half)
    ang_r = (pos // GRID_W).astype(F32)[:, None] * inv
    ang_c = (pos % GRID_W).astype(F32)[:, None] * inv
    cos = jnp.concatenate([jnp.cos(ang_r)] * 2 + [jnp.cos(ang_c)] * 2, axis=-1)
    sin = jnp.concatenate([-jnp.sin(ang_r), jnp.sin(ang_r), -jnp.sin(ang_c), jnp.sin(ang_c)], axis=-1)
    return jnp.tile(cos, (1, LANES // SWA_HD)), jnp.tile(sin, (1, LANES // SWA_HD))


def kernel(x, c, ctx, c_ctx, ada_w, ada_b, norm_g, ffn_w_up, ffn_w_down, even_w_in, even_w_out, na_rpb,
           dn_conv_w, dn_a_log, dn_dt_bias, dn_norm_w, odd_w_in, odd_w_out, swa_sink, final_norm_g):
    bsz, t, d = x.shape
    tc = ctx.shape[1]
    depth = ada_w.shape[0]
    ctx_row = bsz

    c16 = jnp.concatenate([c, c_ctx[None, :], jnp.zeros((MOD_ROWS - bsz - 1, d), F32)], axis=0)
    mods = _adaln_call(c16, ada_w, ada_b).reshape(depth, N_MOD, MOD_ROWS, 1, d)
    norm_g3 = norm_g.reshape(depth * 3, 1, d)

    w_up = ffn_w_up.astype(BF16)
    f = ffn_w_down.shape[2]
    w_down = ffn_w_down.astype(BF16).reshape(depth, 2, f // FF_CHUNK, FF_CHUNK, d)

    h = x
    hc = ctx.reshape(1, bsz * tc, d)
    for i in range(depth):
        need_ctx = i < depth - 1
        j = i // 2
        h = _ffn_call(h, mods, norm_g3, i, 0, None, w_up, w_down)
        hc = _ffn_call(hc, mods, norm_g3, i, 0, ctx_row, w_up, w_down)
        mix_c = None
        if i % 2 == 0:
            w_in = even_w_in[j]
            w_in = jnp.pad(w_in, ((0, 0), (0, -w_in.shape[1] % LANES))).astype(BF16)
            na, dn, z, ab = _proj_even_call(h, mods, norm_g3, i, None, w_in)
            nac, dnc, zc, abc = [a.reshape(bsz, tc, a.shape[-1])
                                 for a in _proj_even_call(hc, mods, norm_g3, i, ctx_row, w_in)]
            y_na, y_na_c = _na_call(na, nac, _na_bias_table(na_rpb[j]))
            lane_pad = (0, LANES - 2 * DN_HEADS)
            alog = jnp.pad(dn_a_log[j].reshape(-1), lane_pad)[None, :]
            dtb = jnp.pad(dn_dt_bias[j].reshape(-1), lane_pad)[None, :]
            y_dn, y_dn_c = _dn_call(dn, z, ab, dnc, zc, abc, dn_conv_w[j], alog, dtb, dn_norm_w[j][None, :])
            w_out = even_w_out[j].astype(BF16)
            ws = [w_out[:NA_HEADS * NA_HD], w_out[NA_HEADS * NA_HD:]]
            mix = ([y_na, y_dn], ws)
            if need_ctx:
                mix_c = ([y.reshape(1, bsz * tc, y.shape[-1]) for y in (y_na_c, y_dn_c)], ws)
        else:
            if need_ctx:
                raise NotImplementedError("context queries of a windowed layer are only needed before a later layer")
            w_in = odd_w_in[j]
            qw = SWA_HEADS * SWA_HD
            kw = SWA_KV_HEADS * SWA_HD
            wq = _swa_slab_order(w_in[:, :qw].astype(BF16), 1)
            wk = w_in[:, qw:qw + kw].astype(BF16)
            wv = w_in[:, qw + kw:].astype(BF16)
            cos, sin = _rope_tables(t)
            q, k, v = _proj_odd_call(h, mods, norm_g3, i, None, wq, wk, wv, cos, sin)
            kc, vc = [a.reshape(bsz, tc, kw) for a in _proj_odd_call(hc, mods, norm_g3, i, ctx_row, None, wk, wv, None, None)]
            mix = ([_swa_call(swa_sink[j], q, k, v, kc, vc)], [_swa_slab_order(odd_w_out[j].astype(BF16), 0)])
        last = i == depth - 1
        h = _ffn_call(h, mods, norm_g3, i, 1, None, w_up, w_down, mix=mix,
                      final_g=final_norm_g[None, :] if last else None)
        if need_ctx:
            hc = _ffn_call(hc, mods, norm_g3, i, 1, ctx_row, w_up, w_down, mix=mix_c)
    return h
```

```python
import functools
import math

import jax
import jax.numpy as jnp
from jax import lax
from jax.experimental import pallas as pl
from jax.experimental.pallas import tpu as pltpu

F32 = jnp.float32
BF16 = jnp.bfloat16

GRID_W = 64
N_MOD = 9
NORM_EPS = 1e-6
NEG_INF = -1e30
LOG2E = math.log2(math.e)
ROPE_BASE = 10000.0
NA_HEADS = 8
NA_HD = 64
NA_WIN_H = 8
NA_WIN_W = 16
NA_PROJ = 3 * NA_HEADS * NA_HD
DN_HEADS = 4
DN_DK = 128
DN_DV = 128
DN_CONV = 5
DN_CHUNK = 64
DN_QKV = DN_HEADS * (2 * DN_DK + DN_DV)
SWA_HEADS = 16
SWA_KV_HEADS = 4
SWA_HD = 64
SWA_BLOCK = 128

LANES = 128
SUBLANES = 8
VMEM_LIMIT = 56 * 1024 * 1024

TOK_TILE = 1024
FF_CHUNK = 256
MOD_ROWS = 16
DN_HEAD_GROUP = 2
NA_ROWS_PER_STEP = 2
DN_PREP_CHUNKS = 9


def _params(*sem):
    return pltpu.CompilerParams(dimension_semantics=sem, vmem_limit_bytes=VMEM_LIMIT)


def _dot(a, b):
    return jnp.dot(a, b, preferred_element_type=F32)


def _dot_nt(a, b):
    return lax.dot_general(a, b, (((1,), (1,)), ((), ())), preferred_element_type=F32)


def _dot_tn(a, b):
    return lax.dot_general(a, b, (((0,), (0,)), ((), ())), preferred_element_type=F32)


def _sigmoid(x):
    return 0.5 + 0.5 * jnp.tanh(0.5 * x)


def _silu(x):
    h = 0.5 * x
    return h + h * jnp.tanh(h)


def _softplus(x):
    return jnp.maximum(x, 0.0) + jnp.log1p(jnp.exp(-jnp.abs(x)))


def _modulate(x, g, shift, scale):
    y = x * lax.rsqrt(jnp.mean(x * x, axis=-1, keepdims=True) + NORM_EPS) * g
    return y * (1.0 + scale) + shift


def _split3(x):
    h1 = x.astype(BF16)
    r1 = x - h1.astype(F32)
    h2 = r1.astype(BF16)
    h3 = (r1 - h2.astype(F32)).astype(BF16)
    return h1, h2, h3


def _split2(x):
    hi = x.astype(BF16)
    return hi, (x - hi.astype(F32)).astype(BF16)


def _adaln_kernel(c_ref, w_ref, b_ref, o_ref):
    s = _silu(c_ref[...]).astype(BF16)
    o_ref[...] = _dot(s, w_ref[...].astype(BF16)) + b_ref[...]


def _adaln_call(c16, ada_w, ada_b):
    depth, d, _ = ada_w.shape
    b4 = ada_b.reshape(depth * N_MOD, 1, d)
    return pl.pallas_call(
        _adaln_kernel,
        out_shape=jax.ShapeDtypeStruct((depth, N_MOD, MOD_ROWS, d), F32),
        grid=(depth, N_MOD),
        in_specs=[
            pl.BlockSpec((MOD_ROWS, d), lambda i, k: (0, 0)),
            pl.BlockSpec((None, d, d), lambda i, k: (i, 0, k)),
            pl.BlockSpec((None, 1, d), lambda i, k: (i * N_MOD + k, 0, 0)),
        ],
        out_specs=pl.BlockSpec((None, None, MOD_ROWS, d), lambda i, k: (i, k, 0, 0)),
        compiler_params=_params("arbitrary", "arbitrary"),
        name="adaln",
    )(c16, ada_w, b4)


def _mod_spec(layer, k, ctx_row, d):
    if ctx_row is None:
        return pl.BlockSpec((None, None, None, 1, d), lambda b, t: (layer, k, b, 0, 0))
    return pl.BlockSpec((None, None, None, 1, d), lambda b, t: (layer, k, ctx_row, 0, 0))


def _const_spec(shape):
    nd = len(shape)
    return pl.BlockSpec(shape, lambda b, t: (0,) * nd, pipeline_mode=pl.Buffered(1))


def _ffn_kernel(x_ref, g_ref, sh_ref, sc_ref, gt_ref, wup_ref, wd_ref, *rest, n_mix, final):
    if n_mix:
        mg_ref, rest = rest[0], rest[1:]
        y_refs, w_refs, rest = rest[:n_mix], rest[n_mix:2 * n_mix], rest[2 * n_mix:]
    if final:
        fg_ref, rest = rest[0], rest[1:]
    o_ref, u_ref, acc_ref = rest
    n_chunks, fc, _ = wd_ref.shape
    f = n_chunks * fc

    x = x_ref[...]
    if n_mix:
        mix = _dot(y_refs[0][...], w_refs[0][...])
        for y_ref, w_ref in zip(y_refs[1:], w_refs[1:]):
            mix += _dot(y_ref[...], w_ref[...])
        x = x + mg_ref[...] * mix
        o_ref[...] = x
    u_ref[...] = _modulate(x, g_ref[...], sh_ref[...], sc_ref[...]).astype(BF16)
    acc_ref[...] = jnp.zeros_like(acc_ref)

    def body(c, carry):
        u = u_ref[...]
        off = pl.multiple_of(c * fc, fc)
        gate = _dot(u, wup_ref[:, pl.ds(off, fc)])
        up = _dot(u, wup_ref[:, pl.ds(f + off, fc)])
        a = (_silu(gate) * up).astype(BF16)
        acc_ref[...] += _dot(a, wd_ref[c])
        return carry

    lax.fori_loop(0, n_chunks, body, 0, unroll=True)
    y = (o_ref[...] if n_mix else x_ref[...]) + 0.5 * gt_ref[...] * acc_ref[...]
    if final:
        y = y * lax.rsqrt(jnp.mean(y * y, axis=-1, keepdims=True) + NORM_EPS) * fg_ref[...]
    o_ref[...] = y


def _ffn_call(h, mods, norm_g3, layer, which, ctx_row, w_up, w_down, mix=None, final_g=None):
    bsz, t, d = h.shape
    tm = min(TOK_TILE, t)
    k0 = 6 if which else 0
    in_specs = [
        pl.BlockSpec((None, tm, d), lambda b, i: (b, i, 0)),
        pl.BlockSpec((None, 1, d), lambda b, i: (layer * 3 + (2 if which else 0), 0, 0)),
        _mod_spec(layer, k0, ctx_row, d),
        _mod_spec(layer, k0 + 1, ctx_row, d),
        _mod_spec(layer, k0 + 2, ctx_row, d),
        pl.BlockSpec((None, None) + w_up.shape[2:], lambda b, i: (layer, which, 0, 0), pipeline_mode=pl.Buffered(1)),
        pl.BlockSpec((None, None) + w_down.shape[2:], lambda b, i: (layer, which, 0, 0, 0),
                     pipeline_mode=pl.Buffered(1)),
    ]
    args = [h, norm_g3, mods, mods, mods, w_up, w_down]
    n_mix = 0
    if mix is not None:
        ys, ws = mix
        n_mix = len(ys)
        in_specs.append(_mod_spec(layer, 5, ctx_row, d))
        in_specs += [pl.BlockSpec((None, tm, y.shape[2]), lambda b, i: (b, i, 0)) for y in ys]
        in_specs += [_const_spec(w.shape) for w in ws]
        args += [mods, *ys, *ws]
    if final_g is not None:
        in_specs.append(pl.BlockSpec((1, d), lambda b, i: (0, 0)))
        args.append(final_g)
    return pl.pallas_call(
        functools.partial(_ffn_kernel, n_mix=n_mix, final=final_g is not None),
        out_shape=jax.ShapeDtypeStruct(h.shape, F32),
        grid=(bsz, t // tm),
        in_specs=in_specs,
        out_specs=pl.BlockSpec((None, tm, d), lambda b, i: (b, i, 0)),
        scratch_shapes=[pltpu.VMEM((tm, d), BF16), pltpu.VMEM((tm, d), F32)],
        compiler_params=_params("parallel", "parallel"),
        name="ffn",
    )(*args)


def _proj_even_kernel(x_ref, g_ref, sh_ref, sc_ref, w_ref, na_ref, dn_ref, z_ref, ab_ref):
    u = _modulate(x_ref[...], g_ref[...], sh_ref[...], sc_ref[...]).astype(BF16)
    c0 = 0
    for o_ref in (na_ref, dn_ref, z_ref, ab_ref):
        c1 = c0 + o_ref.shape[1]
        y = _dot(u, w_ref[:, c0:c1])
        if o_ref is na_ref:
            qw = NA_HEADS * NA_HD
            o_ref[:, :qw] = (y[:, :qw] * (NA_HD ** -0.5 * LOG2E)).astype(BF16)
            o_ref[:, qw:] = y[:, qw:].astype(BF16)
        else:
            o_ref[...] = y
        c0 = c1


def _proj_even_call(h, mods, norm_g3, layer, ctx_row, w_in):
    bsz, t, d = h.shape
    tm = min(TOK_TILE, t)
    widths = (NA_PROJ, DN_QKV, DN_HEADS * DN_DV, LANES)
    dtypes = (BF16, F32, F32, F32)
    return pl.pallas_call(
        _proj_even_kernel,
        out_shape=[jax.ShapeDtypeStruct((bsz, t, w), dt) for w, dt in zip(widths, dtypes)],
        grid=(bsz, t // tm),
        in_specs=[
            pl.BlockSpec((None, tm, d), lambda b, i: (b, i, 0)),
            pl.BlockSpec((None, 1, d), lambda b, i: (layer * 3 + 1, 0, 0)),
            _mod_spec(layer, 3, ctx_row, d),
            _mod_spec(layer, 4, ctx_row, d),
            _const_spec(w_in.shape),
        ],
        out_specs=[pl.BlockSpec((None, tm, w), lambda b, i: (b, i, 0)) for w in widths],
        compiler_params=_params("parallel", "parallel"),
        name="proj_even",
    )(h, norm_g3, mods, mods, w_in)


def _rope_slab(x, cos, sin, first):
    swapped = jnp.where(first, pltpu.roll(x, LANES - 16, 1), pltpu.roll(x, 16, 1))
    return x * cos + swapped * sin


def _proj_odd_kernel(x_ref, g_ref, sh_ref, sc_ref, *rest, with_q):
    if with_q:
        wq_ref, wk_ref, wv_ref, cos_ref, sin_ref, q_ref, k_ref, v_ref = rest
    else:
        wk_ref, wv_ref, k_ref, v_ref = rest
    u = _modulate(x_ref[...], g_ref[...], sh_ref[...], sc_ref[...]).astype(BF16)
    v_ref[...] = _dot(u, wv_ref[...]).astype(BF16)
    k = _dot(u, wk_ref[...])
    if not with_q:
        k_ref[...] = k.astype(BF16)
        return
    cos = cos_ref[...]
    sin = sin_ref[...]
    lane = lax.broadcasted_iota(jnp.int32, cos.shape, 1)
    first = (lane % 32) < 16
    for j in range(k.shape[1] // LANES):
        sl = slice(j * LANES, (j + 1) * LANES)
        k_ref[:, sl] = _rope_slab(k[:, sl], cos, sin, first).astype(BF16)
    q = _dot(u, wq_ref[...])
    for j in range(q.shape[1] // LANES):
        sl = slice(j * LANES, (j + 1) * LANES)
        q_ref[:, sl] = (_rope_slab(q[:, sl], cos, sin, first) * (SWA_HD ** -0.5 * LOG2E)).astype(BF16)


def _proj_odd_call(h, mods, norm_g3, layer, ctx_row, wq, wk, wv, cos, sin):
    bsz, t, d = h.shape
    tm = min(TOK_TILE, t)
    with_q = wq is not None
    in_specs = [
        pl.BlockSpec((None, tm, d), lambda b, i: (b, i, 0)),
        pl.BlockSpec((None, 1, d), lambda b, i: (layer * 3 + 1, 0, 0)),
        _mod_spec(layer, 3, ctx_row, d),
        _mod_spec(layer, 4, ctx_row, d),
    ]
    args = [h, norm_g3, mods, mods]
    widths = []
    if with_q:
        in_specs.append(_const_spec(wq.shape))
        args.append(wq)
        widths.append(wq.shape[1])
    in_specs += [_const_spec(wk.shape), _const_spec(wv.shape)]
    args += [wk, wv]
    widths += [wk.shape[1], wv.shape[1]]
    if with_q:
        in_specs += [pl.BlockSpec((tm, LANES), lambda b, i: (i, 0))] * 2
        args += [cos, sin]
    return pl.pallas_call(
        functools.partial(_proj_odd_kernel, with_q=with_q),
        out_shape=[jax.ShapeDtypeStruct((bsz, t, w), BF16) for w in widths],
        grid=(bsz, t // tm),
        in_specs=in_specs,
        out_specs=[pl.BlockSpec((None, tm, w), lambda b, i: (b, i, 0)) for w in widths],
        compiler_params=_params("parallel", "parallel"),
        name="proj_odd",
    )(*args)


def _softmax2(cols):
    m = jnp.max(functools.reduce(jnp.maximum, cols), axis=-1, keepdims=True)
    p = [jnp.exp2(c - m) for c in cols]
    den = jnp.sum(functools.reduce(jnp.add, p), axis=-1, keepdims=True)
    return jnp.concatenate([x.astype(BF16) for x in p], axis=1), den


def _na_kernel(q_ref, k_ref, v_ref, qc_ref, kc_ref, vc_ref, bias_ref, o_ref, oc_ref, *, rows):
    n_pairs = q_ref.shape[1] // LANES
    tc = qc_ref.shape[0]
    win = NA_WIN_H * GRID_W
    n_bias = win // LANES
    low1 = lax.broadcasted_iota(jnp.int32, (1, LANES), 1) < NA_HD
    low16 = jnp.where(low1, 1.0, 0.0).astype(BF16)
    high16 = jnp.where(low1, 0.0, 1.0).astype(BF16)
    low_q = lax.broadcasted_iota(jnp.int32, (GRID_W, LANES), 1) < NA_HD
    low_c = lax.broadcasted_iota(jnp.int32, (tc, LANES), 1) < NA_HD
    pairs = [slice(hp * LANES, (hp + 1) * LANES) for hp in range(n_pairs)]

    def row_body(i, carry):
        units = []
        for rr in range(NA_ROWS_PER_STEP):
            r = i * NA_ROWS_PER_STEP + rr
            r0 = jnp.clip(r - NA_WIN_H // 2, 0, rows - NA_WIN_H)
            doff = r0 - r + NA_WIN_H - 1
            qs = pl.ds(pl.multiple_of(r * GRID_W, GRID_W), GRID_W)
            ks = pl.ds(pl.multiple_of(r0 * GRID_W, GRID_W), win)
            units += [(hp, sl, doff, qs, ks) for hp, sl in enumerate(pairs)]
        qm = [jnp.concatenate([q_ref[qs, sl] * low16, q_ref[qs, sl] * high16], axis=0) for _, sl, _, qs, _ in units]
        s = [_dot_nt(x, jnp.concatenate([k_ref[ks, sl], kc_ref[:, sl]], axis=0))
             for x, (_, sl, _, _, ks) in zip(qm, units)]
        cols = [[x[:, j * LANES:(j + 1) * LANES] + bias_ref[hp, doff + 2 * j] for j in range(n_bias)]
                + [x[:, j * LANES:(j + 1) * LANES] for j in range(n_bias, x.shape[1] // LANES)]
                for x, (hp, _, doff, _, _) in zip(s, units)]
        pd = [_softmax2(c) for c in cols]
        o = [_dot(p, jnp.concatenate([v_ref[ks, sl], vc_ref[:, sl]], axis=0)) / den
             for (p, den), (_, sl, _, _, ks) in zip(pd, units)]
        for x, (_, sl, _, qs, _) in zip(o, units):
            o_ref[qs, sl] = jnp.where(low_q, x[:GRID_W], x[GRID_W:]).astype(BF16)
        return carry

    lax.fori_loop(0, rows // NA_ROWS_PER_STEP, row_body, 0, unroll=4)

    for sl in pairs:
        halves = []
        for keep16 in (low16, high16):
            s = _dot_nt(qc_ref[:, sl] * keep16, kc_ref[:, sl])
            p, den = _softmax2([s[:, j * LANES:(j + 1) * LANES] for j in range(tc // LANES)])
            halves.append(_dot(p, vc_ref[:, sl]) / den)
        oc_ref[:, sl] = jnp.where(low_c, halves[0], halves[1]).astype(BF16)


def _na_call(qkv, qkvc, bias):
    bsz, t, w3 = qkv.shape
    tc = qkvc.shape[1]
    w = w3 // 3

    def col(j):
        return lambda b: (b, 0, j)

    return pl.pallas_call(
        functools.partial(_na_kernel, rows=t // GRID_W),
        out_shape=[jax.ShapeDtypeStruct((bsz, t, w), BF16), jax.ShapeDtypeStruct((bsz, tc, w), BF16)],
        grid=(bsz,),
        in_specs=[pl.BlockSpec((None, t, w), col(j)) for j in range(3)]
        + [pl.BlockSpec((None, tc, w), col(j)) for j in range(3)]
        + [pl.BlockSpec(bias.shape, lambda b: (0, 0, 0, 0), pipeline_mode=pl.Buffered(1))],
        out_specs=[pl.BlockSpec((None, t, w), lambda b: (b, 0, 0)), pl.BlockSpec((None, tc, w), lambda b: (b, 0, 0))],
        compiler_params=_params("parallel"),
        name="na_attn",
    )(qkv, qkv, qkv, qkvc, qkvc, qkvc, bias)


def _na_bias_table(rpb):
    c = jnp.arange(GRID_W)[:, None]
    kc = jnp.arange(GRID_W)[None, :]
    cstart = jnp.clip(c - NA_WIN_W // 2, 0, GRID_W - NA_WIN_W)
    ok = (kc >= cstart) & (kc < cstart + NA_WIN_W)
    dc = jnp.clip(kc - c + NA_WIN_W - 1, 0, 2 * NA_WIN_W - 2)
    onehot = (dc[None] == jnp.arange(2 * NA_WIN_W - 1)[:, None, None]).astype(F32)
    h = rpb.shape[0]
    nd = 2 * NA_WIN_H - 2
    rp = (rpb.astype(F32) * LOG2E).reshape(h // 2, 2, 2 * NA_WIN_H - 1, 2 * NA_WIN_W - 1)
    t = jnp.einsum('phdx,xck->pdhck', rp, onehot, precision=lax.Precision.HIGHEST)
    t = jnp.where(ok, t, NEG_INF).reshape(h // 2, 2 * NA_WIN_H - 1, 2 * GRID_W, GRID_W)
    return jnp.concatenate([t[:, :nd], t[:, 1:nd + 1]], axis=-1)


def _dn_kernel(qr_ref, kr_ref, vr_ref, z_ref, ab_ref, qcr_ref, kcr_ref, vcr_ref, zc_ref, abc_ref,
               cwq_ref, cwk_ref, cwv_ref, alog_ref, dtb_ref, nw_ref,
               y_ref, yc_ref,
               pad_ref, abs_ref, aq_ref, b_ref, gl_ref, o_ref, s_ref, *, hg):
    c = DN_CHUNK
    t = qr_ref.shape[0]
    tc = qcr_ref.shape[0]
    ncc, ncl = tc // c, t // c
    lat0 = 2 * SUBLANES + tc
    half = DN_CONV // 2
    n_sq = int(math.log2(c)) - 1

    zeros8 = jnp.zeros((3, SUBLANES, LANES), F32)
    pad_ref[:, 0:SUBLANES, :] = zeros8
    pad_ref[:, SUBLANES + tc:lat0, :] = zeros8
    pad_ref[:, lat0 + t:lat0 + t + SUBLANES, :] = zeros8
    abs_ref[0:tc, :] = abc_ref[...]
    abs_ref[tc:tc + t, :] = ab_ref[...]
    s_ref[...] = jnp.zeros_like(s_ref)

    lane = lax.broadcasted_iota(jnp.int32, (c, LANES), 1)
    row = lax.broadcasted_iota(jnp.int32, (c, LANES), 0)
    col = jnp.bitwise_and(lane, c - 1)
    fwd = lane < c
    incl = (fwd & (row >= col)) | (~fwd & (row <= col))
    strict = (fwd & (row > col)) | (~fwd & (row < col))
    diag = row == col
    eye2 = jnp.where(diag, 1.0, 0.0).astype(F32)
    fwd16 = jnp.where(fwd, 1.0, 0.0).astype(BF16)
    bwd16 = jnp.where(fwd, 0.0, 1.0).astype(BF16)
    ri = lax.broadcasted_iota(jnp.int32, (2 * c, c), 0)
    ci_ = lax.broadcasted_iota(jnp.int32, (2 * c, c), 1)
    tri2 = jnp.where(((ri < c) & (ci_ <= ri)) | ((ri >= c) & (ci_ >= ri - c)), 1.0, 0.0).astype(BF16)
    tri6 = jnp.concatenate([tri2, tri2, tri2], axis=1)
    neg_a = -jnp.exp(alog_ref[...])
    dtb = dtb_ref[...]

    def l2n(x):
        return x * lax.rsqrt(jnp.sum(x * x, axis=-1, keepdims=True) + NORM_EPS)

    def bd(m16):
        return jnp.concatenate([m16 * fwd16, m16 * bwd16], axis=0)

    def mm3(lhs_parts, bh, bl):
        bdh = bd(bh)
        rhs = jnp.concatenate([bdh, bdh, bd(bl)], axis=0)
        lhs = jnp.concatenate([jnp.concatenate([ah, al, ah], axis=1) for ah, al in lhs_parts], axis=0)
        return _dot(lhs, rhs)

    def prep_head(hh):
        hid = pl.program_id(1) * hg + hh
        hs = slice(hh * LANES, (hh + 1) * LANES)
        for s, (cr, lr) in enumerate(((qcr_ref, qr_ref), (kcr_ref, kr_ref), (vcr_ref, vr_ref))):
            pad_ref[s, SUBLANES:SUBLANES + tc, :] = cr[:, hs]
            pad_ref[s, lat0:lat0 + t, :] = lr[:, hs]

        def conv(s, cw_ref, base):
            acc = pad_ref[s, pl.ds(base - half, c), :] * cw_ref[0:1, hs]
            for j in range(1, DN_CONV):
                acc += pad_ref[s, pl.ds(base - half + j, c), :] * cw_ref[j:j + 1, hs]
            return _silu(acc)

        def pick(x, idx):
            return jnp.broadcast_to(jnp.sum(jnp.where(lane == idx, x, 0.0), axis=-1, keepdims=True), (c, LANES))

        def stage_inputs(ch):
            base = ch * c + jnp.where(ch < ncc, SUBLANES, 2 * SUBLANES)
            q = l2n(conv(0, cwq_ref, base))
            k = l2n(conv(1, cwk_ref, base))
            v = conv(2, cwv_ref, base)
            abt = abs_ref[pl.ds(pl.multiple_of(ch * c, c), c), :]
            g_all = neg_a * _softplus(abt + dtb)
            b_all = _sigmoid(abt)
            return q * DN_DK ** -0.5, k, v, g_all, b_all

        def stage_decay(g_all):
            parts = []
            for gp in _split3(g_all):
                gp = gp.astype(F32)
                pf = pick(gp, hid)
                pb = pick(gp, DN_HEADS + hid)
                parts.append(jnp.concatenate([jnp.where(strict, jnp.where(fwd, pf, pb), 0.0), pf, pb], axis=1).astype(BF16))
            res = _dot(tri6, jnp.concatenate(parts, axis=0))
            return jnp.where(fwd, res[:c, 0:LANES], res[c:, 0:LANES]), res[:c, LANES:2 * LANES], res[c:, 2 * LANES:]

        def stage_rhs(qs, k, v, gc_f, gc_b, bt_f, bt_b):
            e_f = jnp.exp(gc_f)
            e_b = jnp.exp(gc_b)
            rhs = jnp.concatenate([
                jnp.concatenate([v * bt_f, k * (bt_f * e_f)], axis=1),
                jnp.concatenate([v * bt_b, k * (bt_b * e_b)], axis=1)], axis=0).astype(BF16)
            gl_f = gc_f[c - 1:c, :]
            gl_b = gc_b[0:1, :]
            kd = jnp.concatenate([k * jnp.exp(gl_f - gc_f), k * jnp.exp(gl_b - gc_b)], axis=0).astype(BF16)
            return rhs, kd, (qs * e_f, qs * e_b), (jnp.exp(gl_f), jnp.exp(gl_b))

        def stage_solve(p, rhs):
            sol16 = _dot(bd(p.astype(BF16)), rhs).astype(BF16)
            z16 = jnp.zeros((c, 2 * DN_DV), BF16)
            return jnp.concatenate([jnp.concatenate([sol16[:c], z16], axis=1),
                                    jnp.concatenate([z16, sol16[c:]], axis=1)], axis=0)

        def stage_store(ch, qsol, ksol, q_dec, egl):
            rows = pl.ds(pl.multiple_of(ch * c, c), c)
            for d in range(2):
                lo = 2 * DN_DV * d
                aq_ref[hh, d, ch, 0:DN_DK, :] = ksol[:, lo + DN_DV:lo + 2 * DN_DV].astype(BF16)
                aq_ref[hh, d, ch, DN_DK:DN_DK + c, :] = (q_dec[d] - qsol[:, lo + DN_DV:lo + 2 * DN_DV]).astype(BF16)
                b_ref[hh, d, ch] = ksol[:, lo:lo + DN_DV]
                o_ref[hh, d, rows, :] = qsol[:, lo:lo + DN_DV]
                gl_ref[hh, d, ch] = jnp.broadcast_to(egl[d], (SUBLANES, LANES))

        def body(i, carry):
            chs = [i * DN_PREP_CHUNKS + j for j in range(DN_PREP_CHUNKS)]
            ins = [stage_inputs(ch) for ch in chs]
            dec = [stage_decay(x[3]) for x in ins]
            bts = [(pick(x[4], 2 * DN_HEADS + hid), pick(x[4], 3 * DN_HEADS + hid)) for x in ins]
            decay = [jnp.where(incl, jnp.exp(jnp.where(incl, d[0], 0.0)), 0.0) for d in dec]
            k16 = [x[1].astype(BF16) for x in ins]
            kk = [jnp.concatenate([a, a], axis=0) for a in k16]
            qk16 = [(_dot_nt(x[0].astype(BF16), b) * dc).astype(BF16) for x, b, dc in zip(ins, kk, decay)]
            lmat = [jnp.where(strict, jnp.where(fwd, bt[0], bt[1]) * _dot_nt(a, b) * dc, 0.0)
                    for a, b, bt, dc in zip(k16, kk, bts, decay)]
            lsp = [_split2(x) for x in lmat]
            m = [mm3([s], s[0], s[1]) for s in lsp]
            p = [eye2 - x for x in lmat]
            for s in range(n_sq):
                msp = [_split2(x) for x in m]
                psp = [_split2(x) for x in p]
                if s < n_sq - 1:
                    res = [mm3([a, b], b[0], b[1]) for a, b in zip(psp, msp)]
                    p = [a + r[:c] for a, r in zip(p, res)]
                    m = [r[c:] for r in res]
                else:
                    p = [a + mm3([b], e[0], e[1]) for a, b, e in zip(p, psp, msp)]
            fin = [stage_rhs(x[0], x[1], x[2], d[1], d[2], *bt) for x, d, bt in zip(ins, dec, bts)]
            both = [stage_solve(a, f[0]) for a, f in zip(p, fin)]
            qsol = [_dot(a, b) for a, b in zip(qk16, both)]
            ksol = [_dot_tn(f[1], b) for f, b in zip(fin, both)]
            for ch, a, b, f in zip(chs, qsol, ksol, fin):
                stage_store(ch, a, b, f[2], f[3])
            return carry

        lax.fori_loop(0, (ncc + ncl) // DN_PREP_CHUNKS, body, 0, unroll=True)

    for hh in range(hg):
        prep_head(hh)

    def scan(first, n):
        def body(i, carry):
            chains = [(hh, d, first + i if d == 0 else first + n - 1 - i) for hh in range(hg) for d in range(2)]
            s_mats = [s_ref[hh, d] for hh, d, _ in chains]
            res = [_dot(aq_ref[hh, d, ch], s.astype(BF16)) for s, (hh, d, ch) in zip(s_mats, chains)]
            for s, r, (hh, d, ch) in zip(s_mats, res, chains):
                s_ref[hh, d] = s * gl_ref[hh, d, ch][0:1, :] + (b_ref[hh, d, ch] - r[:DN_DK])
                rows = pl.ds(pl.multiple_of(ch * c, c), c)
                o_ref[hh, d, rows, :] = o_ref[hh, d, rows, :] + r[DN_DK:]
            return carry
        return body

    lax.fori_loop(0, ncc, scan(0, ncc), 0, unroll=True)
    lax.fori_loop(0, ncl, scan(ncc, ncl), 0, unroll=4)

    def gated_norm(o, z):
        return (o * lax.rsqrt(jnp.mean(o * o, axis=-1, keepdims=True) + NORM_EPS) * nw_ref[...] * _silu(z))

    piece = 4 * c
    for hh in range(hg):
        hs = slice(hh * LANES, (hh + 1) * LANES)
        yc_ref[:, hs] = gated_norm(o_ref[hh, 0, 0:tc, :] + o_ref[hh, 1, 0:tc, :], zc_ref[:, hs]).astype(BF16)
        for j in range(t // piece):
            rows = slice(j * piece, (j + 1) * piece)
            orow = slice(tc + j * piece, tc + (j + 1) * piece)
            y_ref[rows, hs] = gated_norm(o_ref[hh, 0, orow, :] + o_ref[hh, 1, orow, :], z_ref[rows, hs]).astype(BF16)


def _dn_call(dn, z, ab, dnc, zc, abc, conv_w, alog, dtb, norm_w):
    bsz, t, _ = dn.shape
    tc = dnc.shape[1]
    h = DN_HEADS
    hg = DN_HEAD_GROUP
    nch = (t + tc) // DN_CHUNK
    c = DN_CHUNK
    wide = hg * LANES

    def slab(rows, j0):
        return pl.BlockSpec((None, rows, wide), lambda b, g: (b, 0, j0 // hg + g))

    def whole(rows):
        return pl.BlockSpec((None, rows, LANES), lambda b, g: (b, 0, 0))

    def cw(j0):
        return pl.BlockSpec((DN_CONV, wide), lambda b, g: (0, j0 // hg + g))

    vec = pl.BlockSpec((1, LANES), lambda b, g: (0, 0))
    return pl.pallas_call(
        functools.partial(_dn_kernel, hg=hg),
        out_shape=[jax.ShapeDtypeStruct((bsz, t, h * DN_DV), BF16), jax.ShapeDtypeStruct((bsz, tc, h * DN_DV), BF16)],
        grid=(bsz, h // hg),
        in_specs=[slab(t, 0), slab(t, h), slab(t, 2 * h), slab(t, 0), whole(t),
                  slab(tc, 0), slab(tc, h), slab(tc, 2 * h), slab(tc, 0), whole(tc),
                  cw(0), cw(h), cw(2 * h), vec, vec, vec],
        out_specs=[slab(t, 0), slab(tc, 0)],
        scratch_shapes=[
            pltpu.VMEM((3, 3 * SUBLANES + tc + t, LANES), F32),
            pltpu.VMEM((tc + t, LANES), F32),
            pltpu.VMEM((hg, 2, nch, DN_DK + c, DN_DV), BF16),
            pltpu.VMEM((hg, 2, nch, DN_DK, DN_DV), F32),
            pltpu.VMEM((hg, 2, nch, SUBLANES, LANES), F32),
            pltpu.VMEM((hg, 2, tc + t, DN_DV), F32),
            pltpu.VMEM((hg, 2, DN_DK, DN_DV), F32),
        ],
        compiler_params=_params("parallel", "parallel"),
        name="deltanet",
    )(dn, dn, dn, z, ab, dnc, dnc, dnc, zc, abc, conv_w, conv_w, conv_w, alog, dtb, norm_w)


def _swa_kernel(sink_ref, q_ref, k_ref, v_ref, kc_ref, vc_ref, o_ref, *, n_blocks):
    blk = SWA_BLOCK
    n_kv_pairs = k_ref.shape[1] // LANES
    slabs = q_ref.shape[1] // LANES // n_kv_pairs
    rows = slabs * blk
    low = lax.broadcasted_iota(jnp.int32, (rows, LANES), 1) < SWA_HD
    low_blk = lax.broadcasted_iota(jnp.int32, (blk, LANES), 1) < SWA_HD
    low1 = lax.broadcasted_iota(jnp.int32, (1, LANES), 1) < SWA_HD
    low16 = jnp.where(low1, 1.0, 0.0).astype(BF16)
    high16 = jnp.where(low1, 0.0, 1.0).astype(BF16)
    qi = lax.broadcasted_iota(jnp.int32, (rows, blk), 0) % blk
    kj = lax.broadcasted_iota(jnp.int32, (rows, blk), 1)

    def block_body(n, carry):
        lo = jnp.maximum(n - 1, 0)
        hi = jnp.minimum(n + 1, n_blocks - 1)
        ok_lo = (kj >= qi) & (n > 0)
        ok_hi = (kj <= qi) & (n < n_blocks - 1)

        def tok(i):
            return pl.ds(pl.multiple_of(i * blk, blk), blk)

        def kv_rows(ref, ctx_ref, kvl):
            return jnp.concatenate([ref[tok(lo), kvl], ref[tok(n), kvl], ref[tok(hi), kvl], ctx_ref[:, kvl]], axis=0)

        def softmax(s, p, half):
            cols = [s[:, j * blk:(j + 1) * blk] for j in range(s.shape[1] // blk)]
            cols[0] = jnp.where(ok_lo, cols[0], NEG_INF)
            cols[2] = jnp.where(ok_hi, cols[2], NEG_INF)
            probs, e_sink = [], []
            for j in range(slabs):
                cj = [c[j * blk:(j + 1) * blk] for c in cols]
                sink = sink_ref[(2 * p + half) * slabs + j] * LOG2E
                m = jnp.maximum(jnp.max(functools.reduce(jnp.maximum, cj), axis=-1, keepdims=True), sink)
                probs.append(jnp.concatenate([jnp.exp2(c - m).astype(BF16) for c in cj], axis=1))
                e_sink.append(jnp.exp2(sink - m))
            return jnp.concatenate(probs, axis=0), e_sink

        units = [(p, half) for p in range(n_kv_pairs) for half in range(2)]
        kvls = [slice(p * LANES, (p + 1) * LANES) for p in range(n_kv_pairs)]
        k_all = [kv_rows(k_ref, kc_ref, kvl) for kvl in kvls]
        v_all = [kv_rows(v_ref, vc_ref, kvl) for kvl in kvls]
        qs = [jnp.concatenate([q_ref[tok(n), (p * slabs + j) * LANES:(p * slabs + j + 1) * LANES]
                               for j in range(slabs)], axis=0) for p in range(n_kv_pairs)]
        s = [_dot_nt(qs[p] * (high16 if half else low16), k_all[p]) for p, half in units]
        pe = [softmax(x, p, half) for x, (p, half) in zip(s, units)]
        r = [_dot(x[0], v_all[p] * (high16 if half else low16) + (low16 if half else high16))
             for x, (p, half) in zip(pe, units)]
        res = [jnp.concatenate(
            [y[j * blk:(j + 1) * blk] + jnp.where(low_blk if half else ~low_blk, x[1][j], 0.0) for j in range(slabs)],
            axis=0) for y, x, (p, half) in zip(r, pe, units)]
        for p in range(n_kv_pairs):
            num = jnp.where(low, res[2 * p], res[2 * p + 1])
            den = pltpu.roll(jnp.where(low, res[2 * p + 1], res[2 * p]), SWA_HD, 1)
            o = (num / den).astype(BF16)
            for j in range(slabs):
                o_ref[tok(n), (p * slabs + j) * LANES:(p * slabs + j + 1) * LANES] = o[j * blk:(j + 1) * blk]
        return carry

    lax.fori_loop(0, n_blocks, block_body, 0, unroll=4)


def _swa_call(sink, q, k, v, kc, vc):
    bsz, t, wq = q.shape
    wk = k.shape[2]
    tc = kc.shape[1]
    return pl.pallas_call(
        functools.partial(_swa_kernel, n_blocks=t // SWA_BLOCK),
        out_shape=jax.ShapeDtypeStruct((bsz, t, wq), BF16),
        grid=(bsz,),
        in_specs=[
            pl.BlockSpec(memory_space=pltpu.SMEM),
            pl.BlockSpec((None, t, wq), lambda b: (b, 0, 0)),
            pl.BlockSpec((None, t, wk), lambda b: (b, 0, 0)),
            pl.BlockSpec((None, t, wk), lambda b: (b, 0, 0)),
            pl.BlockSpec((None, tc, wk), lambda b: (b, 0, 0)),
            pl.BlockSpec((None, tc, wk), lambda b: (b, 0, 0)),
        ],
        out_specs=pl.BlockSpec((None, t, wq), lambda b: (b, 0, 0)),
        compiler_params=_params("parallel"),
        name="swa_attn",
    )(sink, q, k, v, kc, vc)


def _swa_slab_order(w, axis):
    rep = SWA_HEADS // SWA_KV_HEADS
    shape = w.shape
    split = shape[:axis] + (SWA_KV_HEADS // 2, 2, rep, SWA_HD) + shape[axis + 1:]
    return jnp.swapaxes(w.reshape(split), axis + 1, axis + 2).reshape(shape)


def _rope_tables(t):
    pos = jnp.arange(t)
    half = SWA_HD // 2
    inv = jnp.power(ROPE_BASE, -jnp.arange(0, half, 2, dtype=F32) / half)
    ang_r = (pos // GRID_W).astype(F32)[:, None] * inv
    ang_c = (pos % GRID_W).astype(F32)[:, None] * inv
    cos = jnp.concatenate([jnp.cos(ang_r)] * 2 + [jnp.cos(ang_c)] * 2, axis=-1)
    sin = jnp.concatenate([-jnp.sin(ang_r), jnp.sin(ang_r), -jnp.sin(ang_c), jnp.sin(ang_c)], axis=-1)
    return jnp.tile(cos, (1, LANES // SWA_HD)), jnp.tile(sin, (1, LANES // SWA_HD))


def kernel(x, c, ctx, c_ctx, ada_w, ada_b, norm_g, ffn_w_up, ffn_w_down, even_w_in, even_w_out, na_rpb,
           dn_conv_w, dn_a_log, dn_dt_bias, dn_norm_w, odd_w_in, odd_w_out, swa_sink, final_norm_g):
    bsz, t, d = x.shape
    tc = ctx.shape[1]
    depth = ada_w.shape[0]
    ctx_row = bsz

    c16 = jnp.concatenate([c, c_ctx[None, :], jnp.zeros((MOD_ROWS - bsz - 1, d), F32)], axis=0)
    mods = _adaln_call(c16, ada_w, ada_b).reshape(depth, N_MOD, MOD_ROWS, 1, d)
    norm_g3 = norm_g.reshape(depth * 3, 1, d)

    w_up = ffn_w_up.astype(BF16)
    f = ffn_w_down.shape[2]
    w_down = ffn_w_down.astype(BF16).reshape(depth, 2, f // FF_CHUNK, FF_CHUNK, d)

    h = x
    hc = ctx.reshape(1, bsz * tc, d)
    for i in range(depth):
        need_ctx = i < depth - 1
        j = i // 2
        h = _ffn_call(h, mods, norm_g3, i, 0, None, w_up, w_down)
        hc = _ffn_call(hc, mods, norm_g3, i, 0, ctx_row, w_up, w_down)
        mix_c = None
        if i % 2 == 0:
            w_in = even_w_in[j]
            w_in = jnp.pad(w_in, ((0, 0), (0, -w_in.shape[1] % LANES))).astype(BF16)
            na, dn, z, ab = _proj_even_call(h, mods, norm_g3, i, None, w_in)
            nac, dnc, zc, abc = [a.reshape(bsz, tc, a.shape[-1])
                                 for a in _proj_even_call(hc, mods, norm_g3, i, ctx_row, w_in)]
            y_na, y_na_c = _na_call(na, nac, _na_bias_table(na_rpb[j]))
            lane_pad = (0, LANES - 2 * DN_HEADS)
            alog = jnp.pad(dn_a_log[j].reshape(-1), lane_pad)[None, :]
            dtb = jnp.pad(dn_dt_bias[j].reshape(-1), lane_pad)[None, :]
            y_dn, y_dn_c = _dn_call(dn, z, ab, dnc, zc, abc, dn_conv_w[j], alog, dtb, dn_norm_w[j][None, :])
            w_out = even_w_out[j].astype(BF16)
            ws = [w_out[:NA_HEADS * NA_HD], w_out[NA_HEADS * NA_HD:]]
            mix = ([y_na, y_dn], ws)
            if need_ctx:
                mix_c = ([y.reshape(1, bsz * tc, y.shape[-1]) for y in (y_na_c, y_dn_c)], ws)
        else:
            if need_ctx:
                raise NotImplementedError("context queries of a windowed layer are only needed before a later layer")
            w_in = odd_w_in[j]
            qw = SWA_HEADS * SWA_HD
            kw = SWA_KV_HEADS * SWA_HD
            wq = _swa_slab_order(w_in[:, :qw].astype(BF16), 1)
            wk = w_in[:, qw:qw + kw].astype(BF16)
            wv = w_in[:, qw + kw:].astype(BF16)
            cos, sin = _rope_tables(t)
            q, k, v = _proj_odd_call(h, mods, norm_g3, i, None, wq, wk, wv, cos, sin)
            kc, vc = [a.reshape(bsz, tc, kw) for a in _proj_odd_call(hc, mods, norm_g3, i, ctx_row, None, wk, wv, None, None)]
            mix = ([_swa_call(swa_sink[j], q, k, v, kc, vc)], [_swa_slab_order(odd_w_out[j].astype(BF16), 0)])
        last = i == depth - 1
        h = _ffn_call(h, mods, norm_g3, i, 1, None, w_up, w_down, mix=mix,
                      final_g=final_norm_g[None, :] if last else None)
        if need_ctx:
            hc = _ffn_call(hc, mods, norm_g3, i, 1, ctx_row, w_up, w_down, mix=mix_c)
    return h
```

```python
import functools
import math

import jax
import jax.numpy as jnp
from jax import lax
from jax.experimental import pallas as pl
from jax.experimental.pallas import tpu as pltpu

F32 = jnp.float32
BF16 = jnp.bfloat16

GRID_W = 64
N_MOD = 9
NORM_EPS = 1e-6
NEG_INF = -1e30
LOG2E = math.log2(math.e)
ROPE_BASE = 10000.0
NA_HEADS = 8
NA_HD = 64
NA_WIN_H = 8
NA_WIN_W = 16
NA_PROJ = 3 * NA_HEADS * NA_HD
DN_HEADS = 4
DN_DK = 128
DN_DV = 128
DN_CONV = 5
DN_CHUNK = 64
DN_QKV = DN_HEADS * (2 * DN_DK + DN_DV)
SWA_HEADS = 16
SWA_KV_HEADS = 4
SWA_HD = 64
SWA_BLOCK = 128

LANES = 128
SUBLANES = 8
VMEM_LIMIT = 56 * 1024 * 1024

TOK_TILE = 1024
FF_CHUNK = 256
MOD_ROWS = 16
DN_HEAD_GROUP = 2
NA_ROWS_PER_STEP = 2
DN_PREP_CHUNKS = 9


def _params(*sem):
    return pltpu.CompilerParams(dimension_semantics=sem, vmem_limit_bytes=VMEM_LIMIT)


def _dot(a, b):
    return jnp.dot(a, b, preferred_element_type=F32)


def _dot_nt(a, b):
    return lax.dot_general(a, b, (((1,), (1,)), ((), ())), preferred_element_type=F32)


def _dot_tn(a, b):
    return lax.dot_general(a, b, (((0,), (0,)), ((), ())), preferred_element_type=F32)


def _sigmoid(x):
    return 0.5 + 0.5 * jnp.tanh(0.5 * x)


def _silu(x):
    h = 0.5 * x
    return h + h * jnp.tanh(h)


def _softplus(x):
    return jnp.maximum(x, 0.0) + jnp.log1p(jnp.exp(-jnp.abs(x)))


def _modulate(x, g, shift, scale):
    y = x * lax.rsqrt(jnp.mean(x * x, axis=-1, keepdims=True) + NORM_EPS) * g
    return y * (1.0 + scale) + shift


def _split3(x):
    h1 = x.astype(BF16)
    r1 = x - h1.astype(F32)
    h2 = r1.astype(BF16)
    h3 = (r1 - h2.astype(F32)).astype(BF16)
    return h1, h2, h3


def _split2(x):
    hi = x.astype(BF16)
    return hi, (x - hi.astype(F32)).astype(BF16)


def _adaln_kernel(c_ref, w_ref, b_ref, o_ref):
    s = _silu(c_ref[...]).astype(BF16)
    o_ref[...] = _dot(s, w_ref[...].astype(BF16)) + b_ref[...]


def _adaln_call(c16, ada_w, ada_b):
    depth, d, _ = ada_w.shape
    b4 = ada_b.reshape(depth * N_MOD, 1, d)
    return pl.pallas_call(
        _adaln_kernel,
        out_shape=jax.ShapeDtypeStruct((depth, N_MOD, MOD_ROWS, d), F32),
        grid=(depth, N_MOD),
        in_specs=[
            pl.BlockSpec((MOD_ROWS, d), lambda i, k: (0, 0)),
            pl.BlockSpec((None, d, d), lambda i, k: (i, 0, k)),
            pl.BlockSpec((None, 1, d), lambda i, k: (i * N_MOD + k, 0, 0)),
        ],
        out_specs=pl.BlockSpec((None, None, MOD_ROWS, d), lambda i, k: (i, k, 0, 0)),
        compiler_params=_params("arbitrary", "arbitrary"),
        name="adaln",
    )(c16, ada_w, b4)


def _mod_spec(layer, k, ctx_row, d):
    if ctx_row is None:
        return pl.BlockSpec((None, None, None, 1, d), lambda b, t: (layer, k, b, 0, 0))
    return pl.BlockSpec((None, None, None, 1, d), lambda b, t: (layer, k, ctx_row, 0, 0))


def _const_spec(shape):
    nd = len(shape)
    return pl.BlockSpec(shape, lambda b, t: (0,) * nd, pipeline_mode=pl.Buffered(1))


def _ffn_kernel(x_ref, g_ref, sh_ref, sc_ref, gt_ref, wup_ref, wd_ref, *rest, n_mix, final):
    if n_mix:
        mg_ref, rest = rest[0], rest[1:]
        y_refs, w_refs, rest = rest[:n_mix], rest[n_mix:2 * n_mix], rest[2 * n_mix:]
    if final:
        fg_ref, rest = rest[0], rest[1:]
    o_ref, u_ref, acc_ref = rest
    n_chunks, fc, _ = wd_ref.shape
    f = n_chunks * fc

    x = x_ref[...]
    if n_mix:
        mix = _dot(y_refs[0][...], w_refs[0][...])
        for y_ref, w_ref in zip(y_refs[1:], w_refs[1:]):
            mix += _dot(y_ref[...], w_ref[...])
        x = x + mg_ref[...] * mix
        o_ref[...] = x
    u_ref[...] = _modulate(x, g_ref[...], sh_ref[...], sc_ref[...]).astype(BF16)
    acc_ref[...] = jnp.zeros_like(acc_ref)

    def body(c, carry):
        u = u_ref[...]
        off = pl.multiple_of(c * fc, fc)
        gate = _dot(u, wup_ref[:, pl.ds(off, fc)])
        up = _dot(u, wup_ref[:, pl.ds(f + off, fc)])
        a = (_silu(gate) * up).astype(BF16)
        acc_ref[...] += _dot(a, wd_ref[c].astype(BF16))
        return carry

    lax.fori_loop(0, n_chunks, body, 0, unroll=True)
    y = (o_ref[...] if n_mix else x_ref[...]) + 0.5 * gt_ref[...] * acc_ref[...]
    if final:
        y = y * lax.rsqrt(jnp.mean(y * y, axis=-1, keepdims=True) + NORM_EPS) * fg_ref[...]
    o_ref[...] = y


def _ffn_call(h, mods, norm_g3, layer, which, ctx_row, w_up, w_down, mix=None, final_g=None):
    bsz, t, d = h.shape
    tm = min(TOK_TILE, t)
    k0 = 6 if which else 0
    in_specs = [
        pl.BlockSpec((None, tm, d), lambda b, i: (b, i, 0)),
        pl.BlockSpec((None, 1, d), lambda b, i: (layer * 3 + (2 if which else 0), 0, 0)),
        _mod_spec(layer, k0, ctx_row, d),
        _mod_spec(layer, k0 + 1, ctx_row, d),
        _mod_spec(layer, k0 + 2, ctx_row, d),
        pl.BlockSpec((None, None) + w_up.shape[2:], lambda b, i: (layer, which, 0, 0), pipeline_mode=pl.Buffered(1)),
        pl.BlockSpec((None, None) + w_down.shape[2:], lambda b, i: (layer, which, 0, 0, 0),
                     pipeline_mode=pl.Buffered(1)),
    ]
    args = [h, norm_g3, mods, mods, mods, w_up, w_down]
    n_mix = 0
    if mix is not None:
        ys, ws = mix
        n_mix = len(ys)
        in_specs.append(_mod_spec(layer, 5, ctx_row, d))
        in_specs += [pl.BlockSpec((None, tm, y.shape[2]), lambda b, i: (b, i, 0)) for y in ys]
        in_specs += [_const_spec(w.shape) for w in ws]
        args += [mods, *ys, *ws]
    if final_g is not None:
        in_specs.append(pl.BlockSpec((1, d), lambda b, i: (0, 0)))
        args.append(final_g)
    return pl.pallas_call(
        functools.partial(_ffn_kernel, n_mix=n_mix, final=final_g is not None),
        out_shape=jax.ShapeDtypeStruct(h.shape, F32),
        grid=(bsz, t // tm),
        in_specs=in_specs,
        out_specs=pl.BlockSpec((None, tm, d), lambda b, i: (b, i, 0)),
        scratch_shapes=[pltpu.VMEM((tm, d), BF16), pltpu.VMEM((tm, d), F32)],
        compiler_params=_params("parallel", "parallel"),
        name="ffn",
    )(*args)


def _proj_even_kernel(x_ref, g_ref, sh_ref, sc_ref, w_ref, na_ref, dn_ref, z_ref, ab_ref):
    u = _modulate(x_ref[...], g_ref[...], sh_ref[...], sc_ref[...]).astype(BF16)
    c0 = 0
    for o_ref in (na_ref, dn_ref, z_ref, ab_ref):
        c1 = c0 + o_ref.shape[1]
        y = _dot(u, w_ref[:, c0:c1])
        if o_ref is na_ref:
            qw = NA_HEADS * NA_HD
            o_ref[:, :qw] = (y[:, :qw] * (NA_HD ** -0.5 * LOG2E)).astype(BF16)
            o_ref[:, qw:] = y[:, qw:].astype(BF16)
        else:
            o_ref[...] = y
        c0 = c1


def _proj_even_call(h, mods, norm_g3, layer, ctx_row, w_in):
    bsz, t, d = h.shape
    tm = min(TOK_TILE, t)
    widths = (NA_PROJ, DN_QKV, DN_HEADS * DN_DV, LANES)
    dtypes = (BF16, F32, F32, F32)
    return pl.pallas_call(
        _proj_even_kernel,
        out_shape=[jax.ShapeDtypeStruct((bsz, t, w), dt) for w, dt in zip(widths, dtypes)],
        grid=(bsz, t // tm),
        in_specs=[
            pl.BlockSpec((None, tm, d), lambda b, i: (b, i, 0)),
            pl.BlockSpec((None, 1, d), lambda b, i: (layer * 3 + 1, 0, 0)),
            _mod_spec(layer, 3, ctx_row, d),
            _mod_spec(layer, 4, ctx_row, d),
            _const_spec(w_in.shape),
        ],
        out_specs=[pl.BlockSpec((None, tm, w), lambda b, i: (b, i, 0)) for w in widths],
        compiler_params=_params("parallel", "parallel"),
        name="proj_even",
    )(h, norm_g3, mods, mods, w_in)


def _rope_slab(x, cos, sin, first):
    swapped = jnp.where(first, pltpu.roll(x, LANES - 16, 1), pltpu.roll(x, 16, 1))
    return x * cos + swapped * sin


def _proj_odd_kernel(x_ref, g_ref, sh_ref, sc_ref, *rest, with_q):
    if with_q:
        wq_ref, wk_ref, wv_ref, cos_ref, sin_ref, q_ref, k_ref, v_ref = rest
    else:
        wk_ref, wv_ref, k_ref, v_ref = rest
    u = _modulate(x_ref[...], g_ref[...], sh_ref[...], sc_ref[...]).astype(BF16)
    v_ref[...] = _dot(u, wv_ref[...]).astype(BF16)
    k = _dot(u, wk_ref[...])
    if not with_q:
        k_ref[...] = k.astype(BF16)
        return
    cos = cos_ref[...]
    sin = sin_ref[...]
    lane = lax.broadcasted_iota(jnp.int32, cos.shape, 1)
    first = (lane % 32) < 16
    for j in range(k.shape[1] // LANES):
        sl = slice(j * LANES, (j + 1) * LANES)
        k_ref[:, sl] = _rope_slab(k[:, sl], cos, sin, first).astype(BF16)
    q = _dot(u, wq_ref[...])
    for j in range(q.shape[1] // LANES):
        sl = slice(j * LANES, (j + 1) * LANES)
        q_ref[:, sl] = (_rope_slab(q[:, sl], cos, sin, first) * (SWA_HD ** -0.5 * LOG2E)).astype(BF16)


def _proj_odd_call(h, mods, norm_g3, layer, ctx_row, wq, wk, wv, cos, sin):
    bsz, t, d = h.shape
    tm = min(TOK_TILE, t)
    with_q = wq is not None
    in_specs = [
        pl.BlockSpec((None, tm, d), lambda b, i: (b, i, 0)),
        pl.BlockSpec((None, 1, d), lambda b, i: (layer * 3 + 1, 0, 0)),
        _mod_spec(layer, 3, ctx_row, d),
        _mod_spec(layer, 4, ctx_row, d),
    ]
    args = [h, norm_g3, mods, mods]
    widths = []
    if with_q:
        in_specs.append(_const_spec(wq.shape))
        args.append(wq)
        widths.append(wq.shape[1])
    in_specs += [_const_spec(wk.shape), _const_spec(wv.shape)]
    args += [wk, wv]
    widths += [wk.shape[1], wv.shape[1]]
    if with_q:
        in_specs += [pl.BlockSpec((tm, LANES), lambda b, i: (i, 0))] * 2
        args += [cos, sin]
    return pl.pallas_call(
        functools.partial(_proj_odd_kernel, with_q=with_q),
        out_shape=[jax.ShapeDtypeStruct((bsz, t, w), BF16) for w in widths],
        grid=(bsz, t // tm),
        in_specs=in_specs,
        out_specs=[pl.BlockSpec((None, tm, w), lambda b, i: (b, i, 0)) for w in widths],
        compiler_params=_params("parallel", "parallel"),
        name="proj_odd",
    )(*args)


def _softmax2(cols):
    m = jnp.max(functools.reduce(jnp.maximum, cols), axis=-1, keepdims=True)
    p = [jnp.exp2(c - m) for c in cols]
    den = jnp.sum(functools.reduce(jnp.add, p), axis=-1, keepdims=True)
    return jnp.concatenate([x.astype(BF16) for x in p], axis=1), den


def _na_kernel(q_ref, k_ref, v_ref, qc_ref, kc_ref, vc_ref, bias_ref, o_ref, oc_ref, *, rows):
    n_pairs = q_ref.shape[1] // LANES
    tc = qc_ref.shape[0]
    win = NA_WIN_H * GRID_W
    n_bias = win // LANES
    low1 = lax.broadcasted_iota(jnp.int32, (1, LANES), 1) < NA_HD
    low16 = jnp.where(low1, 1.0, 0.0).astype(BF16)
    high16 = jnp.where(low1, 0.0, 1.0).astype(BF16)
    low_q = lax.broadcasted_iota(jnp.int32, (GRID_W, LANES), 1) < NA_HD
    low_c = lax.broadcasted_iota(jnp.int32, (tc, LANES), 1) < NA_HD
    pairs = [slice(hp * LANES, (hp + 1) * LANES) for hp in range(n_pairs)]

    def row_body(i, carry):
        units = []
        for rr in range(NA_ROWS_PER_STEP):
            r = i * NA_ROWS_PER_STEP + rr
            r0 = jnp.clip(r - NA_WIN_H // 2, 0, rows - NA_WIN_H)
            doff = r0 - r + NA_WIN_H - 1
            qs = pl.ds(pl.multiple_of(r * GRID_W, GRID_W), GRID_W)
            ks = pl.ds(pl.multiple_of(r0 * GRID_W, GRID_W), win)
            units += [(hp, sl, doff, qs, ks) for hp, sl in enumerate(pairs)]
        qm = [jnp.concatenate([q_ref[qs, sl] * low16, q_ref[qs, sl] * high16], axis=0) for _, sl, _, qs, _ in units]
        s = [_dot_nt(x, jnp.concatenate([k_ref[ks, sl], kc_ref[:, sl]], axis=0))
             for x, (_, sl, _, _, ks) in zip(qm, units)]
        cols = [[x[:, j * LANES:(j + 1) * LANES] + bias_ref[hp, doff + 2 * j] for j in range(n_bias)]
                + [x[:, j * LANES:(j + 1) * LANES] for j in range(n_bias, x.shape[1] // LANES)]
                for x, (hp, _, doff, _, _) in zip(s, units)]
        pd = [_softmax2(c) for c in cols]
        o = [_dot(p, jnp.concatenate([v_ref[ks, sl], vc_ref[:, sl]], axis=0)) / den
             for (p, den), (_, sl, _, _, ks) in zip(pd, units)]
        for x, (_, sl, _, qs, _) in zip(o, units):
            o_ref[qs, sl] = jnp.where(low_q, x[:GRID_W], x[GRID_W:]).astype(BF16)
        return carry

    lax.fori_loop(0, rows // NA_ROWS_PER_STEP, row_body, 0, unroll=4)

    for sl in pairs:
        halves = []
        for keep16 in (low16, high16):
            s = _dot_nt(qc_ref[:, sl] * keep16, kc_ref[:, sl])
            p, den = _softmax2([s[:, j * LANES:(j + 1) * LANES] for j in range(tc // LANES)])
            halves.append(_dot(p, vc_ref[:, sl]) / den)
        oc_ref[:, sl] = jnp.where(low_c, halves[0], halves[1]).astype(BF16)


def _na_call(qkv, qkvc, bias):
    bsz, t, w3 = qkv.shape
    tc = qkvc.shape[1]
    w = w3 // 3

    def col(j):
        return lambda b: (b, 0, j)

    return pl.pallas_call(
        functools.partial(_na_kernel, rows=t // GRID_W),
        out_shape=[jax.ShapeDtypeStruct((bsz, t, w), BF16), jax.ShapeDtypeStruct((bsz, tc, w), BF16)],
        grid=(bsz,),
        in_specs=[pl.BlockSpec((None, t, w), col(j)) for j in range(3)]
        + [pl.BlockSpec((None, tc, w), col(j)) for j in range(3)]
        + [pl.BlockSpec(bias.shape, lambda b: (0, 0, 0, 0), pipeline_mode=pl.Buffered(1))],
        out_specs=[pl.BlockSpec((None, t, w), lambda b: (b, 0, 0)), pl.BlockSpec((None, tc, w), lambda b: (b, 0, 0))],
        compiler_params=_params("parallel"),
        name="na_attn",
    )(qkv, qkv, qkv, qkvc, qkvc, qkvc, bias)


def _na_bias_table(rpb):
    c = jnp.arange(GRID_W)[:, None]
    kc = jnp.arange(GRID_W)[None, :]
    cstart = jnp.clip(c - NA_WIN_W // 2, 0, GRID_W - NA_WIN_W)
    ok = (kc >= cstart) & (kc < cstart + NA_WIN_W)
    dc = jnp.clip(kc - c + NA_WIN_W - 1, 0, 2 * NA_WIN_W - 2)
    onehot = (dc[None] == jnp.arange(2 * NA_WIN_W - 1)[:, None, None]).astype(F32)
    h = rpb.shape[0]
    nd = 2 * NA_WIN_H - 2
    rp = (rpb.astype(F32) * LOG2E).reshape(h // 2, 2, 2 * NA_WIN_H - 1, 2 * NA_WIN_W - 1)
    t = jnp.einsum('phdx,xck->pdhck', rp, onehot, precision=lax.Precision.HIGHEST)
    t = jnp.where(ok, t, NEG_INF).reshape(h // 2, 2 * NA_WIN_H - 1, 2 * GRID_W, GRID_W)
    return jnp.concatenate([t[:, :nd], t[:, 1:nd + 1]], axis=-1)


def _dn_kernel(qr_ref, kr_ref, vr_ref, z_ref, ab_ref, qcr_ref, kcr_ref, vcr_ref, zc_ref, abc_ref,
               cwq_ref, cwk_ref, cwv_ref, alog_ref, dtb_ref, nw_ref,
               y_ref, yc_ref,
               pad_ref, abs_ref, aq_ref, b_ref, gl_ref, o_ref, s_ref, *, hg):
    c = DN_CHUNK
    t = qr_ref.shape[0]
    tc = qcr_ref.shape[0]
    ncc, ncl = tc // c, t // c
    lat0 = 2 * SUBLANES + tc
    half = DN_CONV // 2
    n_sq = int(math.log2(c)) - 1

    zeros8 = jnp.zeros((3, SUBLANES, LANES), F32)
    pad_ref[:, 0:SUBLANES, :] = zeros8
    pad_ref[:, SUBLANES + tc:lat0, :] = zeros8
    pad_ref[:, lat0 + t:lat0 + t + SUBLANES, :] = zeros8
    abs_ref[0:tc, :] = abc_ref[...]
    abs_ref[tc:tc + t, :] = ab_ref[...]
    s_ref[...] = jnp.zeros_like(s_ref)

    lane = lax.broadcasted_iota(jnp.int32, (c, LANES), 1)
    row = lax.broadcasted_iota(jnp.int32, (c, LANES), 0)
    col = jnp.bitwise_and(lane, c - 1)
    fwd = lane < c
    incl = (fwd & (row >= col)) | (~fwd & (row <= col))
    strict = (fwd & (row > col)) | (~fwd & (row < col))
    diag = row == col
    eye2 = jnp.where(diag, 1.0, 0.0).astype(F32)
    fwd16 = jnp.where(fwd, 1.0, 0.0).astype(BF16)
    bwd16 = jnp.where(fwd, 0.0, 1.0).astype(BF16)
    ri = lax.broadcasted_iota(jnp.int32, (2 * c, c), 0)
    ci_ = lax.broadcasted_iota(jnp.int32, (2 * c, c), 1)
    tri2 = jnp.where(((ri < c) & (ci_ <= ri)) | ((ri >= c) & (ci_ >= ri - c)), 1.0, 0.0).astype(BF16)
    tri6 = jnp.concatenate([tri2, tri2, tri2], axis=1)
    neg_a = -jnp.exp(alog_ref[...])
    dtb = dtb_ref[...]

    def l2n(x):
        return x * lax.rsqrt(jnp.sum(x * x, axis=-1, keepdims=True) + NORM_EPS)

    def bd(m16):
        return jnp.concatenate([m16 * fwd16, m16 * bwd16], axis=0)

    def mm3(lhs_parts, bh, bl):
        bdh = bd(bh)
        rhs = jnp.concatenate([bdh, bdh, bd(bl)], axis=0)
        lhs = jnp.concatenate([jnp.concatenate([ah, al, ah], axis=1) for ah, al in lhs_parts], axis=0)
        return _dot(lhs, rhs)

    def prep_head(hh):
        hid = pl.program_id(1) * hg + hh
        hs = slice(hh * LANES, (hh + 1) * LANES)
        for s, (cr, lr) in enumerate(((qcr_ref, qr_ref), (kcr_ref, kr_ref), (vcr_ref, vr_ref))):
            pad_ref[s, SUBLANES:SUBLANES + tc, :] = cr[:, hs]
            pad_ref[s, lat0:lat0 + t, :] = lr[:, hs]

        def conv(s, cw_ref, base):
            acc = pad_ref[s, pl.ds(base - half, c), :] * cw_ref[0:1, hs]
            for j in range(1, DN_CONV):
                acc += pad_ref[s, pl.ds(base - half + j, c), :] * cw_ref[j:j + 1, hs]
            return _silu(acc)

        def pick(x, idx):
            return jnp.broadcast_to(jnp.sum(jnp.where(lane == idx, x, 0.0), axis=-1, keepdims=True), (c, LANES))

        def stage_inputs(ch):
            base = ch * c + jnp.where(ch < ncc, SUBLANES, 2 * SUBLANES)
            q = l2n(conv(0, cwq_ref, base))
            k = l2n(conv(1, cwk_ref, base))
            v = conv(2, cwv_ref, base)
            abt = abs_ref[pl.ds(pl.multiple_of(ch * c, c), c), :]
            g_all = neg_a * _softplus(abt + dtb)
            b_all = _sigmoid(abt)
            return q * DN_DK ** -0.5, k, v, g_all, b_all

        def stage_decay(g_all):
            parts = []
            for gp in _split3(g_all):
                gp = gp.astype(F32)
                pf = pick(gp, hid)
                pb = pick(gp, DN_HEADS + hid)
                parts.append(jnp.concatenate([jnp.where(strict, jnp.where(fwd, pf, pb), 0.0), pf, pb], axis=1).astype(BF16))
            res = _dot(tri6, jnp.concatenate(parts, axis=0))
            return jnp.where(fwd, res[:c, 0:LANES], res[c:, 0:LANES]), res[:c, LANES:2 * LANES], res[c:, 2 * LANES:]

        def stage_rhs(qs, k, v, gc_f, gc_b, bt_f, bt_b):
            e_f = jnp.exp(gc_f)
            e_b = jnp.exp(gc_b)
            rhs = jnp.concatenate([
                jnp.concatenate([v * bt_f, k * (bt_f * e_f)], axis=1),
                jnp.concatenate([v * bt_b, k * (bt_b * e_b)], axis=1)], axis=0).astype(BF16)
            gl_f = gc_f[c - 1:c, :]
            gl_b = gc_b[0:1, :]
            kd = jnp.concatenate([k * jnp.exp(gl_f - gc_f), k * jnp.exp(gl_b - gc_b)], axis=0).astype(BF16)
            return rhs, kd, (qs * e_f, qs * e_b), (jnp.exp(gl_f), jnp.exp(gl_b))

        def stage_solve(p, rhs):
            sol16 = _dot(bd(p.astype(BF16)), rhs).astype(BF16)
            z16 = jnp.zeros((c, 2 * DN_DV), BF16)
            return jnp.concatenate([jnp.concatenate([sol16[:c], z16], axis=1),
                                    jnp.concatenate([z16, sol16[c:]], axis=1)], axis=0)

        def stage_store(ch, qsol, ksol, q_dec, egl):
            rows = pl.ds(pl.multiple_of(ch * c, c), c)
            for d in range(2):
                lo = 2 * DN_DV * d
                aq_ref[hh, d, ch, 0:DN_DK, :] = ksol[:, lo + DN_DV:lo + 2 * DN_DV].astype(BF16)
                aq_ref[hh, d, ch, DN_DK:DN_DK + c, :] = (q_dec[d] - qsol[:, lo + DN_DV:lo + 2 * DN_DV]).astype(BF16)
                b_ref[hh, d, ch] = ksol[:, lo:lo + DN_DV]
                o_ref[hh, d, rows, :] = qsol[:, lo:lo + DN_DV]
                gl_ref[hh, d, ch] = jnp.broadcast_to(egl[d], (SUBLANES, LANES))

        def body(i, carry):
            chs = [i * DN_PREP_CHUNKS + j for j in range(DN_PREP_CHUNKS)]
            ins = [stage_inputs(ch) for ch in chs]
            dec = [stage_decay(x[3]) for x in ins]
            bts = [(pick(x[4], 2 * DN_HEADS + hid), pick(x[4], 3 * DN_HEADS + hid)) for x in ins]
            decay = [jnp.where(incl, jnp.exp(jnp.where(incl, d[0], 0.0)), 0.0) for d in dec]
            k16 = [x[1].astype(BF16) for x in ins]
            kk = [jnp.concatenate([a, a], axis=0) for a in k16]
            qk16 = [(_dot_nt(x[0].astype(BF16), b) * dc).astype(BF16) for x, b, dc in zip(ins, kk, decay)]
            lmat = [jnp.where(strict, jnp.where(fwd, bt[0], bt[1]) * _dot_nt(a, b) * dc, 0.0)
                    for a, b, bt, dc in zip(k16, kk, bts, decay)]
            lsp = [_split2(x) for x in lmat]
            m = [mm3([s], s[0], s[1]) for s in lsp]
            p = [eye2 - x for x in lmat]
            for s in range(n_sq):
                msp = [_split2(x) for x in m]
                psp = [_split2(x) for x in p]
                if s < n_sq - 1:
                    res = [mm3([a, b], b[0], b[1]) for a, b in zip(psp, msp)]
                    p = [a + r[:c] for a, r in zip(p, res)]
                    m = [r[c:] for r in res]
                else:
                    p = [a + mm3([b], e[0], e[1]) for a, b, e in zip(p, psp, msp)]
            fin = [stage_rhs(x[0], x[1], x[2], d[1], d[2], *bt) for x, d, bt in zip(ins, dec, bts)]
            both = [stage_solve(a, f[0]) for a, f in zip(p, fin)]
            qsol = [_dot(a, b) for a, b in zip(qk16, both)]
            ksol = [_dot_tn(f[1], b) for f, b in zip(fin, both)]
            for ch, a, b, f in zip(chs, qsol, ksol, fin):
                stage_store(ch, a, b, f[2], f[3])
            return carry

        lax.fori_loop(0, (ncc + ncl) // DN_PREP_CHUNKS, body, 0, unroll=True)

    for hh in range(hg):
        prep_head(hh)

    def scan(first, n):
        def body(i, carry):
            chains = [(hh, d, first + i if d == 0 else first + n - 1 - i) for hh in range(hg) for d in range(2)]
            s_mats = [s_ref[hh, d] for hh, d, _ in chains]
            res = [_dot(aq_ref[hh, d, ch], s.astype(BF16)) for s, (hh, d, ch) in zip(s_mats, chains)]
            for s, r, (hh, d, ch) in zip(s_mats, res, chains):
                s_ref[hh, d] = s * gl_ref[hh, d, ch][0:1, :] + (b_ref[hh, d, ch] - r[:DN_DK])
                rows = pl.ds(pl.multiple_of(ch * c, c), c)
                o_ref[hh, d, rows, :] = o_ref[hh, d, rows, :] + r[DN_DK:]
            return carry
        return body

    lax.fori_loop(0, ncc, scan(0, ncc), 0, unroll=True)
    lax.fori_loop(0, ncl, scan(ncc, ncl), 0, unroll=4)

    def gated_norm(o, z):
        return (o * lax.rsqrt(jnp.mean(o * o, axis=-1, keepdims=True) + NORM_EPS) * nw_ref[...] * _silu(z))

    piece = 4 * c
    for hh in range(hg):
        hs = slice(hh * LANES, (hh + 1) * LANES)
        yc_ref[:, hs] = gated_norm(o_ref[hh, 0, 0:tc, :] + o_ref[hh, 1, 0:tc, :], zc_ref[:, hs]).astype(BF16)
        for j in range(t // piece):
            rows = slice(j * piece, (j + 1) * piece)
            orow = slice(tc + j * piece, tc + (j + 1) * piece)
            y_ref[rows, hs] = gated_norm(o_ref[hh, 0, orow, :] + o_ref[hh, 1, orow, :], z_ref[rows, hs]).astype(BF16)


def _dn_call(dn, z, ab, dnc, zc, abc, conv_w, alog, dtb, norm_w):
    bsz, t, _ = dn.shape
    tc = dnc.shape[1]
    h = DN_HEADS
    hg = DN_HEAD_GROUP
    nch = (t + tc) // DN_CHUNK
    c = DN_CHUNK
    wide = hg * LANES

    def slab(rows, j0):
        return pl.BlockSpec((None, rows, wide), lambda b, g: (b, 0, j0 // hg + g))

    def whole(rows):
        return pl.BlockSpec((None, rows, LANES), lambda b, g: (b, 0, 0))

    def cw(j0):
        return pl.BlockSpec((DN_CONV, wide), lambda b, g: (0, j0 // hg + g))

    vec = pl.BlockSpec((1, LANES), lambda b, g: (0, 0))
    return pl.pallas_call(
        functools.partial(_dn_kernel, hg=hg),
        out_shape=[jax.ShapeDtypeStruct((bsz, t, h * DN_DV), BF16), jax.ShapeDtypeStruct((bsz, tc, h * DN_DV), BF16)],
        grid=(bsz, h // hg),
        in_specs=[slab(t, 0), slab(t, h), slab(t, 2 * h), slab(t, 0), whole(t),
                  slab(tc, 0), slab(tc, h), slab(tc, 2 * h), slab(tc, 0), whole(tc),
                  cw(0), cw(h), cw(2 * h), vec, vec, vec],
        out_specs=[slab(t, 0), slab(tc, 0)],
        scratch_shapes=[
            pltpu.VMEM((3, 3 * SUBLANES + tc + t, LANES), F32),
            pltpu.VMEM((tc + t, LANES), F32),
            pltpu.VMEM((hg, 2, nch, DN_DK + c, DN_DV), BF16),
            pltpu.VMEM((hg, 2, nch, DN_DK, DN_DV), F32),
            pltpu.VMEM((hg, 2, nch, SUBLANES, LANES), F32),
            pltpu.VMEM((hg, 2, tc + t, DN_DV), F32),
            pltpu.VMEM((hg, 2, DN_DK, DN_DV), F32),
        ],
        compiler_params=_params("parallel", "parallel"),
        name="deltanet",
    )(dn, dn, dn, z, ab, dnc, dnc, dnc, zc, abc, conv_w, conv_w, conv_w, alog, dtb, norm_w)


def _swa_kernel(sink_ref, q_ref, k_ref, v_ref, kc_ref, vc_ref, o_ref, *, n_blocks):
    blk = SWA_BLOCK
    n_kv_pairs = k_ref.shape[1] // LANES
    slabs = q_ref.shape[1] // LANES // n_kv_pairs
    rows = slabs * blk
    low = lax.broadcasted_iota(jnp.int32, (rows, LANES), 1) < SWA_HD
    low_blk = lax.broadcasted_iota(jnp.int32, (blk, LANES), 1) < SWA_HD
    low1 = lax.broadcasted_iota(jnp.int32, (1, LANES), 1) < SWA_HD
    low16 = jnp.where(low1, 1.0, 0.0).astype(BF16)
    high16 = jnp.where(low1, 0.0, 1.0).astype(BF16)
    qi = lax.broadcasted_iota(jnp.int32, (rows, blk), 0) % blk
    kj = lax.broadcasted_iota(jnp.int32, (rows, blk), 1)

    def block_body(n, carry):
        lo = jnp.maximum(n - 1, 0)
        hi = jnp.minimum(n + 1, n_blocks - 1)
        ok_lo = (kj >= qi) & (n > 0)
        ok_hi = (kj <= qi) & (n < n_blocks - 1)

        def tok(i):
            return pl.ds(pl.multiple_of(i * blk, blk), blk)

        def kv_rows(ref, ctx_ref, kvl):
            return jnp.concatenate([ref[tok(lo), kvl], ref[tok(n), kvl], ref[tok(hi), kvl], ctx_ref[:, kvl]], axis=0)

        def softmax(s, p, half):
            cols = [s[:, j * blk:(j + 1) * blk] for j in range(s.shape[1] // blk)]
            cols[0] = jnp.where(ok_lo, cols[0], NEG_INF)
            cols[2] = jnp.where(ok_hi, cols[2], NEG_INF)
            probs, e_sink = [], []
            for j in range(slabs):
                cj = [c[j * blk:(j + 1) * blk] for c in cols]
                sink = sink_ref[(2 * p + half) * slabs + j] * LOG2E
                m = jnp.maximum(jnp.max(functools.reduce(jnp.maximum, cj), axis=-1, keepdims=True), sink)
                probs.append(jnp.concatenate([jnp.exp2(c - m).astype(BF16) for c in cj], axis=1))
                e_sink.append(jnp.exp2(sink - m))
            return jnp.concatenate(probs, axis=0), e_sink

        units = [(p, half) for p in range(n_kv_pairs) for half in range(2)]
        kvls = [slice(p * LANES, (p + 1) * LANES) for p in range(n_kv_pairs)]
        k_all = [kv_rows(k_ref, kc_ref, kvl) for kvl in kvls]
        v_all = [kv_rows(v_ref, vc_ref, kvl) for kvl in kvls]
        qs = [jnp.concatenate([q_ref[tok(n), (p * slabs + j) * LANES:(p * slabs + j + 1) * LANES]
                               for j in range(slabs)], axis=0) for p in range(n_kv_pairs)]
        s = [_dot_nt(qs[p] * (high16 if half else low16), k_all[p]) for p, half in units]
        pe = [softmax(x, p, half) for x, (p, half) in zip(s, units)]
        r = [_dot(x[0], v_all[p] * (high16 if half else low16) + (low16 if half else high16))
             for x, (p, half) in zip(pe, units)]
        res = [jnp.concatenate(
            [y[j * blk:(j + 1) * blk] + jnp.where(low_blk if half else ~low_blk, x[1][j], 0.0) for j in range(slabs)],
            axis=0) for y, x, (p, half) in zip(r, pe, units)]
        for p in range(n_kv_pairs):
            num = jnp.where(low, res[2 * p], res[2 * p + 1])
            den = pltpu.roll(jnp.where(low, res[2 * p + 1], res[2 * p]), SWA_HD, 1)
            o = (num / den).astype(BF16)
            for j in range(slabs):
                o_ref[tok(n), (p * slabs + j) * LANES:(p * slabs + j + 1) * LANES] = o[j * blk:(j + 1) * blk]
        return carry

    lax.fori_loop(0, n_blocks, block_body, 0, unroll=4)


def _swa_call(sink, q, k, v, kc, vc):
    bsz, t, wq = q.shape
    wk = k.shape[2]
    tc = kc.shape[1]
    return pl.pallas_call(
        functools.partial(_swa_kernel, n_blocks=t // SWA_BLOCK),
        out_shape=jax.ShapeDtypeStruct((bsz, t, wq), BF16),
        grid=(bsz,),
        in_specs=[
            pl.BlockSpec(memory_space=pltpu.SMEM),
            pl.BlockSpec((None, t, wq), lambda b: (b, 0, 0)),
            pl.BlockSpec((None, t, wk), lambda b: (b, 0, 0)),
            pl.BlockSpec((None, t, wk), lambda b: (b, 0, 0)),
            pl.BlockSpec((None, tc, wk), lambda b: (b, 0, 0)),
            pl.BlockSpec((None, tc, wk), lambda b: (b, 0, 0)),
        ],
        out_specs=pl.BlockSpec((None, t, wq), lambda b: (b, 0, 0)),
        compiler_params=_params("parallel"),
        name="swa_attn",
    )(sink, q, k, v, kc, vc)


def _swa_slab_order(w, axis):
    rep = SWA_HEADS // SWA_KV_HEADS
    shape = w.shape
    split = shape[:axis] + (SWA_KV_HEADS // 2, 2, rep, SWA_HD) + shape[axis + 1:]
    return jnp.swapaxes(w.reshape(split), axis + 1, axis + 2).reshape(shape)


def _rope_tables(t):
    pos = jnp.arange(t)
    half = SWA_HD // 2
    inv = jnp.power(ROPE_BASE, -jnp.arange(0, half, 2, dtype=F32) / half)
    ang_r = (pos // GRID_W).astype(F32)[:, None] * inv
    ang_c = (pos % GRID_W).astype(F32)[:, None] * inv
    cos = jnp.concatenate([jnp.cos(ang_r)] * 2 + [jnp.cos(ang_c)] * 2, axis=-1)
    sin = jnp.concatenate([-jnp.sin(ang_r), jnp.sin(ang_r), -jnp.sin(ang_c), jnp.sin(ang_c)], axis=-1)
    return jnp.tile(cos, (1, LANES // SWA_HD)), jnp.tile(sin, (1, LANES // SWA_HD))


def kernel(x, c, ctx, c_ctx, ada_w, ada_b, norm_g, ffn_w_up, ffn_w_down, even_w_in, even_w_out, na_rpb,
           dn_conv_w, dn_a_log, dn_dt_bias, dn_norm_w, odd_w_in, odd_w_out, swa_sink, final_norm_g):
    bsz, t, d = x.shape
    tc = ctx.shape[1]
    depth = ada_w.shape[0]
    ctx_row = bsz

    c16 = jnp.concatenate([c, c_ctx[None, :], jnp.zeros((MOD_ROWS - bsz - 1, d), F32)], axis=0)
    mods = _adaln_call(c16, ada_w, ada_b).reshape(depth, N_MOD, MOD_ROWS, 1, d)
    norm_g3 = norm_g.reshape(depth * 3, 1, d)

    w_up = ffn_w_up.astype(BF16)
    f = ffn_w_down.shape[2]
    w_down = ffn_w_down.reshape(depth, 2, f // FF_CHUNK, FF_CHUNK, d)

    h = x
    hc = ctx.reshape(1, bsz * tc, d)
    for i in range(depth):
        need_ctx = i < depth - 1
        j = i // 2
        h = _ffn_call(h, mods, norm_g3, i, 0, None, w_up, w_down)
        hc = _ffn_call(hc, mods, norm_g3, i, 0, ctx_row, w_up, w_down)
        mix_c = None
        if i % 2 == 0:
            w_in = even_w_in[j]
            w_in = jnp.pad(w_in, ((0, 0), (0, -w_in.shape[1] % LANES))).astype(BF16)
            na, dn, z, ab = _proj_even_call(h, mods, norm_g3, i, None, w_in)
            nac, dnc, zc, abc = [a.reshape(bsz, tc, a.shape[-1])
                                 for a in _proj_even_call(hc, mods, norm_g3, i, ctx_row, w_in)]
            y_na, y_na_c = _na_call(na, nac, _na_bias_table(na_rpb[j]))
            lane_pad = (0, LANES - 2 * DN_HEADS)
            alog = jnp.pad(dn_a_log[j].reshape(-1), lane_pad)[None, :]
            dtb = jnp.pad(dn_dt_bias[j].reshape(-1), lane_pad)[None, :]
            y_dn, y_dn_c = _dn_call(dn, z, ab, dnc, zc, abc, dn_conv_w[j], alog, dtb, dn_norm_w[j][None, :])
            w_out = even_w_out[j].astype(BF16)
            ws = [w_out[:NA_HEADS * NA_HD], w_out[NA_HEADS * NA_HD:]]
            mix = ([y_na, y_dn], ws)
            if need_ctx:
                mix_c = ([y.reshape(1, bsz * tc, y.shape[-1]) for y in (y_na_c, y_dn_c)], ws)
        else:
            if need_ctx:
                raise NotImplementedError("context queries of a windowed layer are only needed before a later layer")
            w_in = odd_w_in[j]
            qw = SWA_HEADS * SWA_HD
            kw = SWA_KV_HEADS * SWA_HD
            wq = _swa_slab_order(w_in[:, :qw].astype(BF16), 1)
            wk = w_in[:, qw:qw + kw].astype(BF16)
            wv = w_in[:, qw + kw:].astype(BF16)
            cos, sin = _rope_tables(t)
            q, k, v = _proj_odd_call(h, mods, norm_g3, i, None, wq, wk, wv, cos, sin)
            kc, vc = [a.reshape(bsz, tc, kw) for a in _proj_odd_call(hc, mods, norm_g3, i, ctx_row, None, wk, wv, None, None)]
            mix = ([_swa_call(swa_sink[j], q, k, v, kc, vc)], [_swa_slab_order(odd_w_out[j].astype(BF16), 0)])
        last = i == depth - 1
        h = _ffn_call(h, mods, norm_g3, i, 1, None, w_up, w_down, mix=mix,
                      final_g=final_norm_g[None, :] if last else None)
        if need_ctx:
            hc = _ffn_call(hc, mods, norm_g3, i, 1, ctx_row, w_up, w_down, mix=mix_c)
    return h
```

```python
import functools
import math

import jax
import jax.numpy as jnp
from jax import lax
from jax.experimental import pallas as pl
from jax.experimental.pallas import tpu as pltpu

F32 = jnp.float32
BF16 = jnp.bfloat16

GRID_W = 64
N_MOD = 9
NORM_EPS = 1e-6
NEG_INF = -1e30
LOG2E = math.log2(math.e)
ROPE_BASE = 10000.0
NA_HEADS = 8
NA_HD = 64
NA_WIN_H = 8
NA_WIN_W = 16
NA_PROJ = 3 * NA_HEADS * NA_HD
DN_HEADS = 4
DN_DK = 128
DN_DV = 128
DN_CONV = 5
DN_CHUNK = 64
DN_QKV = DN_HEADS * (2 * DN_DK + DN_DV)
SWA_HEADS = 16
SWA_KV_HEADS = 4
SWA_HD = 64
SWA_BLOCK = 128

LANES = 128
SUBLANES = 8
VMEM_LIMIT = 56 * 1024 * 1024

TOK_TILE = 1024
FF_CHUNK = 256
MOD_ROWS = 16
DN_HEAD_GROUP = 2
NA_ROWS_PER_STEP = 2
DN_PREP_CHUNKS = 9


def _params(*sem):
    return pltpu.CompilerParams(dimension_semantics=sem, vmem_limit_bytes=VMEM_LIMIT)


def _dot(a, b):
    return jnp.dot(a, b, preferred_element_type=F32)


def _dot_nt(a, b):
    return lax.dot_general(a, b, (((1,), (1,)), ((), ())), preferred_element_type=F32)


def _dot_tn(a, b):
    return lax.dot_general(a, b, (((0,), (0,)), ((), ())), preferred_element_type=F32)


def _sigmoid(x):
    return 0.5 + 0.5 * jnp.tanh(0.5 * x)


def _silu(x):
    h = 0.5 * x
    return h + h * jnp.tanh(h)


def _softplus(x):
    return jnp.maximum(x, 0.0) + jnp.log1p(jnp.exp(-jnp.abs(x)))


def _modulate(x, g, shift, scale):
    y = x * lax.rsqrt(jnp.mean(x * x, axis=-1, keepdims=True) + NORM_EPS) * g
    return y * (1.0 + scale) + shift


def _split3(x):
    h1 = x.astype(BF16)
    r1 = x - h1.astype(F32)
    h2 = r1.astype(BF16)
    h3 = (r1 - h2.astype(F32)).astype(BF16)
    return h1, h2, h3


def _split2(x):
    hi = x.astype(BF16)
    return hi, (x - hi.astype(F32)).astype(BF16)


def _adaln_kernel(c_ref, w_ref, b_ref, o_ref):
    s = _silu(c_ref[...]).astype(BF16)
    o_ref[...] = _dot(s, w_ref[...].astype(BF16)) + b_ref[...]


def _adaln_call(c16, ada_w, ada_b):
    depth, d, _ = ada_w.shape
    b4 = ada_b.reshape(depth * N_MOD, 1, d)
    return pl.pallas_call(
        _adaln_kernel,
        out_shape=jax.ShapeDtypeStruct((depth, N_MOD, MOD_ROWS, d), F32),
        grid=(depth, N_MOD),
        in_specs=[
            pl.BlockSpec((MOD_ROWS, d), lambda i, k: (0, 0)),
            pl.BlockSpec((None, d, d), lambda i, k: (i, 0, k)),
            pl.BlockSpec((None, 1, d), lambda i, k: (i * N_MOD + k, 0, 0)),
        ],
        out_specs=pl.BlockSpec((None, None, MOD_ROWS, d), lambda i, k: (i, k, 0, 0)),
        compiler_params=_params("arbitrary", "arbitrary"),
        name="adaln",
    )(c16, ada_w, b4)


def _mod_spec(layer, k, ctx_row, d):
    if ctx_row is None:
        return pl.BlockSpec((None, None, None, 1, d), lambda b, t: (layer, k, b, 0, 0))
    return pl.BlockSpec((None, None, None, 1, d), lambda b, t: (layer, k, ctx_row, 0, 0))


def _const_spec(shape):
    nd = len(shape)
    return pl.BlockSpec(shape, lambda b, t: (0,) * nd, pipeline_mode=pl.Buffered(1))


def _ffn_kernel(x_ref, g_ref, sh_ref, sc_ref, gt_ref, wup_ref, wd_ref, *rest, n_mix, final):
    if n_mix:
        mg_ref, rest = rest[0], rest[1:]
        y_refs, w_refs, rest = rest[:n_mix], rest[n_mix:2 * n_mix], rest[2 * n_mix:]
    if final:
        fg_ref, rest = rest[0], rest[1:]
    o_ref, u_ref, acc_ref = rest
    n_chunks, fc, _ = wd_ref.shape
    f = n_chunks * fc

    x = x_ref[...]
    if n_mix:
        mix = _dot(y_refs[0][...], w_refs[0][...])
        for y_ref, w_ref in zip(y_refs[1:], w_refs[1:]):
            mix += _dot(y_ref[...], w_ref[...])
        x = x + mg_ref[...] * mix
        o_ref[...] = x
    u_ref[...] = _modulate(x, g_ref[...], sh_ref[...], sc_ref[...]).astype(BF16)
    acc_ref[...] = jnp.zeros_like(acc_ref)

    def body(c, carry):
        u = u_ref[...]
        off = pl.multiple_of(c * fc, fc)
        gate = _dot(u, wup_ref[:, pl.ds(off, fc)])
        up = _dot(u, wup_ref[:, pl.ds(f + off, fc)])
        a = (_silu(gate) * up).astype(BF16)
        acc_ref[...] += _dot(a, wd_ref[c].astype(BF16))
        return carry

    lax.fori_loop(0, n_chunks, body, 0, unroll=True)
    y = (o_ref[...] if n_mix else x_ref[...]) + 0.5 * gt_ref[...] * acc_ref[...]
    if final:
        y = y * lax.rsqrt(jnp.mean(y * y, axis=-1, keepdims=True) + NORM_EPS) * fg_ref[...]
    o_ref[...] = y


def _ffn_call(h, mods, norm_g3, layer, which, ctx_row, w_up, w_down, mix=None, final_g=None):
    bsz, t, d = h.shape
    tm = min(TOK_TILE, t)
    k0 = 6 if which else 0
    in_specs = [
        pl.BlockSpec((None, tm, d), lambda b, i: (b, i, 0)),
        pl.BlockSpec((None, 1, d), lambda b, i: (layer * 3 + (2 if which else 0), 0, 0)),
        _mod_spec(layer, k0, ctx_row, d),
        _mod_spec(layer, k0 + 1, ctx_row, d),
        _mod_spec(layer, k0 + 2, ctx_row, d),
        pl.BlockSpec((None, None) + w_up.shape[2:], lambda b, i: (layer, which, 0, 0), pipeline_mode=pl.Buffered(1)),
        pl.BlockSpec((None, None) + w_down.shape[2:], lambda b, i: (layer, which, 0, 0, 0),
                     pipeline_mode=pl.Buffered(1)),
    ]
    args = [h, norm_g3, mods, mods, mods, w_up, w_down]
    n_mix = 0
    if mix is not None:
        ys, ws = mix
        n_mix = len(ys)
        in_specs.append(_mod_spec(layer, 5, ctx_row, d))
        in_specs += [pl.BlockSpec((None, tm, y.shape[2]), lambda b, i: (b, i, 0)) for y in ys]
        in_specs += [_const_spec(w.shape) for w in ws]
        args += [mods, *ys, *ws]
    if final_g is not None:
        in_specs.append(pl.BlockSpec((1, d), lambda b, i: (0, 0)))
        args.append(final_g)
    return pl.pallas_call(
        functools.partial(_ffn_kernel, n_mix=n_mix, final=final_g is not None),
        out_shape=jax.ShapeDtypeStruct(h.shape, F32),
        grid=(bsz, t // tm),
        in_specs=in_specs,
        out_specs=pl.BlockSpec((None, tm, d), lambda b, i: (b, i, 0)),
        scratch_shapes=[pltpu.VMEM((tm, d), BF16), pltpu.VMEM((tm, d), F32)],
        compiler_params=_params("parallel", "parallel"),
        name="ffn",
    )(*args)


def _proj_even_kernel(x_ref, g_ref, sh_ref, sc_ref, w_ref, wab_ref, na_ref, dn_ref, z_ref, ab_ref):
    u = _modulate(x_ref[...], g_ref[...], sh_ref[...], sc_ref[...]).astype(BF16)
    c0 = 0
    for o_ref in (na_ref, dn_ref, z_ref):
        c1 = c0 + o_ref.shape[1]
        y = _dot(u, w_ref[:, c0:c1])
        if o_ref is na_ref:
            qw = NA_HEADS * NA_HD
            o_ref[:, :qw] = (y[:, :qw] * (NA_HD ** -0.5 * LOG2E)).astype(BF16)
            o_ref[:, qw:] = y[:, qw:].astype(BF16)
        else:
            o_ref[...] = y
        c0 = c1
    ab_ref[...] = _dot(u, wab_ref[...])


def _proj_even_call(h, mods, norm_g3, layer, ctx_row, w_in, w_ab):
    bsz, t, d = h.shape
    tm = min(TOK_TILE, t)
    widths = (NA_PROJ, DN_QKV, DN_HEADS * DN_DV, LANES)
    dtypes = (BF16, F32, F32, F32)
    return pl.pallas_call(
        _proj_even_kernel,
        out_shape=[jax.ShapeDtypeStruct((bsz, t, w), dt) for w, dt in zip(widths, dtypes)],
        grid=(bsz, t // tm),
        in_specs=[
            pl.BlockSpec((None, tm, d), lambda b, i: (b, i, 0)),
            pl.BlockSpec((None, 1, d), lambda b, i: (layer * 3 + 1, 0, 0)),
            _mod_spec(layer, 3, ctx_row, d),
            _mod_spec(layer, 4, ctx_row, d),
            _const_spec(w_in.shape),
            _const_spec(w_ab.shape),
        ],
        out_specs=[pl.BlockSpec((None, tm, w), lambda b, i: (b, i, 0)) for w in widths],
        compiler_params=_params("parallel", "parallel"),
        name="proj_even",
    )(h, norm_g3, mods, mods, w_in, w_ab)


def _rope_slab(x, cos, sin, first):
    swapped = jnp.where(first, pltpu.roll(x, LANES - 16, 1), pltpu.roll(x, 16, 1))
    return x * cos + swapped * sin


def _proj_odd_kernel(x_ref, g_ref, sh_ref, sc_ref, *rest, with_q):
    if with_q:
        wq_ref, wk_ref, wv_ref, cos_ref, sin_ref, q_ref, k_ref, v_ref = rest
    else:
        wk_ref, wv_ref, k_ref, v_ref = rest
    u = _modulate(x_ref[...], g_ref[...], sh_ref[...], sc_ref[...]).astype(BF16)
    v_ref[...] = _dot(u, wv_ref[...]).astype(BF16)
    k = _dot(u, wk_ref[...])
    if not with_q:
        k_ref[...] = k.astype(BF16)
        return
    cos = cos_ref[...]
    sin = sin_ref[...]
    lane = lax.broadcasted_iota(jnp.int32, cos.shape, 1)
    first = (lane % 32) < 16
    for j in range(k.shape[1] // LANES):
        sl = slice(j * LANES, (j + 1) * LANES)
        k_ref[:, sl] = _rope_slab(k[:, sl], cos, sin, first).astype(BF16)
    q = _dot(u, wq_ref[...])
    for j in range(q.shape[1] // LANES):
        sl = slice(j * LANES, (j + 1) * LANES)
        q_ref[:, sl] = (_rope_slab(q[:, sl], cos, sin, first) * (SWA_HD ** -0.5 * LOG2E)).astype(BF16)


def _proj_odd_call(h, mods, norm_g3, layer, ctx_row, wq, wk, wv, cos, sin):
    bsz, t, d = h.shape
    tm = min(TOK_TILE, t)
    with_q = wq is not None
    in_specs = [
        pl.BlockSpec((None, tm, d), lambda b, i: (b, i, 0)),
        pl.BlockSpec((None, 1, d), lambda b, i: (layer * 3 + 1, 0, 0)),
        _mod_spec(layer, 3, ctx_row, d),
        _mod_spec(layer, 4, ctx_row, d),
    ]
    args = [h, norm_g3, mods, mods]
    widths = []
    if with_q:
        in_specs.append(_const_spec(wq.shape))
        args.append(wq)
        widths.append(wq.shape[1])
    in_specs += [_const_spec(wk.shape), _const_spec(wv.shape)]
    args += [wk, wv]
    widths += [wk.shape[1], wv.shape[1]]
    if with_q:
        in_specs += [pl.BlockSpec((tm, LANES), lambda b, i: (i, 0))] * 2
        args += [cos, sin]
    return pl.pallas_call(
        functools.partial(_proj_odd_kernel, with_q=with_q),
        out_shape=[jax.ShapeDtypeStruct((bsz, t, w), BF16) for w in widths],
        grid=(bsz, t // tm),
        in_specs=in_specs,
        out_specs=[pl.BlockSpec((None, tm, w), lambda b, i: (b, i, 0)) for w in widths],
        compiler_params=_params("parallel", "parallel"),
        name="proj_odd",
    )(*args)


def _softmax2(cols):
    m = jnp.max(functools.reduce(jnp.maximum, cols), axis=-1, keepdims=True)
    p = [jnp.exp2(c - m) for c in cols]
    den = jnp.sum(functools.reduce(jnp.add, p), axis=-1, keepdims=True)
    return jnp.concatenate([x.astype(BF16) for x in p], axis=1), den


def _na_kernel(q_ref, k_ref, v_ref, qc_ref, kc_ref, vc_ref, bias_ref, o_ref, oc_ref, *, rows):
    n_pairs = q_ref.shape[1] // LANES
    tc = qc_ref.shape[0]
    win = NA_WIN_H * GRID_W
    n_bias = win // LANES
    low1 = lax.broadcasted_iota(jnp.int32, (1, LANES), 1) < NA_HD
    low16 = jnp.where(low1, 1.0, 0.0).astype(BF16)
    high16 = jnp.where(low1, 0.0, 1.0).astype(BF16)
    low_q = lax.broadcasted_iota(jnp.int32, (GRID_W, LANES), 1) < NA_HD
    low_c = lax.broadcasted_iota(jnp.int32, (tc, LANES), 1) < NA_HD
    pairs = [slice(hp * LANES, (hp + 1) * LANES) for hp in range(n_pairs)]

    def row_body(i, carry):
        units = []
        for rr in range(NA_ROWS_PER_STEP):
            r = i * NA_ROWS_PER_STEP + rr
            r0 = jnp.clip(r - NA_WIN_H // 2, 0, rows - NA_WIN_H)
            doff = r0 - r + NA_WIN_H - 1
            qs = pl.ds(pl.multiple_of(r * GRID_W, GRID_W), GRID_W)
            ks = pl.ds(pl.multiple_of(r0 * GRID_W, GRID_W), win)
            units += [(hp, sl, doff, qs, ks) for hp, sl in enumerate(pairs)]
        qm = [jnp.concatenate([q_ref[qs, sl] * low16, q_ref[qs, sl] * high16], axis=0) for _, sl, _, qs, _ in units]
        s = [_dot_nt(x, jnp.concatenate([k_ref[ks, sl], kc_ref[:, sl]], axis=0))
             for x, (_, sl, _, _, ks) in zip(qm, units)]
        cols = [[x[:, j * LANES:(j + 1) * LANES] + bias_ref[hp, doff + 2 * j] for j in range(n_bias)]
                + [x[:, j * LANES:(j + 1) * LANES] for j in range(n_bias, x.shape[1] // LANES)]
                for x, (hp, _, doff, _, _) in zip(s, units)]
        pd = [_softmax2(c) for c in cols]
        o = [_dot(p, jnp.concatenate([v_ref[ks, sl], vc_ref[:, sl]], axis=0)) / den
             for (p, den), (_, sl, _, _, ks) in zip(pd, units)]
        for x, (_, sl, _, qs, _) in zip(o, units):
            o_ref[qs, sl] = jnp.where(low_q, x[:GRID_W], x[GRID_W:]).astype(BF16)
        return carry

    lax.fori_loop(0, rows // NA_ROWS_PER_STEP, row_body, 0, unroll=4)

    for sl in pairs:
        halves = []
        for keep16 in (low16, high16):
            s = _dot_nt(qc_ref[:, sl] * keep16, kc_ref[:, sl])
            p, den = _softmax2([s[:, j * LANES:(j + 1) * LANES] for j in range(tc // LANES)])
            halves.append(_dot(p, vc_ref[:, sl]) / den)
        oc_ref[:, sl] = jnp.where(low_c, halves[0], halves[1]).astype(BF16)


def _na_call(qkv, qkvc, bias):
    bsz, t, w3 = qkv.shape
    tc = qkvc.shape[1]
    w = w3 // 3

    def col(j):
        return lambda b: (b, 0, j)

    return pl.pallas_call(
        functools.partial(_na_kernel, rows=t // GRID_W),
        out_shape=[jax.ShapeDtypeStruct((bsz, t, w), BF16), jax.ShapeDtypeStruct((bsz, tc, w), BF16)],
        grid=(bsz,),
        in_specs=[pl.BlockSpec((None, t, w), col(j)) for j in range(3)]
        + [pl.BlockSpec((None, tc, w), col(j)) for j in range(3)]
        + [pl.BlockSpec(bias.shape, lambda b: (0, 0, 0, 0), pipeline_mode=pl.Buffered(1))],
        out_specs=[pl.BlockSpec((None, t, w), lambda b: (b, 0, 0)), pl.BlockSpec((None, tc, w), lambda b: (b, 0, 0))],
        compiler_params=_params("parallel"),
        name="na_attn",
    )(qkv, qkv, qkv, qkvc, qkvc, qkvc, bias)


def _na_bias_table(rpb):
    c = jnp.arange(GRID_W)[:, None]
    kc = jnp.arange(GRID_W)[None, :]
    cstart = jnp.clip(c - NA_WIN_W // 2, 0, GRID_W - NA_WIN_W)
    ok = (kc >= cstart) & (kc < cstart + NA_WIN_W)
    dc = jnp.clip(kc - c + NA_WIN_W - 1, 0, 2 * NA_WIN_W - 2)
    onehot = (dc[None] == jnp.arange(2 * NA_WIN_W - 1)[:, None, None]).astype(F32)
    h = rpb.shape[0]
    nd = 2 * NA_WIN_H - 2
    rp = (rpb.astype(F32) * LOG2E).reshape(h // 2, 2, 2 * NA_WIN_H - 1, 2 * NA_WIN_W - 1)
    t = jnp.einsum('phdx,xck->pdhck', rp, onehot, precision=lax.Precision.HIGHEST)
    t = jnp.where(ok, t, NEG_INF).reshape(h // 2, 2 * NA_WIN_H - 1, 2 * GRID_W, GRID_W)
    return jnp.concatenate([t[:, :nd], t[:, 1:nd + 1]], axis=-1)


def _dn_kernel(qr_ref, kr_ref, vr_ref, z_ref, ab_ref, qcr_ref, kcr_ref, vcr_ref, zc_ref, abc_ref,
               cwq_ref, cwk_ref, cwv_ref, alog_ref, dtb_ref, nw_ref,
               y_ref, yc_ref,
               pad_ref, abs_ref, aq_ref, b_ref, gl_ref, o_ref, s_ref, *, hg):
    c = DN_CHUNK
    t = qr_ref.shape[0]
    tc = qcr_ref.shape[0]
    ncc, ncl = tc // c, t // c
    lat0 = 2 * SUBLANES + tc
    half = DN_CONV // 2
    n_sq = int(math.log2(c)) - 1

    zeros8 = jnp.zeros((3, SUBLANES, LANES), F32)
    pad_ref[:, 0:SUBLANES, :] = zeros8
    pad_ref[:, SUBLANES + tc:lat0, :] = zeros8
    pad_ref[:, lat0 + t:lat0 + t + SUBLANES, :] = zeros8
    abs_ref[0:tc, :] = abc_ref[...]
    abs_ref[tc:tc + t, :] = ab_ref[...]
    s_ref[...] = jnp.zeros_like(s_ref)

    lane = lax.broadcasted_iota(jnp.int32, (c, LANES), 1)
    row = lax.broadcasted_iota(jnp.int32, (c, LANES), 0)
    col = jnp.bitwise_and(lane, c - 1)
    fwd = lane < c
    incl = (fwd & (row >= col)) | (~fwd & (row <= col))
    strict = (fwd & (row > col)) | (~fwd & (row < col))
    diag = row == col
    eye2 = jnp.where(diag, 1.0, 0.0).astype(F32)
    fwd16 = jnp.where(fwd, 1.0, 0.0).astype(BF16)
    bwd16 = jnp.where(fwd, 0.0, 1.0).astype(BF16)
    ri = lax.broadcasted_iota(jnp.int32, (2 * c, c), 0)
    ci_ = lax.broadcasted_iota(jnp.int32, (2 * c, c), 1)
    tri2 = jnp.where(((ri < c) & (ci_ <= ri)) | ((ri >= c) & (ci_ >= ri - c)), 1.0, 0.0).astype(BF16)
    tri6 = jnp.concatenate([tri2, tri2, tri2], axis=1)
    neg_a = -jnp.exp(alog_ref[...])
    dtb = dtb_ref[...]

    def l2n(x):
        return x * lax.rsqrt(jnp.sum(x * x, axis=-1, keepdims=True) + NORM_EPS)

    def bd(m16):
        return jnp.concatenate([m16 * fwd16, m16 * bwd16], axis=0)

    def mm3(lhs_parts, bh, bl):
        bdh = bd(bh)
        rhs = jnp.concatenate([bdh, bdh, bd(bl)], axis=0)
        lhs = jnp.concatenate([jnp.concatenate([ah, al, ah], axis=1) for ah, al in lhs_parts], axis=0)
        return _dot(lhs, rhs)

    def prep_head(hh):
        hid = pl.program_id(1) * hg + hh
        hs = slice(hh * LANES, (hh + 1) * LANES)
        for s, (cr, lr) in enumerate(((qcr_ref, qr_ref), (kcr_ref, kr_ref), (vcr_ref, vr_ref))):
            pad_ref[s, SUBLANES:SUBLANES + tc, :] = cr[:, hs]
            pad_ref[s, lat0:lat0 + t, :] = lr[:, hs]

        def conv(s, cw_ref, base):
            acc = pad_ref[s, pl.ds(base - half, c), :] * cw_ref[0:1, hs]
            for j in range(1, DN_CONV):
                acc += pad_ref[s, pl.ds(base - half + j, c), :] * cw_ref[j:j + 1, hs]
            return _silu(acc)

        def pick(x, idx):
            return jnp.broadcast_to(jnp.sum(jnp.where(lane == idx, x, 0.0), axis=-1, keepdims=True), (c, LANES))

        def stage_inputs(ch):
            base = ch * c + jnp.where(ch < ncc, SUBLANES, 2 * SUBLANES)
            q = l2n(conv(0, cwq_ref, base))
            k = l2n(conv(1, cwk_ref, base))
            v = conv(2, cwv_ref, base)
            abt = abs_ref[pl.ds(pl.multiple_of(ch * c, c), c), :]
            g_all = neg_a * _softplus(abt + dtb)
            b_all = _sigmoid(abt)
            return q * DN_DK ** -0.5, k, v, g_all, b_all

        def stage_decay(g_all):
            parts = []
            for gp in _split3(g_all):
                gp = gp.astype(F32)
                pf = pick(gp, hid)
                pb = pick(gp, DN_HEADS + hid)
                parts.append(jnp.concatenate([jnp.where(strict, jnp.where(fwd, pf, pb), 0.0), pf, pb], axis=1).astype(BF16))
            res = _dot(tri6, jnp.concatenate(parts, axis=0))
            return jnp.where(fwd, res[:c, 0:LANES], res[c:, 0:LANES]), res[:c, LANES:2 * LANES], res[c:, 2 * LANES:]

        def stage_rhs(qs, k, v, gc_f, gc_b, bt_f, bt_b):
            e_f = jnp.exp(gc_f)
            e_b = jnp.exp(gc_b)
            rhs = jnp.concatenate([
                jnp.concatenate([v * bt_f, k * (bt_f * e_f)], axis=1),
                jnp.concatenate([v * bt_b, k * (bt_b * e_b)], axis=1)], axis=0).astype(BF16)
            gl_f = gc_f[c - 1:c, :]
            gl_b = gc_b[0:1, :]
            kd = jnp.concatenate([k * jnp.exp(gl_f - gc_f), k * jnp.exp(gl_b - gc_b)], axis=0).astype(BF16)
            return rhs, kd, (qs * e_f, qs * e_b), (jnp.exp(gl_f), jnp.exp(gl_b))

        def stage_solve(p, rhs):
            sol16 = _dot(bd(p.astype(BF16)), rhs).astype(BF16)
            z16 = jnp.zeros((c, 2 * DN_DV), BF16)
            return jnp.concatenate([jnp.concatenate([sol16[:c], z16], axis=1),
                                    jnp.concatenate([z16, sol16[c:]], axis=1)], axis=0)

        def stage_store(ch, qsol, ksol, q_dec, egl):
            rows = pl.ds(pl.multiple_of(ch * c, c), c)
            for d in range(2):
                lo = 2 * DN_DV * d
                aq_ref[hh, d, ch, 0:DN_DK, :] = ksol[:, lo + DN_DV:lo + 2 * DN_DV].astype(BF16)
                aq_ref[hh, d, ch, DN_DK:DN_DK + c, :] = (q_dec[d] - qsol[:, lo + DN_DV:lo + 2 * DN_DV]).astype(BF16)
                b_ref[hh, d, ch] = ksol[:, lo:lo + DN_DV]
                o_ref[hh, d, rows, :] = qsol[:, lo:lo + DN_DV]
                gl_ref[hh, d, ch] = jnp.broadcast_to(egl[d], (SUBLANES, LANES))

        def body(i, carry):
            chs = [i * DN_PREP_CHUNKS + j for j in range(DN_PREP_CHUNKS)]
            ins = [stage_inputs(ch) for ch in chs]
            dec = [stage_decay(x[3]) for x in ins]
            bts = [(pick(x[4], 2 * DN_HEADS + hid), pick(x[4], 3 * DN_HEADS + hid)) for x in ins]
            decay = [jnp.where(incl, jnp.exp(jnp.where(incl, d[0], 0.0)), 0.0) for d in dec]
            k16 = [x[1].astype(BF16) for x in ins]
            kk = [jnp.concatenate([a, a], axis=0) for a in k16]
            qk16 = [(_dot_nt(x[0].astype(BF16), b) * dc).astype(BF16) for x, b, dc in zip(ins, kk, decay)]
            lmat = [jnp.where(strict, jnp.where(fwd, bt[0], bt[1]) * _dot_nt(a, b) * dc, 0.0)
                    for a, b, bt, dc in zip(k16, kk, bts, decay)]
            lsp = [_split2(x) for x in lmat]
            m = [mm3([s], s[0], s[1]) for s in lsp]
            p = [eye2 - x for x in lmat]
            for s in range(n_sq):
                msp = [_split2(x) for x in m]
                psp = [_split2(x) for x in p]
                if s < n_sq - 1:
                    res = [mm3([a, b], b[0], b[1]) for a, b in zip(psp, msp)]
                    p = [a + r[:c] for a, r in zip(p, res)]
                    m = [r[c:] for r in res]
                else:
                    p = [a + mm3([b], e[0], e[1]) for a, b, e in zip(p, psp, msp)]
            fin = [stage_rhs(x[0], x[1], x[2], d[1], d[2], *bt) for x, d, bt in zip(ins, dec, bts)]
            both = [stage_solve(a, f[0]) for a, f in zip(p, fin)]
            qsol = [_dot(a, b) for a, b in zip(qk16, both)]
            ksol = [_dot_tn(f[1], b) for f, b in zip(fin, both)]
            for ch, a, b, f in zip(chs, qsol, ksol, fin):
                stage_store(ch, a, b, f[2], f[3])
            return carry

        lax.fori_loop(0, (ncc + ncl) // DN_PREP_CHUNKS, body, 0, unroll=True)

    for hh in range(hg):
        prep_head(hh)

    def scan(first, n):
        def body(i, carry):
            chains = [(hh, d, first + i if d == 0 else first + n - 1 - i) for hh in range(hg) for d in range(2)]
            s_mats = [s_ref[hh, d] for hh, d, _ in chains]
            res = [_dot(aq_ref[hh, d, ch], s.astype(BF16)) for s, (hh, d, ch) in zip(s_mats, chains)]
            for s, r, (hh, d, ch) in zip(s_mats, res, chains):
                s_ref[hh, d] = s * gl_ref[hh, d, ch][0:1, :] + (b_ref[hh, d, ch] - r[:DN_DK])
                rows = pl.ds(pl.multiple_of(ch * c, c), c)
                o_ref[hh, d, rows, :] = o_ref[hh, d, rows, :] + r[DN_DK:]
            return carry
        return body

    lax.fori_loop(0, ncc, scan(0, ncc), 0, unroll=True)
    lax.fori_loop(0, ncl, scan(ncc, ncl), 0, unroll=4)

    def gated_norm(o, z):
        return (o * lax.rsqrt(jnp.mean(o * o, axis=-1, keepdims=True) + NORM_EPS) * nw_ref[...] * _silu(z))

    piece = 4 * c
    for hh in range(hg):
        hs = slice(hh * LANES, (hh + 1) * LANES)
        yc_ref[:, hs] = gated_norm(o_ref[hh, 0, 0:tc, :] + o_ref[hh, 1, 0:tc, :], zc_ref[:, hs]).astype(BF16)
        for j in range(t // piece):
            rows = slice(j * piece, (j + 1) * piece)
            orow = slice(tc + j * piece, tc + (j + 1) * piece)
            y_ref[rows, hs] = gated_norm(o_ref[hh, 0, orow, :] + o_ref[hh, 1, orow, :], z_ref[rows, hs]).astype(BF16)


def _dn_call(dn, z, ab, dnc, zc, abc, conv_w, alog, dtb, norm_w):
    bsz, t, _ = dn.shape
    tc = dnc.shape[1]
    h = DN_HEADS
    hg = DN_HEAD_GROUP
    nch = (t + tc) // DN_CHUNK
    c = DN_CHUNK
    wide = hg * LANES

    def slab(rows, j0):
        return pl.BlockSpec((None, rows, wide), lambda b, g: (b, 0, j0 // hg + g))

    def whole(rows):
        return pl.BlockSpec((None, rows, LANES), lambda b, g: (b, 0, 0))

    def cw(j0):
        return pl.BlockSpec((DN_CONV, wide), lambda b, g: (0, j0 // hg + g))

    vec = pl.BlockSpec((1, LANES), lambda b, g: (0, 0))
    return pl.pallas_call(
        functools.partial(_dn_kernel, hg=hg),
        out_shape=[jax.ShapeDtypeStruct((bsz, t, h * DN_DV), BF16), jax.ShapeDtypeStruct((bsz, tc, h * DN_DV), BF16)],
        grid=(bsz, h // hg),
        in_specs=[slab(t, 0), slab(t, h), slab(t, 2 * h), slab(t, 0), whole(t),
                  slab(tc, 0), slab(tc, h), slab(tc, 2 * h), slab(tc, 0), whole(tc),
                  cw(0), cw(h), cw(2 * h), vec, vec, vec],
        out_specs=[slab(t, 0), slab(tc, 0)],
        scratch_shapes=[
            pltpu.VMEM((3, 3 * SUBLANES + tc + t, LANES), F32),
            pltpu.VMEM((tc + t, LANES), F32),
            pltpu.VMEM((hg, 2, nch, DN_DK + c, DN_DV), BF16),
            pltpu.VMEM((hg, 2, nch, DN_DK, DN_DV), F32),
            pltpu.VMEM((hg, 2, nch, SUBLANES, LANES), F32),
            pltpu.VMEM((hg, 2, tc + t, DN_DV), F32),
            pltpu.VMEM((hg, 2, DN_DK, DN_DV), F32),
        ],
        compiler_params=_params("parallel", "parallel"),
        name="deltanet",
    )(dn, dn, dn, z, ab, dnc, dnc, dnc, zc, abc, conv_w, conv_w, conv_w, alog, dtb, norm_w)


def _swa_kernel(sink_ref, q_ref, k_ref, v_ref, kc_ref, vc_ref, o_ref, *, n_blocks):
    blk = SWA_BLOCK
    n_kv_pairs = k_ref.shape[1] // LANES
    slabs = q_ref.shape[1] // LANES // n_kv_pairs
    rows = slabs * blk
    low = lax.broadcasted_iota(jnp.int32, (rows, LANES), 1) < SWA_HD
    low_blk = lax.broadcasted_iota(jnp.int32, (blk, LANES), 1) < SWA_HD
    low1 = lax.broadcasted_iota(jnp.int32, (1, LANES), 1) < SWA_HD
    low16 = jnp.where(low1, 1.0, 0.0).astype(BF16)
    high16 = jnp.where(low1, 0.0, 1.0).astype(BF16)
    qi = lax.broadcasted_iota(jnp.int32, (rows, blk), 0) % blk
    kj = lax.broadcasted_iota(jnp.int32, (rows, blk), 1)

    def block_body(n, carry):
        lo = jnp.maximum(n - 1, 0)
        hi = jnp.minimum(n + 1, n_blocks - 1)
        ok_lo = (kj >= qi) & (n > 0)
        ok_hi = (kj <= qi) & (n < n_blocks - 1)

        def tok(i):
            return pl.ds(pl.multiple_of(i * blk, blk), blk)

        def kv_rows(ref, ctx_ref, kvl):
            return jnp.concatenate([ref[tok(lo), kvl], ref[tok(n), kvl], ref[tok(hi), kvl], ctx_ref[:, kvl]], axis=0)

        def softmax(s, p, half):
            cols = [s[:, j * blk:(j + 1) * blk] for j in range(s.shape[1] // blk)]
            cols[0] = jnp.where(ok_lo, cols[0], NEG_INF)
            cols[2] = jnp.where(ok_hi, cols[2], NEG_INF)
            probs, e_sink = [], []
            for j in range(slabs):
                cj = [c[j * blk:(j + 1) * blk] for c in cols]
                sink = sink_ref[(2 * p + half) * slabs + j] * LOG2E
                m = jnp.maximum(jnp.max(functools.reduce(jnp.maximum, cj), axis=-1, keepdims=True), sink)
                probs.append(jnp.concatenate([jnp.exp2(c - m).astype(BF16) for c in cj], axis=1))
                e_sink.append(jnp.exp2(sink - m))
            return jnp.concatenate(probs, axis=0), e_sink

        units = [(p, half) for p in range(n_kv_pairs) for half in range(2)]
        kvls = [slice(p * LANES, (p + 1) * LANES) for p in range(n_kv_pairs)]
        k_all = [kv_rows(k_ref, kc_ref, kvl) for kvl in kvls]
        v_all = [kv_rows(v_ref, vc_ref, kvl) for kvl in kvls]
        qs = [jnp.concatenate([q_ref[tok(n), (p * slabs + j) * LANES:(p * slabs + j + 1) * LANES]
                               for j in range(slabs)], axis=0) for p in range(n_kv_pairs)]
        s = [_dot_nt(qs[p] * (high16 if half else low16), k_all[p]) for p, half in units]
        pe = [softmax(x, p, half) for x, (p, half) in zip(s, units)]
        r = [_dot(x[0], v_all[p] * (high16 if half else low16) + (low16 if half else high16))
             for x, (p, half) in zip(pe, units)]
        res = [jnp.concatenate(
            [y[j * blk:(j + 1) * blk] + jnp.where(low_blk if half else ~low_blk, x[1][j], 0.0) for j in range(slabs)],
            axis=0) for y, x, (p, half) in zip(r, pe, units)]
        for p in range(n_kv_pairs):
            num = jnp.where(low, res[2 * p], res[2 * p + 1])
            den = pltpu.roll(jnp.where(low, res[2 * p + 1], res[2 * p]), SWA_HD, 1)
            o = (num / den).astype(BF16)
            for j in range(slabs):
                o_ref[tok(n), (p * slabs + j) * LANES:(p * slabs + j + 1) * LANES] = o[j * blk:(j + 1) * blk]
        return carry

    lax.fori_loop(0, n_blocks, block_body, 0, unroll=4)


def _swa_call(sink, q, k, v, kc, vc):
    bsz, t, wq = q.shape
    wk = k.shape[2]
    tc = kc.shape[1]
    return pl.pallas_call(
        functools.partial(_swa_kernel, n_blocks=t // SWA_BLOCK),
        out_shape=jax.ShapeDtypeStruct((bsz, t, wq), BF16),
        grid=(bsz,),
        in_specs=[
            pl.BlockSpec(memory_space=pltpu.SMEM),
            pl.BlockSpec((None, t, wq), lambda b: (b, 0, 0)),
            pl.BlockSpec((None, t, wk), lambda b: (b, 0, 0)),
            pl.BlockSpec((None, t, wk), lambda b: (b, 0, 0)),
            pl.BlockSpec((None, tc, wk), lambda b: (b, 0, 0)),
            pl.BlockSpec((None, tc, wk), lambda b: (b, 0, 0)),
        ],
        out_specs=pl.BlockSpec((None, t, wq), lambda b: (b, 0, 0)),
        compiler_params=_params("parallel"),
        name="swa_attn",
    )(sink, q, k, v, kc, vc)


def _swa_slab_order(w, axis):
    rep = SWA_HEADS // SWA_KV_HEADS
    shape = w.shape
    split = shape[:axis] + (SWA_KV_HEADS // 2, 2, rep, SWA_HD) + shape[axis + 1:]
    return jnp.swapaxes(w.reshape(split), axis + 1, axis + 2).reshape(shape)


def _rope_tables(t):
    pos = jnp.arange(t)
    half = SWA_HD // 2
    inv = jnp.power(ROPE_BASE, -jnp.arange(0, half, 2, dtype=F32) / half)
    ang_r = (pos // GRID_W).astype(F32)[:, None] * inv
    ang_c = (pos % GRID_W).astype(F32)[:, None] * inv
    cos = jnp.concatenate([jnp.cos(ang_r)] * 2 + [jnp.cos(ang_c)] * 2, axis=-1)
    sin = jnp.concatenate([-jnp.sin(ang_r), jnp.sin(ang_r), -jnp.sin(ang_c), jnp.sin(ang_c)], axis=-1)
    return jnp.tile(cos, (1, LANES // SWA_HD)), jnp.tile(sin, (1, LANES // SWA_HD))


def kernel(x, c, ctx, c_ctx, ada_w, ada_b, norm_g, ffn_w_up, ffn_w_down, even_w_in, even_w_out, na_rpb,
           dn_conv_w, dn_a_log, dn_dt_bias, dn_norm_w, odd_w_in, odd_w_out, swa_sink, final_norm_g):
    bsz, t, d = x.shape
    tc = ctx.shape[1]
    depth = ada_w.shape[0]
    ctx_row = bsz

    c16 = jnp.concatenate([c, c_ctx[None, :], jnp.zeros((MOD_ROWS - bsz - 1, d), F32)], axis=0)
    mods = _adaln_call(c16, ada_w, ada_b).reshape(depth, N_MOD, MOD_ROWS, 1, d)
    norm_g3 = norm_g.reshape(depth * 3, 1, d)

    w_up = ffn_w_up.astype(BF16)
    f = ffn_w_down.shape[2]
    w_down = ffn_w_down.reshape(depth, 2, f // FF_CHUNK, FF_CHUNK, d)

    h = x
    hc = ctx.reshape(1, bsz * tc, d)
    for i in range(depth):
        need_ctx = i < depth - 1
        j = i // 2
        h = _ffn_call(h, mods, norm_g3, i, 0, None, w_up, w_down)
        hc = _ffn_call(hc, mods, norm_g3, i, 0, ctx_row, w_up, w_down)
        mix_c = None
        if i % 2 == 0:
            n_main = NA_PROJ + DN_QKV + DN_HEADS * DN_DV
            w_in = even_w_in[j][:, :n_main].astype(BF16)
            w_ab = even_w_in[j][:, n_main:]
            w_ab = jnp.pad(w_ab, ((0, 0), (0, LANES - w_ab.shape[1]))).astype(BF16)
            na, dn, z, ab = _proj_even_call(h, mods, norm_g3, i, None, w_in, w_ab)
            nac, dnc, zc, abc = [a.reshape(bsz, tc, a.shape[-1])
                                 for a in _proj_even_call(hc, mods, norm_g3, i, ctx_row, w_in, w_ab)]
            y_na, y_na_c = _na_call(na, nac, _na_bias_table(na_rpb[j]))
            lane_pad = (0, LANES - 2 * DN_HEADS)
            alog = jnp.pad(dn_a_log[j].reshape(-1), lane_pad)[None, :]
            dtb = jnp.pad(dn_dt_bias[j].reshape(-1), lane_pad)[None, :]
            y_dn, y_dn_c = _dn_call(dn, z, ab, dnc, zc, abc, dn_conv_w[j], alog, dtb, dn_norm_w[j][None, :])
            w_out = even_w_out[j].astype(BF16)
            ws = [w_out[:NA_HEADS * NA_HD], w_out[NA_HEADS * NA_HD:]]
            mix = ([y_na, y_dn], ws)
            if need_ctx:
                mix_c = ([y.reshape(1, bsz * tc, y.shape[-1]) for y in (y_na_c, y_dn_c)], ws)
        else:
            if need_ctx:
                raise NotImplementedError("context queries of a windowed layer are only needed before a later layer")
            w_in = odd_w_in[j]
            qw = SWA_HEADS * SWA_HD
            kw = SWA_KV_HEADS * SWA_HD
            wq = _swa_slab_order(w_in[:, :qw].astype(BF16), 1)
            wk = w_in[:, qw:qw + kw].astype(BF16)
            wv = w_in[:, qw + kw:].astype(BF16)
            cos, sin = _rope_tables(t)
            q, k, v = _proj_odd_call(h, mods, norm_g3, i, None, wq, wk, wv, cos, sin)
            kc, vc = [a.reshape(bsz, tc, kw) for a in _proj_odd_call(hc, mods, norm_g3, i, ctx_row, None, wk, wv, None, None)]
            mix = ([_swa_call(swa_sink[j], q, k, v, kc, vc)], [_swa_slab_order(odd_w_out[j].astype(BF16), 0)])
        last = i == depth - 1
        h = _ffn_call(h, mods, norm_g3, i, 1, None, w_up, w_down, mix=mix,
                      final_g=final_norm_g[None, :] if last else None)
        if need_ctx:
            hc = _ffn_call(hc, mods, norm_g3, i, 1, ctx_row, w_up, w_down, mix=mix_c)
    return h
```

```python
import functools
import math

import jax
import jax.numpy as jnp
from jax import lax
from jax.experimental import pallas as pl
from jax.experimental.pallas import tpu as pltpu

F32 = jnp.float32
BF16 = jnp.bfloat16

GRID_W = 64
N_MOD = 9
NORM_EPS = 1e-6
NEG_INF = -1e30
LOG2E = math.log2(math.e)
ROPE_BASE = 10000.0
NA_HEADS = 8
NA_HD = 64
NA_WIN_H = 8
NA_WIN_W = 16
NA_PROJ = 3 * NA_HEADS * NA_HD
DN_HEADS = 4
DN_DK = 128
DN_DV = 128
DN_CONV = 5
DN_CHUNK = 64
DN_QKV = DN_HEADS * (2 * DN_DK + DN_DV)
SWA_HEADS = 16
SWA_KV_HEADS = 4
SWA_HD = 64
SWA_BLOCK = 128

LANES = 128
SUBLANES = 8
VMEM_LIMIT = 56 * 1024 * 1024

TOK_TILE = 1024
FF_CHUNK = 256
MOD_ROWS = 16
DN_HEAD_GROUP = 2
NA_ROWS_PER_STEP = 2
DN_PREP_CHUNKS = 9


def _params(*sem):
    return pltpu.CompilerParams(dimension_semantics=sem, vmem_limit_bytes=VMEM_LIMIT)


def _dot(a, b):
    return jnp.dot(a, b, preferred_element_type=F32)


def _dot_nt(a, b):
    return lax.dot_general(a, b, (((1,), (1,)), ((), ())), preferred_element_type=F32)


def _dot_tn(a, b):
    return lax.dot_general(a, b, (((0,), (0,)), ((), ())), preferred_element_type=F32)


def _sigmoid(x):
    return 0.5 + 0.5 * jnp.tanh(0.5 * x)


def _silu(x):
    h = 0.5 * x
    return h + h * jnp.tanh(h)


def _softplus(x):
    return jnp.maximum(x, 0.0) + jnp.log1p(jnp.exp(-jnp.abs(x)))


def _modulate(x, g, shift, scale):
    y = x * lax.rsqrt(jnp.mean(x * x, axis=-1, keepdims=True) + NORM_EPS) * g
    return y * (1.0 + scale) + shift


def _split3(x):
    h1 = x.astype(BF16)
    r1 = x - h1.astype(F32)
    h2 = r1.astype(BF16)
    h3 = (r1 - h2.astype(F32)).astype(BF16)
    return h1, h2, h3


def _split2(x):
    hi = x.astype(BF16)
    return hi, (x - hi.astype(F32)).astype(BF16)


def _adaln_kernel(c_ref, w_ref, b_ref, o_ref):
    s = _silu(c_ref[...]).astype(BF16)
    o_ref[...] = _dot(s, w_ref[...].astype(BF16)) + b_ref[...]


def _adaln_call(c16, ada_w, ada_b):
    depth, d, _ = ada_w.shape
    b4 = ada_b.reshape(depth * N_MOD, 1, d)
    return pl.pallas_call(
        _adaln_kernel,
        out_shape=jax.ShapeDtypeStruct((depth, N_MOD, MOD_ROWS, d), F32),
        grid=(depth, N_MOD),
        in_specs=[
            pl.BlockSpec((MOD_ROWS, d), lambda i, k: (0, 0)),
            pl.BlockSpec((None, d, d), lambda i, k: (i, 0, k)),
            pl.BlockSpec((None, 1, d), lambda i, k: (i * N_MOD + k, 0, 0)),
        ],
        out_specs=pl.BlockSpec((None, None, MOD_ROWS, d), lambda i, k: (i, k, 0, 0)),
        compiler_params=_params("arbitrary", "arbitrary"),
        name="adaln",
    )(c16, ada_w, b4)


def _mod_spec(layer, k, ctx_row, d):
    if ctx_row is None:
        return pl.BlockSpec((None, None, None, 1, d), lambda b, t: (layer, k, b, 0, 0))
    return pl.BlockSpec((None, None, None, 1, d), lambda b, t: (layer, k, ctx_row, 0, 0))


def _const_spec(shape):
    nd = len(shape)
    return pl.BlockSpec(shape, lambda b, t: (0,) * nd, pipeline_mode=pl.Buffered(1))


def _ffn_kernel(x_ref, g_ref, sh_ref, sc_ref, gt_ref, wup_ref, wd_ref, *rest, n_mix, final):
    if n_mix:
        mg_ref, rest = rest[0], rest[1:]
        y_refs, w_refs, rest = rest[:n_mix], rest[n_mix:2 * n_mix], rest[2 * n_mix:]
    if final:
        fg_ref, rest = rest[0], rest[1:]
    o_ref, u_ref, acc_ref = rest
    n_chunks, fc, _ = wd_ref.shape
    f = n_chunks * fc

    x = x_ref[...]
    if n_mix:
        mix = _dot(y_refs[0][...], w_refs[0][...])
        for y_ref, w_ref in zip(y_refs[1:], w_refs[1:]):
            mix += _dot(y_ref[...], w_ref[...])
        x = x + mg_ref[...] * mix
        o_ref[...] = x
    u_ref[...] = _modulate(x, g_ref[...], sh_ref[...], sc_ref[...]).astype(BF16)
    acc_ref[...] = jnp.zeros_like(acc_ref)

    def body(c, carry):
        u = u_ref[...]
        off = pl.multiple_of(c * fc, fc)
        gate = _dot(u, wup_ref[:, pl.ds(off, fc)])
        up = _dot(u, wup_ref[:, pl.ds(f + off, fc)])
        a = (_silu(gate) * up).astype(BF16)
        acc_ref[...] += _dot(a, wd_ref[c].astype(BF16))
        return carry

    lax.fori_loop(0, n_chunks, body, 0, unroll=True)
    y = (o_ref[...] if n_mix else x_ref[...]) + 0.5 * gt_ref[...] * acc_ref[...]
    if final:
        y = y * lax.rsqrt(jnp.mean(y * y, axis=-1, keepdims=True) + NORM_EPS) * fg_ref[...]
    o_ref[...] = y


def _ffn_call(h, mods, norm_g3, layer, which, ctx_row, w_up, w_down, mix=None, final_g=None):
    bsz, t, d = h.shape
    tm = min(TOK_TILE, t)
    k0 = 6 if which else 0
    in_specs = [
        pl.BlockSpec((None, tm, d), lambda b, i: (b, i, 0)),
        pl.BlockSpec((None, 1, d), lambda b, i: (layer * 3 + (2 if which else 0), 0, 0)),
        _mod_spec(layer, k0, ctx_row, d),
        _mod_spec(layer, k0 + 1, ctx_row, d),
        _mod_spec(layer, k0 + 2, ctx_row, d),
        pl.BlockSpec((None, None) + w_up.shape[2:], lambda b, i: (layer, which, 0, 0), pipeline_mode=pl.Buffered(1)),
        pl.BlockSpec((None, None) + w_down.shape[2:], lambda b, i: (layer, which, 0, 0, 0),
                     pipeline_mode=pl.Buffered(1)),
    ]
    args = [h, norm_g3, mods, mods, mods, w_up, w_down]
    n_mix = 0
    if mix is not None:
        ys, ws = mix
        n_mix = len(ys)
        in_specs.append(_mod_spec(layer, 5, ctx_row, d))
        in_specs += [pl.BlockSpec((None, tm, y.shape[2]), lambda b, i: (b, i, 0)) for y in ys]
        in_specs += [_const_spec(w.shape) for w in ws]
        args += [mods, *ys, *ws]
    if final_g is not None:
        in_specs.append(pl.BlockSpec((1, d), lambda b, i: (0, 0)))
        args.append(final_g)
    return pl.pallas_call(
        functools.partial(_ffn_kernel, n_mix=n_mix, final=final_g is not None),
        out_shape=jax.ShapeDtypeStruct(h.shape, F32),
        grid=(bsz, t // tm),
        in_specs=in_specs,
        out_specs=pl.BlockSpec((None, tm, d), lambda b, i: (b, i, 0)),
        scratch_shapes=[pltpu.VMEM((tm, d), BF16), pltpu.VMEM((tm, d), F32)],
        compiler_params=_params("parallel", "parallel"),
        name="ffn",
    )(*args)


def _proj_even_kernel(x_ref, g_ref, sh_ref, sc_ref, w_ref, na_ref, dn_ref, z_ref, ab_ref):
    u = _modulate(x_ref[...], g_ref[...], sh_ref[...], sc_ref[...]).astype(BF16)
    c0 = 0
    for o_ref in (na_ref, dn_ref, z_ref, ab_ref):
        c1 = c0 + o_ref.shape[1]
        y = _dot(u, w_ref[:, c0:c1])
        if o_ref is na_ref:
            qw = NA_HEADS * NA_HD
            o_ref[:, :qw] = (y[:, :qw] * (NA_HD ** -0.5 * LOG2E)).astype(BF16)
            o_ref[:, qw:] = y[:, qw:].astype(BF16)
        else:
            o_ref[...] = y
        c0 = c1


def _proj_even_call(h, mods, norm_g3, layer, ctx_row, w_in):
    bsz, t, d = h.shape
    tm = min(TOK_TILE, t)
    widths = (NA_PROJ, DN_QKV, DN_HEADS * DN_DV, LANES)
    dtypes = (BF16, F32, F32, F32)
    return pl.pallas_call(
        _proj_even_kernel,
        out_shape=[jax.ShapeDtypeStruct((bsz, t, w), dt) for w, dt in zip(widths, dtypes)],
        grid=(bsz, t // tm),
        in_specs=[
            pl.BlockSpec((None, tm, d), lambda b, i: (b, i, 0)),
            pl.BlockSpec((None, 1, d), lambda b, i: (layer * 3 + 1, 0, 0)),
            _mod_spec(layer, 3, ctx_row, d),
            _mod_spec(layer, 4, ctx_row, d),
            _const_spec(w_in.shape),
        ],
        out_specs=[pl.BlockSpec((None, tm, w), lambda b, i: (b, i, 0)) for w in widths],
        compiler_params=_params("parallel", "parallel"),
        name="proj_even",
    )(h, norm_g3, mods, mods, w_in)


def _rope_slab(x, cos, sin, first):
    swapped = jnp.where(first, pltpu.roll(x, LANES - 16, 1), pltpu.roll(x, 16, 1))
    return x * cos + swapped * sin


def _proj_odd_kernel(x_ref, g_ref, sh_ref, sc_ref, *rest, with_q):
    if with_q:
        wq_ref, wk_ref, wv_ref, cos_ref, sin_ref, q_ref, k_ref, v_ref = rest
    else:
        wk_ref, wv_ref, k_ref, v_ref = rest
    u = _modulate(x_ref[...], g_ref[...], sh_ref[...], sc_ref[...]).astype(BF16)
    v_ref[...] = _dot(u, wv_ref[...]).astype(BF16)
    k = _dot(u, wk_ref[...])
    if not with_q:
        k_ref[...] = k.astype(BF16)
        return
    cos = cos_ref[...]
    sin = sin_ref[...]
    lane = lax.broadcasted_iota(jnp.int32, cos.shape, 1)
    first = (lane % 32) < 16
    for j in range(k.shape[1] // LANES):
        sl = slice(j * LANES, (j + 1) * LANES)
        k_ref[:, sl] = _rope_slab(k[:, sl], cos, sin, first).astype(BF16)
    q = _dot(u, wq_ref[...])
    for j in range(q.shape[1] // LANES):
        sl = slice(j * LANES, (j + 1) * LANES)
        q_ref[:, sl] = (_rope_slab(q[:, sl], cos, sin, first) * (SWA_HD ** -0.5 * LOG2E)).astype(BF16)


def _proj_odd_call(h, mods, norm_g3, layer, ctx_row, wq, wk, wv, cos, sin):
    bsz, t, d = h.shape
    tm = min(TOK_TILE, t)
    with_q = wq is not None
    in_specs = [
        pl.BlockSpec((None, tm, d), lambda b, i: (b, i, 0)),
        pl.BlockSpec((None, 1, d), lambda b, i: (layer * 3 + 1, 0, 0)),
        _mod_spec(layer, 3, ctx_row, d),
        _mod_spec(layer, 4, ctx_row, d),
    ]
    args = [h, norm_g3, mods, mods]
    widths = []
    if with_q:
        in_specs.append(_const_spec(wq.shape))
        args.append(wq)
        widths.append(wq.shape[1])
    in_specs += [_const_spec(wk.shape), _const_spec(wv.shape)]
    args += [wk, wv]
    widths += [wk.shape[1], wv.shape[1]]
    if with_q:
        in_specs += [pl.BlockSpec((tm, LANES), lambda b, i: (i, 0))] * 2
        args += [cos, sin]
    return pl.pallas_call(
        functools.partial(_proj_odd_kernel, with_q=with_q),
        out_shape=[jax.ShapeDtypeStruct((bsz, t, w), BF16) for w in widths],
        grid=(bsz, t // tm),
        in_specs=in_specs,
        out_specs=[pl.BlockSpec((None, tm, w), lambda b, i: (b, i, 0)) for w in widths],
        compiler_params=_params("parallel", "parallel"),
        name="proj_odd",
    )(*args)


def _softmax2(cols):
    m = jnp.max(functools.reduce(jnp.maximum, cols), axis=-1, keepdims=True)
    p = [jnp.exp2(c - m) for c in cols]
    den = jnp.sum(functools.reduce(jnp.add, p), axis=-1, keepdims=True)
    return jnp.concatenate([x.astype(BF16) for x in p], axis=1), den


def _na_kernel(q_ref, k_ref, v_ref, qc_ref, kc_ref, vc_ref, bias_ref, o_ref, oc_ref, *, rows):
    n_pairs = q_ref.shape[1] // LANES
    tc = qc_ref.shape[0]
    win = NA_WIN_H * GRID_W
    n_bias = win // LANES
    low1 = lax.broadcasted_iota(jnp.int32, (1, LANES), 1) < NA_HD
    low16 = jnp.where(low1, 1.0, 0.0).astype(BF16)
    high16 = jnp.where(low1, 0.0, 1.0).astype(BF16)
    low_q = lax.broadcasted_iota(jnp.int32, (GRID_W, LANES), 1) < NA_HD
    low_c = lax.broadcasted_iota(jnp.int32, (tc, LANES), 1) < NA_HD
    pairs = [slice(hp * LANES, (hp + 1) * LANES) for hp in range(n_pairs)]

    def row_body(i, carry):
        units = []
        for rr in range(NA_ROWS_PER_STEP):
            r = i * NA_ROWS_PER_STEP + rr
            r0 = jnp.clip(r - NA_WIN_H // 2, 0, rows - NA_WIN_H)
            doff = r0 - r + NA_WIN_H - 1
            qs = pl.ds(pl.multiple_of(r * GRID_W, GRID_W), GRID_W)
            ks = pl.ds(pl.multiple_of(r0 * GRID_W, GRID_W), win)
            units += [(hp, sl, doff, qs, ks) for hp, sl in enumerate(pairs)]
        qm = [jnp.concatenate([q_ref[qs, sl] * low16, q_ref[qs, sl] * high16], axis=0) for _, sl, _, qs, _ in units]
        s = [_dot_nt(x, jnp.concatenate([k_ref[ks, sl], kc_ref[:, sl]], axis=0))
             for x, (_, sl, _, _, ks) in zip(qm, units)]
        cols = [[x[:, j * LANES:(j + 1) * LANES] + bias_ref[hp, doff + 2 * j] for j in range(n_bias)]
                + [x[:, j * LANES:(j + 1) * LANES] for j in range(n_bias, x.shape[1] // LANES)]
                for x, (hp, _, doff, _, _) in zip(s, units)]
        pd = [_softmax2(c) for c in cols]
        o = [_dot(p, jnp.concatenate([v_ref[ks, sl], vc_ref[:, sl]], axis=0)) / den
             for (p, den), (_, sl, _, _, ks) in zip(pd, units)]
        for x, (_, sl, _, qs, _) in zip(o, units):
            o_ref[qs, sl] = jnp.where(low_q, x[:GRID_W], x[GRID_W:]).astype(BF16)
        return carry

    lax.fori_loop(0, rows // NA_ROWS_PER_STEP, row_body, 0, unroll=8)

    for sl in pairs:
        halves = []
        for keep16 in (low16, high16):
            s = _dot_nt(qc_ref[:, sl] * keep16, kc_ref[:, sl])
            p, den = _softmax2([s[:, j * LANES:(j + 1) * LANES] for j in range(tc // LANES)])
            halves.append(_dot(p, vc_ref[:, sl]) / den)
        oc_ref[:, sl] = jnp.where(low_c, halves[0], halves[1]).astype(BF16)


def _na_call(qkv, qkvc, bias):
    bsz, t, w3 = qkv.shape
    tc = qkvc.shape[1]
    w = w3 // 3

    def col(j):
        return lambda b: (b, 0, j)

    return pl.pallas_call(
        functools.partial(_na_kernel, rows=t // GRID_W),
        out_shape=[jax.ShapeDtypeStruct((bsz, t, w), BF16), jax.ShapeDtypeStruct((bsz, tc, w), BF16)],
        grid=(bsz,),
        in_specs=[pl.BlockSpec((None, t, w), col(j)) for j in range(3)]
        + [pl.BlockSpec((None, tc, w), col(j)) for j in range(3)]
        + [pl.BlockSpec(bias.shape, lambda b: (0, 0, 0, 0), pipeline_mode=pl.Buffered(1))],
        out_specs=[pl.BlockSpec((None, t, w), lambda b: (b, 0, 0)), pl.BlockSpec((None, tc, w), lambda b: (b, 0, 0))],
        compiler_params=_params("parallel"),
        name="na_attn",
    )(qkv, qkv, qkv, qkvc, qkvc, qkvc, bias)


def _na_bias_table(rpb):
    c = jnp.arange(GRID_W)[:, None]
    kc = jnp.arange(GRID_W)[None, :]
    cstart = jnp.clip(c - NA_WIN_W // 2, 0, GRID_W - NA_WIN_W)
    ok = (kc >= cstart) & (kc < cstart + NA_WIN_W)
    dc = jnp.clip(kc - c + NA_WIN_W - 1, 0, 2 * NA_WIN_W - 2)
    onehot = (dc[None] == jnp.arange(2 * NA_WIN_W - 1)[:, None, None]).astype(F32)
    h = rpb.shape[0]
    nd = 2 * NA_WIN_H - 2
    rp = (rpb.astype(F32) * LOG2E).reshape(h // 2, 2, 2 * NA_WIN_H - 1, 2 * NA_WIN_W - 1)
    t = jnp.einsum('phdx,xck->pdhck', rp, onehot, precision=lax.Precision.HIGHEST)
    t = jnp.where(ok, t, NEG_INF).reshape(h // 2, 2 * NA_WIN_H - 1, 2 * GRID_W, GRID_W)
    return jnp.concatenate([t[:, :nd], t[:, 1:nd + 1]], axis=-1)


def _dn_kernel(qr_ref, kr_ref, vr_ref, z_ref, ab_ref, qcr_ref, kcr_ref, vcr_ref, zc_ref, abc_ref,
               cwq_ref, cwk_ref, cwv_ref, alog_ref, dtb_ref, nw_ref,
               y_ref, yc_ref,
               pad_ref, abs_ref, aq_ref, b_ref, gl_ref, o_ref, s_ref, *, hg):
    c = DN_CHUNK
    t = qr_ref.shape[0]
    tc = qcr_ref.shape[0]
    ncc, ncl = tc // c, t // c
    lat0 = 2 * SUBLANES + tc
    half = DN_CONV // 2
    n_sq = int(math.log2(c)) - 1

    zeros8 = jnp.zeros((3, SUBLANES, LANES), F32)
    pad_ref[:, 0:SUBLANES, :] = zeros8
    pad_ref[:, SUBLANES + tc:lat0, :] = zeros8
    pad_ref[:, lat0 + t:lat0 + t + SUBLANES, :] = zeros8
    abs_ref[0:tc, :] = abc_ref[...]
    abs_ref[tc:tc + t, :] = ab_ref[...]
    s_ref[...] = jnp.zeros_like(s_ref)

    lane = lax.broadcasted_iota(jnp.int32, (c, LANES), 1)
    row = lax.broadcasted_iota(jnp.int32, (c, LANES), 0)
    col = jnp.bitwise_and(lane, c - 1)
    fwd = lane < c
    incl = (fwd & (row >= col)) | (~fwd & (row <= col))
    strict = (fwd & (row > col)) | (~fwd & (row < col))
    diag = row == col
    eye2 = jnp.where(diag, 1.0, 0.0).astype(F32)
    fwd16 = jnp.where(fwd, 1.0, 0.0).astype(BF16)
    bwd16 = jnp.where(fwd, 0.0, 1.0).astype(BF16)
    ri = lax.broadcasted_iota(jnp.int32, (2 * c, c), 0)
    ci_ = lax.broadcasted_iota(jnp.int32, (2 * c, c), 1)
    tri2 = jnp.where(((ri < c) & (ci_ <= ri)) | ((ri >= c) & (ci_ >= ri - c)), 1.0, 0.0).astype(BF16)
    tri6 = jnp.concatenate([tri2, tri2, tri2], axis=1)
    neg_a = -jnp.exp(alog_ref[...])
    dtb = dtb_ref[...]

    def l2n(x):
        return x * lax.rsqrt(jnp.sum(x * x, axis=-1, keepdims=True) + NORM_EPS)

    def bd(m16):
        return jnp.concatenate([m16 * fwd16, m16 * bwd16], axis=0)

    def mm3(lhs_parts, bh, bl):
        bdh = bd(bh)
        rhs = jnp.concatenate([bdh, bdh, bd(bl)], axis=0)
        lhs = jnp.concatenate([jnp.concatenate([ah, al, ah], axis=1) for ah, al in lhs_parts], axis=0)
        return _dot(lhs, rhs)

    def prep_head(hh):
        hid = pl.program_id(1) * hg + hh
        hs = slice(hh * LANES, (hh + 1) * LANES)
        for s, (cr, lr) in enumerate(((qcr_ref, qr_ref), (kcr_ref, kr_ref), (vcr_ref, vr_ref))):
            pad_ref[s, SUBLANES:SUBLANES + tc, :] = cr[:, hs]
            pad_ref[s, lat0:lat0 + t, :] = lr[:, hs]

        def conv(s, cw_ref, base):
            acc = pad_ref[s, pl.ds(base - half, c), :] * cw_ref[0:1, hs]
            for j in range(1, DN_CONV):
                acc += pad_ref[s, pl.ds(base - half + j, c), :] * cw_ref[j:j + 1, hs]
            return _silu(acc)

        def pick(x, idx):
            return jnp.broadcast_to(jnp.sum(jnp.where(lane == idx, x, 0.0), axis=-1, keepdims=True), (c, LANES))

        def stage_inputs(ch):
            base = ch * c + jnp.where(ch < ncc, SUBLANES, 2 * SUBLANES)
            q = l2n(conv(0, cwq_ref, base))
            k = l2n(conv(1, cwk_ref, base))
            v = conv(2, cwv_ref, base)
            abt = abs_ref[pl.ds(pl.multiple_of(ch * c, c), c), :]
            g_all = neg_a * _softplus(abt + dtb)
            b_all = _sigmoid(abt)
            return q * DN_DK ** -0.5, k, v, g_all, b_all

        def stage_decay(g_all):
            parts = []
            for gp in _split3(g_all):
                gp = gp.astype(F32)
                pf = pick(gp, hid)
                pb = pick(gp, DN_HEADS + hid)
                parts.append(jnp.concatenate([jnp.where(strict, jnp.where(fwd, pf, pb), 0.0), pf, pb], axis=1).astype(BF16))
            res = _dot(tri6, jnp.concatenate(parts, axis=0))
            return jnp.where(fwd, res[:c, 0:LANES], res[c:, 0:LANES]), res[:c, LANES:2 * LANES], res[c:, 2 * LANES:]

        def stage_rhs(qs, k, v, gc_f, gc_b, bt_f, bt_b):
            e_f = jnp.exp(gc_f)
            e_b = jnp.exp(gc_b)
            rhs = jnp.concatenate([
                jnp.concatenate([v * bt_f, k * (bt_f * e_f)], axis=1),
                jnp.concatenate([v * bt_b, k * (bt_b * e_b)], axis=1)], axis=0).astype(BF16)
            gl_f = gc_f[c - 1:c, :]
            gl_b = gc_b[0:1, :]
            kd = jnp.concatenate([k * jnp.exp(gl_f - gc_f), k * jnp.exp(gl_b - gc_b)], axis=0).astype(BF16)
            return rhs, kd, (qs * e_f, qs * e_b), (jnp.exp(gl_f), jnp.exp(gl_b))

        def stage_solve(p, rhs):
            sol16 = _dot(bd(p.astype(BF16)), rhs).astype(BF16)
            z16 = jnp.zeros((c, 2 * DN_DV), BF16)
            return jnp.concatenate([jnp.concatenate([sol16[:c], z16], axis=1),
                                    jnp.concatenate([z16, sol16[c:]], axis=1)], axis=0)

        def stage_store(ch, qsol, ksol, q_dec, egl):
            rows = pl.ds(pl.multiple_of(ch * c, c), c)
            for d in range(2):
                lo = 2 * DN_DV * d
                aq_ref[hh, d, ch, 0:DN_DK, :] = ksol[:, lo + DN_DV:lo + 2 * DN_DV].astype(BF16)
                aq_ref[hh, d, ch, DN_DK:DN_DK + c, :] = (q_dec[d] - qsol[:, lo + DN_DV:lo + 2 * DN_DV]).astype(BF16)
                b_ref[hh, d, ch] = ksol[:, lo:lo + DN_DV]
                o_ref[hh, d, rows, :] = qsol[:, lo:lo + DN_DV]
                gl_ref[hh, d, ch] = jnp.broadcast_to(egl[d], (SUBLANES, LANES))

        def body(i, carry):
            chs = [i * DN_PREP_CHUNKS + j for j in range(DN_PREP_CHUNKS)]
            ins = [stage_inputs(ch) for ch in chs]
            dec = [stage_decay(x[3]) for x in ins]
            bts = [(pick(x[4], 2 * DN_HEADS + hid), pick(x[4], 3 * DN_HEADS + hid)) for x in ins]
            decay = [jnp.where(incl, jnp.exp(jnp.where(incl, d[0], 0.0)), 0.0) for d in dec]
            k16 = [x[1].astype(BF16) for x in ins]
            kk = [jnp.concatenate([a, a], axis=0) for a in k16]
            qk16 = [(_dot_nt(x[0].astype(BF16), b) * dc).astype(BF16) for x, b, dc in zip(ins, kk, decay)]
            lmat = [jnp.where(strict, jnp.where(fwd, bt[0], bt[1]) * _dot_nt(a, b) * dc, 0.0)
                    for a, b, bt, dc in zip(k16, kk, bts, decay)]
            lsp = [_split2(x) for x in lmat]
            m = [mm3([s], s[0], s[1]) for s in lsp]
            p = [eye2 - x for x in lmat]
            for s in range(n_sq):
                msp = [_split2(x) for x in m]
                psp = [_split2(x) for x in p]
                if s < n_sq - 1:
                    res = [mm3([a, b], b[0], b[1]) for a, b in zip(psp, msp)]
                    p = [a + r[:c] for a, r in zip(p, res)]
                    m = [r[c:] for r in res]
                else:
                    p = [a + mm3([b], e[0], e[1]) for a, b, e in zip(p, psp, msp)]
            fin = [stage_rhs(x[0], x[1], x[2], d[1], d[2], *bt) for x, d, bt in zip(ins, dec, bts)]
            both = [stage_solve(a, f[0]) for a, f in zip(p, fin)]
            qsol = [_dot(a, b) for a, b in zip(qk16, both)]
            ksol = [_dot_tn(f[1], b) for f, b in zip(fin, both)]
            for ch, a, b, f in zip(chs, qsol, ksol, fin):
                stage_store(ch, a, b, f[2], f[3])
            return carry

        lax.fori_loop(0, (ncc + ncl) // DN_PREP_CHUNKS, body, 0, unroll=True)

    for hh in range(hg):
        prep_head(hh)

    def scan(first, n):
        def body(i, carry):
            chains = [(hh, d, first + i if d == 0 else first + n - 1 - i) for hh in range(hg) for d in range(2)]
            s_mats = [s_ref[hh, d] for hh, d, _ in chains]
            res = [_dot(aq_ref[hh, d, ch], s.astype(BF16)) for s, (hh, d, ch) in zip(s_mats, chains)]
            for s, r, (hh, d, ch) in zip(s_mats, res, chains):
                s_ref[hh, d] = s * gl_ref[hh, d, ch][0:1, :] + (b_ref[hh, d, ch] - r[:DN_DK])
                rows = pl.ds(pl.multiple_of(ch * c, c), c)
                o_ref[hh, d, rows, :] = o_ref[hh, d, rows, :] + r[DN_DK:]
            return carry
        return body

    lax.fori_loop(0, ncc, scan(0, ncc), 0, unroll=True)
    lax.fori_loop(0, ncl, scan(ncc, ncl), 0, unroll=4)

    def gated_norm(o, z):
        return (o * lax.rsqrt(jnp.mean(o * o, axis=-1, keepdims=True) + NORM_EPS) * nw_ref[...] * _silu(z))

    piece = 4 * c
    for hh in range(hg):
        hs = slice(hh * LANES, (hh + 1) * LANES)
        yc_ref[:, hs] = gated_norm(o_ref[hh, 0, 0:tc, :] + o_ref[hh, 1, 0:tc, :], zc_ref[:, hs]).astype(BF16)
        for j in range(t // piece):
            rows = slice(j * piece, (j + 1) * piece)
            orow = slice(tc + j * piece, tc + (j + 1) * piece)
            y_ref[rows, hs] = gated_norm(o_ref[hh, 0, orow, :] + o_ref[hh, 1, orow, :], z_ref[rows, hs]).astype(BF16)


def _dn_call(dn, z, ab, dnc, zc, abc, conv_w, alog, dtb, norm_w):
    bsz, t, _ = dn.shape
    tc = dnc.shape[1]
    h = DN_HEADS
    hg = DN_HEAD_GROUP
    nch = (t + tc) // DN_CHUNK
    c = DN_CHUNK
    wide = hg * LANES

    def slab(rows, j0):
        return pl.BlockSpec((None, rows, wide), lambda b, g: (b, 0, j0 // hg + g))

    def whole(rows):
        return pl.BlockSpec((None, rows, LANES), lambda b, g: (b, 0, 0))

    def cw(j0):
        return pl.BlockSpec((DN_CONV, wide), lambda b, g: (0, j0 // hg + g))

    vec = pl.BlockSpec((1, LANES), lambda b, g: (0, 0))
    return pl.pallas_call(
        functools.partial(_dn_kernel, hg=hg),
        out_shape=[jax.ShapeDtypeStruct((bsz, t, h * DN_DV), BF16), jax.ShapeDtypeStruct((bsz, tc, h * DN_DV), BF16)],
        grid=(bsz, h // hg),
        in_specs=[slab(t, 0), slab(t, h), slab(t, 2 * h), slab(t, 0), whole(t),
                  slab(tc, 0), slab(tc, h), slab(tc, 2 * h), slab(tc, 0), whole(tc),
                  cw(0), cw(h), cw(2 * h), vec, vec, vec],
        out_specs=[slab(t, 0), slab(tc, 0)],
        scratch_shapes=[
            pltpu.VMEM((3, 3 * SUBLANES + tc + t, LANES), F32),
            pltpu.VMEM((tc + t, LANES), F32),
            pltpu.VMEM((hg, 2, nch, DN_DK + c, DN_DV), BF16),
            pltpu.VMEM((hg, 2, nch, DN_DK, DN_DV), F32),
            pltpu.VMEM((hg, 2, nch, SUBLANES, LANES), F32),
            pltpu.VMEM((hg, 2, tc + t, DN_DV), F32),
            pltpu.VMEM((hg, 2, DN_DK, DN_DV), F32),
        ],
        compiler_params=_params("parallel", "parallel"),
        name="deltanet",
    )(dn, dn, dn, z, ab, dnc, dnc, dnc, zc, abc, conv_w, conv_w, conv_w, alog, dtb, norm_w)


def _swa_kernel(sink_ref, q_ref, k_ref, v_ref, kc_ref, vc_ref, o_ref, *, n_blocks):
    blk = SWA_BLOCK
    n_kv_pairs = k_ref.shape[1] // LANES
    slabs = q_ref.shape[1] // LANES // n_kv_pairs
    rows = slabs * blk
    low = lax.broadcasted_iota(jnp.int32, (rows, LANES), 1) < SWA_HD
    low_blk = lax.broadcasted_iota(jnp.int32, (blk, LANES), 1) < SWA_HD
    low1 = lax.broadcasted_iota(jnp.int32, (1, LANES), 1) < SWA_HD
    low16 = jnp.where(low1, 1.0, 0.0).astype(BF16)
    high16 = jnp.where(low1, 0.0, 1.0).astype(BF16)
    qi = lax.broadcasted_iota(jnp.int32, (rows, blk), 0) % blk
    kj = lax.broadcasted_iota(jnp.int32, (rows, blk), 1)

    def block_body(n, carry):
        lo = jnp.maximum(n - 1, 0)
        hi = jnp.minimum(n + 1, n_blocks - 1)
        ok_lo = (kj >= qi) & (n > 0)
        ok_hi = (kj <= qi) & (n < n_blocks - 1)

        def tok(i):
            return pl.ds(pl.multiple_of(i * blk, blk), blk)

        def kv_rows(ref, ctx_ref, kvl):
            return jnp.concatenate([ref[tok(lo), kvl], ref[tok(n), kvl], ref[tok(hi), kvl], ctx_ref[:, kvl]], axis=0)

        def softmax(s, p, half):
            cols = [s[:, j * blk:(j + 1) * blk] for j in range(s.shape[1] // blk)]
            cols[0] = jnp.where(ok_lo, cols[0], NEG_INF)
            cols[2] = jnp.where(ok_hi, cols[2], NEG_INF)
            probs, e_sink = [], []
            for j in range(slabs):
                cj = [c[j * blk:(j + 1) * blk] for c in cols]
                sink = sink_ref[(2 * p + half) * slabs + j] * LOG2E
                m = jnp.maximum(jnp.max(functools.reduce(jnp.maximum, cj), axis=-1, keepdims=True), sink)
                probs.append(jnp.concatenate([jnp.exp2(c - m).astype(BF16) for c in cj], axis=1))
                e_sink.append(jnp.exp2(sink - m))
            return jnp.concatenate(probs, axis=0), e_sink

        units = [(p, half) for p in range(n_kv_pairs) for half in range(2)]
        kvls = [slice(p * LANES, (p + 1) * LANES) for p in range(n_kv_pairs)]
        k_all = [kv_rows(k_ref, kc_ref, kvl) for kvl in kvls]
        v_all = [kv_rows(v_ref, vc_ref, kvl) for kvl in kvls]
        qs = [jnp.concatenate([q_ref[tok(n), (p * slabs + j) * LANES:(p * slabs + j + 1) * LANES]
                               for j in range(slabs)], axis=0) for p in range(n_kv_pairs)]
        s = [_dot_nt(qs[p] * (high16 if half else low16), k_all[p]) for p, half in units]
        pe = [softmax(x, p, half) for x, (p, half) in zip(s, units)]
        r = [_dot(x[0], v_all[p] * (high16 if half else low16) + (low16 if half else high16))
             for x, (p, half) in zip(pe, units)]
        res = [jnp.concatenate(
            [y[j * blk:(j + 1) * blk] + jnp.where(low_blk if half else ~low_blk, x[1][j], 0.0) for j in range(slabs)],
            axis=0) for y, x, (p, half) in zip(r, pe, units)]
        for p in range(n_kv_pairs):
            num = jnp.where(low, res[2 * p], res[2 * p + 1])
            den = pltpu.roll(jnp.where(low, res[2 * p + 1], res[2 * p]), SWA_HD, 1)
            o = (num / den).astype(BF16)
            for j in range(slabs):
                o_ref[tok(n), (p * slabs + j) * LANES:(p * slabs + j + 1) * LANES] = o[j * blk:(j + 1) * blk]
        return carry

    lax.fori_loop(0, n_blocks, block_body, 0, unroll=8)


def _swa_call(sink, q, k, v, kc, vc):
    bsz, t, wq = q.shape
    wk = k.shape[2]
    tc = kc.shape[1]
    return pl.pallas_call(
        functools.partial(_swa_kernel, n_blocks=t // SWA_BLOCK),
        out_shape=jax.ShapeDtypeStruct((bsz, t, wq), BF16),
        grid=(bsz,),
        in_specs=[
            pl.BlockSpec(memory_space=pltpu.SMEM),
            pl.BlockSpec((None, t, wq), lambda b: (b, 0, 0)),
            pl.BlockSpec((None, t, wk), lambda b: (b, 0, 0)),
            pl.BlockSpec((None, t, wk), lambda b: (b, 0, 0)),
            pl.BlockSpec((None, tc, wk), lambda b: (b, 0, 0)),
            pl.BlockSpec((None, tc, wk), lambda b: (b, 0, 0)),
        ],
        out_specs=pl.BlockSpec((None, t, wq), lambda b: (b, 0, 0)),
        compiler_params=_params("parallel"),
        name="swa_attn",
    )(sink, q, k, v, kc, vc)


def _swa_slab_order(w, axis):
    rep = SWA_HEADS // SWA_KV_HEADS
    shape = w.shape
    split = shape[:axis] + (SWA_KV_HEADS // 2, 2, rep, SWA_HD) + shape[axis + 1:]
    return jnp.swapaxes(w.reshape(split), axis + 1, axis + 2).reshape(shape)


def _rope_tables(t):
    pos = jnp.arange(t)
    half = SWA_HD // 2
    inv = jnp.power(ROPE_BASE, -jnp.arange(0, half, 2, dtype=F32) / half)
    ang_r = (pos // GRID_W).astype(F32)[:, None] * inv
    ang_c = (pos % GRID_W).astype(F32)[:, None] * inv
    cos = jnp.concatenate([jnp.cos(ang_r)] * 2 + [jnp.cos(ang_c)] * 2, axis=-1)
    sin = jnp.concatenate([-jnp.sin(ang_r), jnp.sin(ang_r), -jnp.sin(ang_c), jnp.sin(ang_c)], axis=-1)
    return jnp.tile(cos, (1, LANES // SWA_HD)), jnp.tile(sin, (1, LANES // SWA_HD))


def kernel(x, c, ctx, c_ctx, ada_w, ada_b, norm_g, ffn_w_up, ffn_w_down, even_w_in, even_w_out, na_rpb,
           dn_conv_w, dn_a_log, dn_dt_bias, dn_norm_w, odd_w_in, odd_w_out, swa_sink, final_norm_g):
    bsz, t, d = x.shape
    tc = ctx.shape[1]
    depth = ada_w.shape[0]
    ctx_row = bsz

    c16 = jnp.concatenate([c, c_ctx[None, :], jnp.zeros((MOD_ROWS - bsz - 1, d), F32)], axis=0)
    mods = _adaln_call(c16, ada_w, ada_b).reshape(depth, N_MOD, MOD_ROWS, 1, d)
    norm_g3 = norm_g.reshape(depth * 3, 1, d)

    w_up = ffn_w_up.astype(BF16)
    f = ffn_w_down.shape[2]
    w_down = ffn_w_down.reshape(depth, 2, f // FF_CHUNK, FF_CHUNK, d)

    h = x
    hc = ctx.reshape(1, bsz * tc, d)
    for i in range(depth):
        need_ctx = i < depth - 1
        j = i // 2
        h = _ffn_call(h, mods, norm_g3, i, 0, None, w_up, w_down)
        hc = _ffn_call(hc, mods, norm_g3, i, 0, ctx_row, w_up, w_down)
        mix_c = None
        if i % 2 == 0:
            w_in = even_w_in[j]
            w_in = jnp.pad(w_in, ((0, 0), (0, -w_in.shape[1] % LANES))).astype(BF16)
            na, dn, z, ab = _proj_even_call(h, mods, norm_g3, i, None, w_in)
            nac, dnc, zc, abc = [a.reshape(bsz, tc, a.shape[-1])
                                 for a in _proj_even_call(hc, mods, norm_g3, i, ctx_row, w_in)]
            y_na, y_na_c = _na_call(na, nac, _na_bias_table(na_rpb[j]))
            lane_pad = (0, LANES - 2 * DN_HEADS)
            alog = jnp.pad(dn_a_log[j].reshape(-1), lane_pad)[None, :]
            dtb = jnp.pad(dn_dt_bias[j].reshape(-1), lane_pad)[None, :]
            y_dn, y_dn_c = _dn_call(dn, z, ab, dnc, zc, abc, dn_conv_w[j], alog, dtb, dn_norm_w[j][None, :])
            w_out = even_w_out[j].astype(BF16)
            ws = [w_out[:NA_HEADS * NA_HD], w_out[NA_HEADS * NA_HD:]]
            mix = ([y_na, y_dn], ws)
            if need_ctx:
                mix_c = ([y.reshape(1, bsz * tc, y.shape[-1]) for y in (y_na_c, y_dn_c)], ws)
        else:
            if need_ctx:
                raise NotImplementedError("context queries of a windowed layer are only needed before a later layer")
            w_in = odd_w_in[j]
            qw = SWA_HEADS * SWA_HD
            kw = SWA_KV_HEADS * SWA_HD
            wq = _swa_slab_order(w_in[:, :qw].astype(BF16), 1)
            wk = w_in[:, qw:qw + kw].astype(BF16)
            wv = w_in[:, qw + kw:].astype(BF16)
            cos, sin = _rope_tables(t)
            q, k, v = _proj_odd_call(h, mods, norm_g3, i, None, wq, wk, wv, cos, sin)
            kc, vc = [a.reshape(bsz, tc, kw) for a in _proj_odd_call(hc, mods, norm_g3, i, ctx_row, None, wk, wv, None, None)]
            mix = ([_swa_call(swa_sink[j], q, k, v, kc, vc)], [_swa_slab_order(odd_w_out[j].astype(BF16), 0)])
        last = i == depth - 1
        h = _ffn_call(h, mods, norm_g3, i, 1, None, w_up, w_down, mix=mix,
                      final_g=final_norm_g[None, :] if last else None)
        if need_ctx:
            hc = _ffn_call(hc, mods, norm_g3, i, 1, ctx_row, w_up, w_down, mix=mix_c)
    return h
```
